```python
import math
import jax, jax.numpy as jnp
from jax import lax
import numpy as np

D_MODEL = 1024
BATCH = 2
SEQ = 16384
DEPTH = 2

N_MIXERS = 2
Q_BLOCK = 128
ROPE_THETA = 10000.0
NORM_EPS = 1e-6

DIFF_HEAD_DIM = 64
DIFF_HEADS = D_MODEL // (2 * DIFF_HEAD_DIM)
DIFF_QK_WIDTH = DIFF_HEADS * 2 * DIFF_HEAD_DIM
DIFF_V_DIM = 2 * DIFF_HEAD_DIM
DIFF_V_WIDTH = DIFF_HEADS * DIFF_V_DIM

MLA_NOPE = 128
MLA_ROPE = 64
MLA_QK_DIM = MLA_NOPE + MLA_ROPE
MLA_V = 128
MLA_HEADS = D_MODEL // MLA_V
MLA_Q_LORA = 3 * D_MODEL // 8
MLA_KV_LORA = D_MODEL // 4
MLA_A_WIDTH = MLA_Q_LORA + MLA_KV_LORA + MLA_ROPE

N_GROUPS = 8
EXPERTS_PER_GROUP = 8
N_EXPERTS = N_GROUPS * EXPERTS_PER_GROUP
TOP_K = 2
EXPERT_FF = D_MODEL // 4

kernel_name = "hybrid_diffattn_mla_hiermoe"


def rms_norm(x, gain):
    xf = x.astype(jnp.float32)
    y = xf * lax.rsqrt(jnp.mean(xf * xf, axis=-1, keepdims=True) + NORM_EPS)
    return (y * gain.astype(jnp.float32)).astype(x.dtype)


def rope(x, pos):
    d = x.shape[-1]
    half = d // 2
    inv_freq = 1.0 / (ROPE_THETA ** (jnp.arange(0, d, 2, dtype=jnp.float32) / d))
    ang = pos.astype(jnp.float32)[:, None] * inv_freq[None, :]
    shape = (pos.shape[0],) + (1,) * (x.ndim - 3) + (half,)
    cos = jnp.cos(ang).reshape(shape)
    sin = jnp.sin(ang).reshape(shape)
    xf = x.astype(jnp.float32)
    x1, x2 = xf[..., :half], xf[..., half:]
    return jnp.concatenate([x1 * cos - x2 * sin, x2 * cos + x1 * sin], axis=-1).astype(x.dtype)


def causal_softmax(scores, blk, seq):
    qpos = blk * Q_BLOCK + jnp.arange(Q_BLOCK)
    mask = jnp.arange(seq)[None, :] <= qpos[:, None]
    return jax.nn.softmax(jnp.where(mask, scores, -jnp.inf), axis=-1)


def query_block_map(fn, *qs):
    b, s = qs[0].shape[:2]
    nb = s // Q_BLOCK
    blocks = tuple(q.reshape(b, nb, Q_BLOCK, *q.shape[2:]).swapaxes(0, 1) for q in qs)
    out = lax.map(lambda a: fn(a[0], *a[1:]), (jnp.arange(nb), *blocks))
    return out.swapaxes(0, 1).reshape(b, s, *out.shape[3:])


def diff_attention(h, w_in, q_gain, k_gain, lq1, lk1, lq2, lk2, subln, w_out, lambda_init, pos):
    b, s, _ = h.shape
    qkv = h @ w_in
    q, k, v = jnp.split(qkv, [DIFF_QK_WIDTH, 2 * DIFF_QK_WIDTH], axis=-1)
    q = rope(rms_norm(q.reshape(b, s, DIFF_HEADS, 2, DIFF_HEAD_DIM), q_gain), pos)
    k = rope(rms_norm(k.reshape(b, s, DIFF_HEADS, 2, DIFF_HEAD_DIM), k_gain), pos)
    v = v.reshape(b, s, DIFF_HEADS, DIFF_V_DIM)
    q1, q2 = q[..., 0, :], q[..., 1, :]
    k1, k2 = k[..., 0, :], k[..., 1, :]
    lam = (jnp.exp(jnp.sum(lq1.astype(jnp.float32) * lk1.astype(jnp.float32)))
           - jnp.exp(jnp.sum(lq2.astype(jnp.float32) * lk2.astype(jnp.float32)))
           + lambda_init)
    scale = DIFF_HEAD_DIM ** -0.5

    def block(blk, qb1, qb2):
        s1 = jnp.einsum('bqhd,bkhd->bhqk', qb1, k1).astype(jnp.float32) * scale
        s2 = jnp.einsum('bqhd,bkhd->bhqk', qb2, k2).astype(jnp.float32) * scale
        a = causal_softmax(s1, blk, s) - lam * causal_softmax(s2, blk, s)
        return jnp.einsum('bhqk,bkhd->bqhd', a.astype(v.dtype), v)

    o = query_block_map(block, q1, q2)
    o = rms_norm(o, subln) * (1.0 - lambda_init)
    return o.reshape(b, s, DIFF_V_WIDTH) @ w_out


def mla_attention(h, w_a, q_a_gain, kv_a_gain, w_qb, w_kvb, q_gain, k_gain, w_out, pos):
    b, s, _ = h.shape
    a = h @ w_a
    cq, ckv, k_pe = jnp.split(a, [MLA_Q_LORA, MLA_Q_LORA + MLA_KV_LORA], axis=-1)
    q = (rms_norm(cq, q_a_gain) @ w_qb).reshape(b, s, MLA_HEADS, MLA_QK_DIM)
    kv = (rms_norm(ckv, kv_a_gain) @ w_kvb).reshape(b, s, MLA_HEADS, MLA_NOPE + MLA_V)
    k_nope, v = kv[..., :MLA_NOPE], kv[..., MLA_NOPE:]
    k_pe = jnp.broadcast_to(k_pe[:, :, None, :], (b, s, MLA_HEADS, MLA_ROPE))
    k = jnp.concatenate([k_nope, k_pe], axis=-1)
    q = rms_norm(q, q_gain)
    k = rms_norm(k, k_gain)
    q = jnp.concatenate([q[..., :MLA_NOPE], rope(q[..., MLA_NOPE:], pos)], axis=-1)
    k = jnp.concatenate([k[..., :MLA_NOPE], rope(k[..., MLA_NOPE:], pos)], axis=-1)
    scale = MLA_QK_DIM ** -0.5

    def block(blk, qb):
        sc = jnp.einsum('bqhd,bkhd->bhqk', qb, k).astype(jnp.float32) * scale
        p = causal_softmax(sc, blk, s)
        return jnp.einsum('bhqk,bkhd->bqhd', p.astype(v.dtype), v)

    o = query_block_map(block, q)
    return o.reshape(b, s, MLA_HEADS * MLA_V) @ w_out


def hier_moe(h, w_group, b_group, w_expert, b_expert, w_gate, w_up, w_down):
    b, s, d = h.shape
    t = h.reshape(-1, d)
    n_tok = t.shape[0]
    p_group = jax.nn.softmax((t @ w_group + b_group).astype(jnp.float32), axis=-1)
    g_w, g_idx = lax.top_k(p_group, 1)
    e_logits = (t @ w_expert + b_expert).astype(jnp.float32).reshape(n_tok, N_GROUPS, EXPERTS_PER_GROUP)
    e_logits = jnp.take_along_axis(e_logits, g_idx[:, :, None], axis=1)[:, 0]
    e_w, e_idx = lax.top_k(jax.nn.softmax(e_logits, axis=-1), TOP_K)
    weights = g_w * (e_w / jnp.sum(e_w, axis=-1, keepdims=True))
    expert_id = g_idx * EXPERTS_PER_GROUP + e_idx
    combine = jnp.einsum('tk,tke->te', weights,
                         jax.nn.one_hot(expert_id, N_EXPERTS, dtype=jnp.float32)).astype(t.dtype)

    def body(e, acc):
        hid = jax.nn.silu(t @ w_gate[e]) * (t @ w_up[e])
        return acc + combine[:, e, None] * (hid @ w_down[e])

    out = lax.fori_loop(0, N_EXPERTS, body, jnp.zeros_like(t))
    return out.reshape(b, s, d)


def setup_inputs(seed: int = 0) -> dict:
    key = jax.random.key(seed)
    ks = iter(jax.random.split(key, 40))
    nd = (DEPTH + 1) // 2
    nm = DEPTH // 2

    def nrm(shape, scale):
        return jax.random.normal(next(ks), shape, jnp.float32) * scale

    def gain(shape):
        return 1.0 + nrm(shape, 0.05)

    D = D_MODEL
    return {
        "x": nrm((BATCH, SEQ, D), 1.0),
        "attn_norm": gain((DEPTH, D)),
        "ffn_norm": gain((DEPTH, D)),
        "diff_w_in": nrm((nd, D, 2 * DIFF_QK_WIDTH + DIFF_V_WIDTH), D ** -0.5),
        "diff_q_norm": gain((nd, DIFF_HEAD_DIM)),
        "diff_k_norm": gain((nd, DIFF_HEAD_DIM)),
        "diff_lambda_q1": nrm((nd, DIFF_HEAD_DIM), 0.1),
        "diff_lambda_k1": nrm((nd, DIFF_HEAD_DIM), 0.1),
        "diff_lambda_q2": nrm((nd, DIFF_HEAD_DIM), 0.1),
        "diff_lambda_k2": nrm((nd, DIFF_HEAD_DIM), 0.1),
        "diff_subln": gain((nd, DIFF_V_DIM)),
        "diff_w_out": nrm((nd, DIFF_V_WIDTH, D), DIFF_V_WIDTH ** -0.5),
        "mla_w_a": nrm((nm, D, MLA_A_WIDTH), D ** -0.5),
        "mla_q_a_norm": gain((nm, MLA_Q_LORA)),
        "mla_kv_a_norm": gain((nm, MLA_KV_LORA)),
        "mla_w_qb": nrm((nm, MLA_Q_LORA, MLA_HEADS * MLA_QK_DIM), MLA_Q_LORA ** -0.5),
        "mla_w_kvb": nrm((nm, MLA_KV_LORA, MLA_HEADS * (MLA_NOPE + MLA_V)), MLA_KV_LORA ** -0.5),
        "mla_q_norm": gain((nm, MLA_QK_DIM)),
        "mla_k_norm": gain((nm, MLA_QK_DIM)),
        "mla_w_out": nrm((nm, MLA_HEADS * MLA_V, D), (MLA_HEADS * MLA_V) ** -0.5),
        "moe_w_group": nrm((DEPTH, D, N_GROUPS), D ** -0.5),
        "moe_b_group": nrm((DEPTH, N_GROUPS), 0.01),
        "moe_w_expert": nrm((DEPTH, D, N_EXPERTS), D ** -0.5),
        "moe_b_expert": nrm((DEPTH, N_EXPERTS), 0.01),
        "moe_w_gate": nrm((DEPTH, N_EXPERTS, D, EXPERT_FF), D ** -0.5),
        "moe_w_up": nrm((DEPTH, N_EXPERTS, D, EXPERT_FF), D ** -0.5),
        "moe_w_down": nrm((DEPTH, N_EXPERTS, EXPERT_FF, D), EXPERT_FF ** -0.5),
    }


def reference(x, attn_norm, ffn_norm, diff_w_in, diff_q_norm, diff_k_norm,
              diff_lambda_q1, diff_lambda_k1, diff_lambda_q2, diff_lambda_k2,
              diff_subln, diff_w_out, mla_w_a, mla_q_a_norm, mla_kv_a_norm,
              mla_w_qb, mla_w_kvb, mla_q_norm, mla_k_norm, mla_w_out,
              moe_w_group, moe_b_group, moe_w_expert, moe_b_expert,
              moe_w_gate, moe_w_up, moe_w_down):
    pos = jnp.arange(x.shape[1])
    for i in range(DEPTH):
        h = rms_norm(x, attn_norm[i])
        j = i // N_MIXERS
        if i % N_MIXERS == 0:
            lambda_init = 0.8 - 0.6 * math.exp(-0.3 * i)
            x = x + diff_attention(h, diff_w_in[j], diff_q_norm[j], diff_k_norm[j],
                                   diff_lambda_q1[j], diff_lambda_k1[j],
                                   diff_lambda_q2[j], diff_lambda_k2[j],
                                   diff_subln[j], diff_w_out[j], lambda_init, pos)
        else:
            x = x + mla_attention(h, mla_w_a[j], mla_q_a_norm[j], mla_kv_a_norm[j],
                                  mla_w_qb[j], mla_w_kvb[j], mla_q_norm[j], mla_k_norm[j],
                                  mla_w_out[j], pos)
        h = rms_norm(x, ffn_norm[i])
        x = x + hier_moe(h, moe_w_group[i], moe_b_group[i], moe_w_expert[i], moe_b_expert[i],
                         moe_w_gate[i], moe_w_up[i], moe_w_down[i])
    return x
```

```python
import functools
import math

import jax
import jax.numpy as jnp
from jax import lax
from jax.experimental import pallas as pl
from jax.experimental.pallas import tpu as pltpu

F32 = jnp.float32
BF16 = jnp.bfloat16

D_MODEL = 1024
ROPE_THETA = 10000.0
NORM_EPS = 1e-6
LANES = 128
MXU_DIM = 256

DIFF_HEAD_DIM = 64
DIFF_HEADS = 8
DIFF_QK_WIDTH = 1024

MLA_NOPE = 128
MLA_ROPE = 64
MLA_QK_DIM = 192
MLA_QK_PAD = 256
MLA_HEADS = 8
MLA_Q_LORA = 384
MLA_KV_LORA = 256

N_GROUPS = 8
EXPERTS_PER_GROUP = 8
N_EXPERTS = 64
TOP_K = 2
EXPERT_FF = 256

ROW_TILE = 512
ATTN_TILE = 512
EXPERT_TILE = 256
VMEM_LIMIT = 56 * 1024 * 1024


def _cparams(sem):
    return pltpu.CompilerParams(dimension_semantics=sem, vmem_limit_bytes=VMEM_LIMIT)


def _rms(x, gain):
    return x * lax.rsqrt(jnp.mean(x * x, axis=-1, keepdims=True) + NORM_EPS) * gain


def _rope128(u, cos, sin):
    return u * cos + pltpu.roll(u, 64, 1) * sin


def _diff_qkv_kernel(x_ref, g_ref, w_ref, qg_ref, kg_ref, cos_ref, sin_ref, seg_ref,
                     q_ref, k_ref, v_ref):
    h = _rms(x_ref[...], g_ref[...])
    qkv = jnp.dot(h.astype(BF16), w_ref[...], preferred_element_type=F32)
    cos = cos_ref[...]
    sin = sin_ref[...]
    seg = seg_ref[...]
    for off, gain_ref, out_ref in ((0, qg_ref, q_ref), (DIFF_QK_WIDTH, kg_ref, k_ref)):
        for c in range(DIFF_QK_WIDTH // MXU_DIM):
            t = qkv[:, off + c * MXU_DIM: off + (c + 1) * MXU_DIM]
            ss = jnp.dot((t * t).astype(BF16), seg, preferred_element_type=F32)
            tn = t * lax.rsqrt(ss * (1.0 / DIFF_HEAD_DIM) + NORM_EPS) * gain_ref[...]
            for half in range(MXU_DIM // LANES):
                u = tn[:, half * LANES:(half + 1) * LANES]
                lo = c * MXU_DIM + half * LANES
                out_ref[:, lo:lo + LANES] = _rope128(u, cos, sin).astype(BF16)
    v_ref[...] = qkv[:, 2 * DIFF_QK_WIDTH:].astype(BF16)


def _diff_qkv(x2d, gain, w_in, q_gain, k_gain, cos, sin, seg, seq):
    t = x2d.shape[0]
    tm = ROW_TILE
    pos_tiles = seq // tm
    row = lambda i: (i, 0)
    const = lambda i: (0, 0)
    out = jax.ShapeDtypeStruct((t, DIFF_QK_WIDTH), BF16)
    return pl.pallas_call(
        _diff_qkv_kernel,
        grid=(t // tm,),
        in_specs=[
            pl.BlockSpec((tm, D_MODEL), row),
            pl.BlockSpec((1, D_MODEL), const),
            pl.BlockSpec(w_in.shape, const),
            pl.BlockSpec((1, MXU_DIM), const),
            pl.BlockSpec((1, MXU_DIM), const),
            pl.BlockSpec((tm, LANES), lambda i: (i % pos_tiles, 0)),
            pl.BlockSpec((tm, LANES), lambda i: (i % pos_tiles, 0)),
            pl.BlockSpec((MXU_DIM, MXU_DIM), const),
        ],
        out_specs=[pl.BlockSpec((tm, DIFF_QK_WIDTH), row)] * 3,
        out_shape=[out, out, out],
        compiler_params=_cparams(("parallel",)),
        name="diff_qkv",
    )(x2d, gain, w_in, q_gain, k_gain, cos, sin, seg)


def _mla_proj_kernel(x_ref, g_ref, wa_ref, qag_ref, kvag_ref, wqb_ref, wkvb_ref,
                     qg_ref, kg_ref, cos_ref, sin_ref, q_ref, k_ref, v_ref):
    h = _rms(x_ref[...], g_ref[...])
    a = jnp.dot(h.astype(BF16), wa_ref[...], preferred_element_type=F32)
    cq = _rms(a[:, :MLA_Q_LORA], qag_ref[...])
    ckv = _rms(a[:, MLA_Q_LORA:MLA_Q_LORA + MLA_KV_LORA], kvag_ref[...])
    kpe = a[:, MLA_Q_LORA + MLA_KV_LORA:]
    q = jnp.dot(cq.astype(BF16), wqb_ref[...], preferred_element_type=F32)
    kv = jnp.dot(ckv.astype(BF16), wkvb_ref[...], preferred_element_type=F32)
    cos = cos_ref[...]
    sin = sin_ref[...]
    qg = qg_ref[...]
    kg = kg_ref[...]
    kpe_ss = jnp.sum(kpe * kpe, axis=-1, keepdims=True)
    kpe_rot = _rope128(kpe * kg[:, MLA_NOPE:], cos, sin)
    inv_d = 1.0 / MLA_QK_DIM
    for hd in range(MLA_HEADS):
        lo = hd * MLA_QK_PAD
        qh = q[:, lo:lo + MLA_QK_PAD]
        rq = lax.rsqrt(jnp.sum(qh * qh, axis=-1, keepdims=True) * inv_d + NORM_EPS)
        q_ref[:, lo:lo + MLA_NOPE] = (qh[:, :MLA_NOPE] * rq * qg[:, :MLA_NOPE]).astype(BF16)
        q_ref[:, lo + MLA_NOPE:lo + MLA_QK_PAD] = _rope128(
            qh[:, MLA_NOPE:] * rq * qg[:, MLA_NOPE:], cos, sin).astype(BF16)
        kn = kv[:, hd * MLA_NOPE:(hd + 1) * MLA_NOPE]
        rk = lax.rsqrt((jnp.sum(kn * kn, axis=-1, keepdims=True) + kpe_ss) * inv_d + NORM_EPS)
        k_ref[:, lo:lo + MLA_NOPE] = (kn * rk * kg[:, :MLA_NOPE]).astype(BF16)
        k_ref[:, lo + MLA_NOPE:lo + MLA_QK_PAD] = (kpe_rot * rk).astype(BF16)
    v_ref[...] = kv[:, MLA_HEADS * MLA_NOPE:].astype(BF16)


def _mla_proj(x2d, gain, w_a, qa_gain, kva_gain, w_qb, w_kvb, q_gain, k_gain, cos, sin, seq):
    t = x2d.shape[0]
    tm = ROW_TILE
    pos_tiles = seq // tm
    row = lambda i: (i, 0)
    const = lambda i: (0, 0)
    qk_out = jax.ShapeDtypeStruct((t, MLA_HEADS * MLA_QK_PAD), BF16)
    v_out = jax.ShapeDtypeStruct((t, D_MODEL), BF16)
    return pl.pallas_call(
        _mla_proj_kernel,
        grid=(t // tm,),
        in_specs=[
            pl.BlockSpec((tm, D_MODEL), row),
            pl.BlockSpec((1, D_MODEL), const),
            pl.BlockSpec(w_a.shape, const),
            pl.BlockSpec((1, MLA_Q_LORA), const),
            pl.BlockSpec((1, MLA_KV_LORA), const),
            pl.BlockSpec(w_qb.shape, const),
            pl.BlockSpec(w_kvb.shape, const),
            pl.BlockSpec((1, MLA_QK_PAD), const),
            pl.BlockSpec((1, MLA_QK_PAD), const),
            pl.BlockSpec((tm, LANES), lambda i: (i % pos_tiles, 0)),
            pl.BlockSpec((tm, LANES), lambda i: (i % pos_tiles, 0)),
        ],
        out_specs=[pl.BlockSpec((tm, MLA_HEADS * MLA_QK_PAD), row),
                   pl.BlockSpec((tm, MLA_HEADS * MLA_QK_PAD), row),
                   pl.BlockSpec((tm, D_MODEL), row)],
        out_shape=[qk_out, qk_out, v_out],
        compiler_params=_cparams(("parallel",)),
        name="mla_proj",
    )(x2d, gain, w_a, qa_gain, kva_gain, w_qb, w_kvb, q_gain, k_gain, cos, sin)


def _flash_kernel(*refs, diff, lambda_init, tile):
    if diff:
        (lq1_ref, lk1_ref, lq2_ref, lk2_ref, subln_ref, q_ref, k_ref, v_ref, o_ref,
         qs_ref, m_ref, l_ref, acc_ref) = refs
    else:
        q_ref, k_ref, v_ref, o_ref, m_ref, l_ref, acc_ref = refs
    qi = pl.program_id(2)
    rows = m_ref.shape[0]

    if diff:
        q = q_ref[0]
        lane = lax.broadcasted_iota(jnp.int32, q.shape, 1)
        first = ((lane // 32) % 2) == 0
        zero = jnp.zeros_like(q)
        qs_ref[:tile] = jnp.where(first, q, zero)
        qs_ref[tile:] = jnp.where(first, zero, q)
        load_q = lambda: qs_ref[...]
    else:
        load_q = lambda: q_ref[0]

    m_ref[...] = jnp.full(m_ref.shape, -jnp.inf, F32)
    l_ref[...] = jnp.zeros(l_ref.shape, F32)
    acc_ref[...] = jnp.zeros(acc_ref.shape, F32)

    def step(j, masked):
        off = pl.multiple_of(j * tile, tile)
        k = k_ref[0, pl.ds(off, tile), :]
        v = v_ref[0, pl.ds(off, tile), :]
        s = lax.dot_general(load_q(), k, (((1,), (1,)), ((), ())),
                            preferred_element_type=F32)
        if masked:
            r = lax.broadcasted_iota(jnp.int32, s.shape, 0)
            c = lax.broadcasted_iota(jnp.int32, s.shape, 1)
            if diff:
                r = jnp.where(r >= tile, r - tile, r)
            s = jnp.where(c <= r, s, -jnp.inf)
        m_prev = m_ref[...]
        m_new = jnp.maximum(m_prev, jnp.max(s, axis=-1, keepdims=True))
        alpha = jnp.exp(m_prev - m_new)
        p = jnp.exp(s - m_new)
        l_ref[...] = alpha * l_ref[...] + jnp.sum(p, axis=-1, keepdims=True)
        acc_ref[...] = alpha * acc_ref[...] + jnp.dot(p.astype(BF16), v,
                                                      preferred_element_type=F32)
        m_ref[...] = m_new

    def body(j, carry):
        step(j, False)
        return carry

    lax.fori_loop(0, qi, body, 0)
    step(qi, True)

    o = acc_ref[...] / l_ref[...]
    if diff:
        lam = (jnp.exp(jnp.sum(lq1_ref[...] * lk1_ref[...], axis=-1, keepdims=True))
               - jnp.exp(jnp.sum(lq2_ref[...] * lk2_ref[...], axis=-1, keepdims=True))
               + lambda_init)
        o = o[:tile] - lam * o[tile:]
        o = _rms(o, subln_ref[...]) * (1.0 - lambda_init)
    o_ref[0] = o.astype(o_ref.dtype)
    del rows


def _flash(q, k, v, heads, dk, diff_params=None, lambda_init=0.0):
    b, s, _ = q.shape
    tile = ATTN_TILE
    diff = diff_params is not None
    rows = 2 * tile if diff else tile
    q_spec = pl.BlockSpec((1, tile, dk), lambda bi, hi, qi: (bi, qi, hi))
    k_spec = pl.BlockSpec((1, s, dk), lambda bi, hi, qi: (bi, 0, hi))
    v_spec = pl.BlockSpec((1, s, LANES), lambda bi, hi, qi: (bi, 0, hi))
    o_spec = pl.BlockSpec((1, tile, LANES), lambda bi, hi, qi: (bi, qi, hi))
    small = lambda shape: pl.BlockSpec(shape, lambda bi, hi, qi: (0, 0))
    in_specs = [q_spec, k_spec, v_spec]
    args = [q, k, v]
    scratch = [pltpu.VMEM((rows, 1), F32), pltpu.VMEM((rows, 1), F32),
               pltpu.VMEM((rows, LANES), F32)]
    if diff:
        in_specs = [small((1, DIFF_HEAD_DIM))] * 4 + [small((1, LANES))] + in_specs
        args = list(diff_params) + args
        scratch = [pltpu.VMEM((rows, dk), BF16)] + scratch
    return pl.pallas_call(
        functools.partial(_flash_kernel, diff=diff, lambda_init=lambda_init, tile=tile),
        grid=(b, heads, s // tile),
        in_specs=in_specs,
        out_specs=o_spec,
        out_shape=jax.ShapeDtypeStruct((b, s, heads * LANES), BF16),
        scratch_shapes=scratch,
        compiler_params=_cparams(("parallel", "parallel", "arbitrary")),
        name="diff_flash" if diff else "mla_flash",
    )(*args)


def _post_attn_kernel(o_ref, x_ref, wo_ref, fg_ref, wrh_ref, wrl_ref, br_ref,
                      x1_ref, hn_ref, meta_ref, cnt_ref, carry_ref):
    i = pl.program_id(0)

    @pl.when(i == 0)
    def _():
        carry_ref[...] = jnp.zeros(carry_ref.shape, F32)

    x1 = x_ref[...] + jnp.dot(o_ref[...], wo_ref[...], preferred_element_type=F32)
    x1_ref[...] = x1
    hn = _rms(x1, fg_ref[...])
    hn_ref[...] = hn

    hi = hn.astype(BF16)
    lo = (hn - hi.astype(F32)).astype(BF16)
    wrh = wrh_ref[...]
    logits = (jnp.dot(hi, wrh, preferred_element_type=F32)
              + jnp.dot(lo, wrh, preferred_element_type=F32)
              + jnp.dot(hi, wrl_ref[...], preferred_element_type=F32)
              + br_ref[...])
    tm = logits.shape[0]
    lane = lax.broadcasted_iota(jnp.int32, logits.shape, 1)
    lane_f = lane.astype(F32)
    neg = jnp.float32(-jnp.inf)
    big = jnp.float32(1e9)

    is_group = (lane >= N_EXPERTS) & (lane < N_EXPERTS + N_GROUPS)
    gl = jnp.where(is_group, logits, neg)
    gmax = jnp.max(gl, axis=-1, keepdims=True)
    gidx = jnp.min(jnp.where(gl == gmax, lane_f, big), axis=-1, keepdims=True) - N_EXPERTS
    g_w = 1.0 / jnp.sum(jnp.exp(gl - gmax), axis=-1, keepdims=True)

    in_group = (lane < N_EXPERTS) & ((lane // EXPERTS_PER_GROUP).astype(F32) == gidx)
    el = jnp.where(in_group, logits, neg)
    m1 = jnp.max(el, axis=-1, keepdims=True)
    i1 = jnp.min(jnp.where(el == m1, lane_f, big), axis=-1, keepdims=True)
    el2 = jnp.where(lane_f == i1, neg, el)
    m2 = jnp.max(el2, axis=-1, keepdims=True)
    i2 = jnp.min(jnp.where(el2 == m2, lane_f, big), axis=-1, keepdims=True)
    d = jnp.exp(m2 - m1)
    w1 = g_w / (1.0 + d)
    w2 = g_w * d / (1.0 + d)

    oh1 = lane_f == i1
    oh2 = lane_f == i2
    oh = jnp.where(oh1 | oh2, 1.0, 0.0)
    r = lax.broadcasted_iota(jnp.int32, (tm, tm), 0)
    c = lax.broadcasted_iota(jnp.int32, (tm, tm), 1)
    lower = jnp.where(c < r, 1.0, 0.0).astype(BF16)
    prefix = jnp.dot(lower, oh.astype(BF16), preferred_element_type=F32) + carry_ref[...]
    rank1 = jnp.sum(jnp.where(oh1, prefix, 0.0), axis=-1, keepdims=True)
    rank2 = jnp.sum(jnp.where(oh2, prefix, 0.0), axis=-1, keepdims=True)
    carry = carry_ref[...] + jnp.sum(oh, axis=0, keepdims=True)
    carry_ref[...] = carry
    cnt_ref[...] = jnp.broadcast_to(carry, cnt_ref.shape)

    meta = jnp.zeros(logits.shape, F32)
    for ln, val in enumerate((i1, i2, w1, w2, rank1, rank2)):
        meta = jnp.where(lane == ln, val, meta)
    meta_ref[...] = meta


def _post_attn(o2d, x2d, w_out, ffn_gain, wr_hi, wr_lo, b_r):
    t = x2d.shape[0]
    tm = ROW_TILE
    row = lambda i: (i, 0)
    const = lambda i: (0, 0)
    return pl.pallas_call(
        _post_attn_kernel,
        grid=(t // tm,),
        in_specs=[
            pl.BlockSpec((tm, D_MODEL), row),
            pl.BlockSpec((tm, D_MODEL), row),
            pl.BlockSpec((D_MODEL, D_MODEL), const),
            pl.BlockSpec((1, D_MODEL), const),
            pl.BlockSpec((D_MODEL, LANES), const),
            pl.BlockSpec((D_MODEL, LANES), const),
            pl.BlockSpec((1, LANES), const),
        ],
        out_specs=[pl.BlockSpec((tm, D_MODEL), row),
                   pl.BlockSpec((tm, D_MODEL), row),
                   pl.BlockSpec((tm, LANES), row),
                   pl.BlockSpec((8, LANES), const)],
        out_shape=[jax.ShapeDtypeStruct((t, D_MODEL), F32),
                   jax.ShapeDtypeStruct((t, D_MODEL), F32),
                   jax.ShapeDtypeStruct((t, LANES), F32),
                   jax.ShapeDtypeStruct((8, LANES), F32)],
        scratch_shapes=[pltpu.VMEM((1, LANES), F32)],
        compiler_params=_cparams(("arbitrary",)),
        name="post_attn_router",
    )(o2d, x2d, w_out, ffn_gain, wr_hi, wr_lo, b_r)


def _row_copy(src, dst, sem):
    return pltpu.make_async_copy(src, dst, sem)


def _dispatch_kernel(dest_ref, hn_ref, xs_in_ref, xs_ref, sem):
    del xs_in_ref
    tm = hn_ref.shape[0]
    base = pl.program_id(0) * (TOP_K * tm)

    def issue(t, carry):
        for kk in range(TOP_K):
            d = dest_ref[base + TOP_K * t + kk]
            _row_copy(hn_ref.at[pl.ds(t, 1)], xs_ref.at[pl.ds(d, 1)], sem).start()
        return carry

    lax.fori_loop(0, tm, issue, 0)

    def drain(t, carry):
        for kk in range(TOP_K):
            _row_copy(hn_ref.at[pl.ds(0, 1)], xs_ref.at[pl.ds(0, 1)], sem).wait()
        return carry

    lax.fori_loop(0, tm, drain, 0)


def _dispatch(dest, hn, xs_zero):
    t = hn.shape[0]
    tm = ROW_TILE
    grid_spec = pltpu.PrefetchScalarGridSpec(
        num_scalar_prefetch=1,
        grid=(t // tm,),
        in_specs=[pl.BlockSpec((tm, D_MODEL), lambda i, dest: (i, 0)),
                  pl.BlockSpec(memory_space=pl.ANY)],
        out_specs=pl.BlockSpec(memory_space=pl.ANY),
        scratch_shapes=[pltpu.SemaphoreType.DMA(())],
    )
    return pl.pallas_call(
        _dispatch_kernel,
        grid_spec=grid_spec,
        out_shape=jax.ShapeDtypeStruct(xs_zero.shape, xs_zero.dtype),
        input_output_aliases={2: 0},
        compiler_params=_cparams(("arbitrary",)),
        name="moe_dispatch",
    )(dest, hn, xs_zero)


def _expert_kernel(te_ref, tv_ref, xs_ref, wg_ref, wu_ref, wd_ref, ys_ref):
    i = pl.program_id(0)

    @pl.when(tv_ref[i] == 1)
    def _():
        x = xs_ref[...].astype(BF16)
        g = jnp.dot(x, wg_ref[0].astype(BF16), preferred_element_type=F32)
        u = jnp.dot(x, wu_ref[0].astype(BF16), preferred_element_type=F32)
        hid = g * jax.nn.sigmoid(g) * u
        ys_ref[...] = jnp.dot(hid.astype(BF16), wd_ref[0].astype(BF16),
                              preferred_element_type=F32)

    @pl.when(tv_ref[i] == 0)
    def _():
        ys_ref[...] = jnp.zeros(ys_ref.shape, ys_ref.dtype)


def _experts(tile_expert, tile_valid, xs, w_gate, w_up, w_down):
    p = xs.shape[0]
    tile = EXPERT_TILE
    grid_spec = pltpu.PrefetchScalarGridSpec(
        num_scalar_prefetch=2,
        grid=(p // tile,),
        in_specs=[
            pl.BlockSpec((tile, D_MODEL), lambda i, te, tv: (i, 0)),
            pl.BlockSpec((1, D_MODEL, EXPERT_FF), lambda i, te, tv: (te[i], 0, 0)),
            pl.BlockSpec((1, D_MODEL, EXPERT_FF), lambda i, te, tv: (te[i], 0, 0)),
            pl.BlockSpec((1, EXPERT_FF, D_MODEL), lambda i, te, tv: (te[i], 0, 0)),
        ],
        out_specs=pl.BlockSpec((tile, D_MODEL), lambda i, te, tv: (i, 0)),
    )
    return pl.pallas_call(
        _expert_kernel,
        grid_spec=grid_spec,
        out_shape=jax.ShapeDtypeStruct((p, D_MODEL), F32),
        compiler_params=_cparams(("arbitrary",)),
        name="moe_experts",
    )(tile_expert, tile_valid, xs, w_gate, w_up, w_down)


def _combine_kernel(dest_ref, x1_ref, meta_ref, ys_ref, out_ref, buf_ref, sem):
    tm = x1_ref.shape[0]
    base = pl.program_id(0) * (TOP_K * tm)

    def issue(t, carry):
        for kk in range(TOP_K):
            d = dest_ref[base + TOP_K * t + kk]
            _row_copy(ys_ref.at[pl.ds(d, 1)], buf_ref.at[kk, pl.ds(t, 1)], sem).start()
        return carry

    lax.fori_loop(0, tm, issue, 0)

    def drain(t, carry):
        for kk in range(TOP_K):
            _row_copy(ys_ref.at[pl.ds(0, 1)], buf_ref.at[kk, pl.ds(0, 1)], sem).wait()
        return carry

    lax.fori_loop(0, tm, drain, 0)

    meta = meta_ref[...]
    out_ref[...] = (x1_ref[...] + meta[:, 2:3] * buf_ref[0] + meta[:, 3:4] * buf_ref[1])


def _combine(dest, x1, meta, ys):
    t = x1.shape[0]
    tm = ROW_TILE
    grid_spec = pltpu.PrefetchScalarGridSpec(
        num_scalar_prefetch=1,
        grid=(t // tm,),
        in_specs=[pl.BlockSpec((tm, D_MODEL), lambda i, dest: (i, 0)),
                  pl.BlockSpec((tm, LANES), lambda i, dest: (i, 0)),
                  pl.BlockSpec(memory_space=pl.ANY)],
        out_specs=pl.BlockSpec((tm, D_MODEL), lambda i, dest: (i, 0)),
        scratch_shapes=[pltpu.VMEM((TOP_K, tm, D_MODEL), F32),
                        pltpu.SemaphoreType.DMA(())],
    )
    return pl.pallas_call(
        _combine_kernel,
        grid_spec=grid_spec,
        out_shape=jax.ShapeDtypeStruct((t, D_MODEL), F32),
        compiler_params=_cparams(("arbitrary",)),
        name="moe_combine",
    )(dest, x1, meta, ys)


def _rope_tables(seq):
    half = DIFF_HEAD_DIM // 2
    inv_freq = 1.0 / (ROPE_THETA ** (jnp.arange(0, DIFF_HEAD_DIM, 2, dtype=F32) / DIFF_HEAD_DIM))
    ang = jnp.arange(seq, dtype=F32)[:, None] * inv_freq[None, :]
    cos = jnp.cos(ang)
    sin = jnp.sin(ang)
    zero = jnp.zeros((seq, half), F32)
    diff_cos = jnp.concatenate([cos, cos, cos, cos], axis=-1)
    diff_sin = jnp.concatenate([-sin, -sin, sin, sin], axis=-1)
    mla_cos = jnp.concatenate([cos, zero, cos, zero], axis=-1)
    mla_sin = jnp.concatenate([-sin, zero, sin, zero], axis=-1)
    return diff_cos, diff_sin, mla_cos, mla_sin


def _diff_head_layout(w):
    lead = w.shape[:-1]
    w = w.reshape(*lead, DIFF_HEADS, 2, 2, DIFF_HEAD_DIM // 2)
    w = jnp.swapaxes(w, -3, -2)
    return w.reshape(*lead, DIFF_QK_WIDTH)


def _diff_gain_layout(g, scale):
    g = (g.astype(F32) * scale).reshape(2, DIFF_HEAD_DIM // 2)
    g = jnp.broadcast_to(g[:, None, :], (2, 2, DIFF_HEAD_DIM // 2))
    return jnp.tile(g.reshape(1, LANES), (1, MXU_DIM // LANES))


def _pad_rope(w):
    half = MLA_ROPE // 2
    z = jnp.zeros(w.shape[:-1] + (half,), w.dtype)
    return jnp.concatenate([w[..., :half], z, w[..., half:], z], axis=-1)


def _mla_qk_layout(w):
    return jnp.concatenate([w[..., :MLA_NOPE], _pad_rope(w[..., MLA_NOPE:])], axis=-1)


def _segment_matrix():
    lane = jnp.arange(MXU_DIM)
    key = (lane // LANES) * 2 + (lane // 32) % 2
    return (key[:, None] == key[None, :]).astype(BF16)


def _moe(x1, hn, meta, cnt, w_gate, w_up, w_down):
    t = x1.shape[0]
    tile = EXPERT_TILE
    n_slots = t * TOP_K + N_EXPERTS * tile
    n_tiles = n_slots // tile
    counts = cnt[0, :N_EXPERTS].astype(jnp.int32)
    padded = ((counts + tile - 1) // tile) * tile
    ends = jnp.cumsum(padded)
    offsets = ends - padded
    tile_start = jnp.arange(n_tiles, dtype=jnp.int32) * tile
    tile_expert = jnp.sum(tile_start[:, None] >= ends[None, :], axis=1).astype(jnp.int32)
    tile_valid = (tile_start < ends[-1]).astype(jnp.int32)
    tile_expert = jnp.minimum(tile_expert, N_EXPERTS - 1)
    ids = meta[:, 0:TOP_K].astype(jnp.int32)
    rank = meta[:, 4:4 + TOP_K].astype(jnp.int32)
    onehot = ids[..., None] == jnp.arange(N_EXPERTS, dtype=jnp.int32)
    dest = (rank + jnp.sum(jnp.where(onehot, offsets, 0), axis=-1)).reshape(-1)
    xs = _dispatch(dest, hn, jnp.zeros((n_slots, D_MODEL), F32))
    ys = _experts(tile_expert, tile_valid, xs, w_gate, w_up, w_down)
    return _combine(dest, x1, meta, ys)


def _router_weights(w_group, b_group, w_expert, b_expert):
    pad = LANES - N_EXPERTS - N_GROUPS
    w = jnp.concatenate([w_expert, w_group, jnp.zeros((D_MODEL, pad), F32)], axis=-1)
    b = jnp.concatenate([b_expert, b_group, jnp.zeros((pad,), F32)]).reshape(1, LANES)
    w_hi = w.astype(BF16)
    w_lo = (w - w_hi.astype(F32)).astype(BF16)
    return w_hi, w_lo, b


def kernel(x, attn_norm, ffn_norm, diff_w_in, diff_q_norm, diff_k_norm, diff_lambda_q1, diff_lambda_k1, diff_lambda_q2, diff_lambda_k2, diff_subln, diff_w_out, mla_w_a, mla_q_a_norm, mla_kv_a_norm, mla_w_qb, mla_w_kvb, mla_q_norm, mla_k_norm, mla_w_out, moe_w_group, moe_b_group, moe_w_expert, moe_b_expert, moe_w_gate, moe_w_up, moe_w_down):
    b, s, d = x.shape
    assert d == D_MODEL and s % ATTN_TILE == 0 and s % ROW_TILE == 0
    t = b * s
    diff_cos, diff_sin, mla_cos, mla_sin = _rope_tables(s)
    row = lambda v: v.astype(F32).reshape(1, -1)
    x2d = x.reshape(t, d)

    lambda_init = 0.8 - 0.6 * math.exp(-0.3 * 0)
    w_in = diff_w_in[0]
    w_in = jnp.concatenate([_diff_head_layout(w_in[:, :DIFF_QK_WIDTH]),
                            _diff_head_layout(w_in[:, DIFF_QK_WIDTH:2 * DIFF_QK_WIDTH]),
                            w_in[:, 2 * DIFF_QK_WIDTH:]], axis=-1).astype(BF16)
    q, k, v = _diff_qkv(x2d, row(attn_norm[0]), w_in,
                        _diff_gain_layout(diff_q_norm[0], DIFF_HEAD_DIM ** -0.5),
                        _diff_gain_layout(diff_k_norm[0], 1.0),
                        diff_cos, diff_sin, _segment_matrix(), s)
    shp = (b, s, DIFF_QK_WIDTH)
    o = _flash(q.reshape(shp), k.reshape(shp), v.reshape(shp), DIFF_HEADS, LANES,
               diff_params=(row(diff_lambda_q1[0]), row(diff_lambda_k1[0]),
                            row(diff_lambda_q2[0]), row(diff_lambda_k2[0]),
                            row(diff_subln[0])),
               lambda_init=lambda_init)
    wr_hi, wr_lo, b_r = _router_weights(moe_w_group[0], moe_b_group[0],
                                        moe_w_expert[0], moe_b_expert[0])
    x1, hn, meta, cnt = _post_attn(o.reshape(t, d), x2d, diff_w_out[0].astype(BF16),
                                   row(ffn_norm[0]), wr_hi, wr_lo, b_r)
    x2d = _moe(x1, hn, meta, cnt, moe_w_gate[0], moe_w_up[0], moe_w_down[0])

    w_a = mla_w_a[0]
    split = MLA_Q_LORA + MLA_KV_LORA
    w_a = jnp.concatenate([w_a[:, :split], _pad_rope(w_a[:, split:])], axis=-1).astype(BF16)
    w_qb = _mla_qk_layout(mla_w_qb[0].reshape(MLA_Q_LORA, MLA_HEADS, MLA_QK_DIM))
    w_qb = w_qb.reshape(MLA_Q_LORA, MLA_HEADS * MLA_QK_PAD).astype(BF16)
    w_kvb = mla_w_kvb[0].reshape(MLA_KV_LORA, MLA_HEADS, 2 * MLA_NOPE)
    w_kvb = jnp.concatenate([w_kvb[..., :MLA_NOPE].reshape(MLA_KV_LORA, -1),
                             w_kvb[..., MLA_NOPE:].reshape(MLA_KV_LORA, -1)], axis=-1).astype(BF16)
    q_gain = _mla_qk_layout(mla_q_norm[0].astype(F32) * MLA_QK_DIM ** -0.5).reshape(1, -1)
    k_gain = _mla_qk_layout(mla_k_norm[0].astype(F32)).reshape(1, -1)
    q, k, v = _mla_proj(x2d, row(attn_norm[1]), w_a, row(mla_q_a_norm[0]), row(mla_kv_a_norm[0]),
                        w_qb, w_kvb, q_gain, k_gain, mla_cos, mla_sin, s)
    qk_shp = (b, s, MLA_HEADS * MLA_QK_PAD)
    o = _flash(q.reshape(qk_shp), k.reshape(qk_shp), v.reshape(b, s, d), MLA_HEADS, MLA_QK_PAD)
    wr_hi, wr_lo, b_r = _router_weights(moe_w_group[1], moe_b_group[1],
                                        moe_w_expert[1], moe_b_expert[1])
    x1, hn, meta, cnt = _post_attn(o.reshape(t, d), x2d, mla_w_out[0].astype(BF16),
                                   row(ffn_norm[1]), wr_hi, wr_lo, b_r)
    x2d = _moe(x1, hn, meta, cnt, moe_w_gate[1], moe_w_up[1], moe_w_down[1])
    return x2d.reshape(b, s, d)
```

```python
import functools
import math

import jax
import jax.numpy as jnp
from jax import lax
from jax.experimental import pallas as pl
from jax.experimental.pallas import tpu as pltpu

F32 = jnp.float32
BF16 = jnp.bfloat16

D_MODEL = 1024
ROPE_THETA = 10000.0
NORM_EPS = 1e-6
LANES = 128
MXU_DIM = 256

DIFF_HEAD_DIM = 64
DIFF_HEADS = 8
DIFF_QK_WIDTH = 1024

MLA_NOPE = 128
MLA_ROPE = 64
MLA_QK_DIM = 192
MLA_QK_PAD = 256
MLA_HEADS = 8
MLA_Q_LORA = 384
MLA_KV_LORA = 256

N_GROUPS = 8
EXPERTS_PER_GROUP = 8
N_EXPERTS = 64
TOP_K = 2
EXPERT_FF = 256

ROW_TILE = 512
Q_TILE = 1024
FLASH_CHUNK = 512
V_ROWS = 144
LOG2E = math.log2(math.e)
EXPERT_TILE = 256
ISSUE_UNROLL = 8
VMEM_LIMIT = 56 * 1024 * 1024


def _cparams(sem):
    return pltpu.CompilerParams(dimension_semantics=sem, vmem_limit_bytes=VMEM_LIMIT)


def _rms(x, gain):
    return x * lax.rsqrt(jnp.mean(x * x, axis=-1, keepdims=True) + NORM_EPS) * gain


def _store_v_transposed(v, vt_ref, heads):
    tm = v.shape[0]
    ones = jnp.ones((V_ROWS - LANES, tm), BF16)
    for hd in range(heads):
        lo = hd * V_ROWS
        vt_ref[0, 0, lo:lo + LANES, :] = v[:, hd * LANES:(hd + 1) * LANES].T.astype(BF16)
        vt_ref[0, 0, lo + LANES:lo + V_ROWS, :] = ones


def _rope128(u, cos, sin):
    return u * cos + pltpu.roll(u, 64, 1) * sin


def _diff_qkv_kernel(x_ref, g_ref, w_ref, qg_ref, kg_ref, cos_ref, sin_ref, seg_ref,
                     q_ref, k_ref, vt_ref):
    h = _rms(x_ref[...], g_ref[...])
    qkv = jnp.dot(h.astype(BF16), w_ref[...], preferred_element_type=F32)
    cos = cos_ref[...]
    sin = sin_ref[...]
    seg = seg_ref[...]
    for off, gain_ref, out_ref in ((0, qg_ref, q_ref), (DIFF_QK_WIDTH, kg_ref, k_ref)):
        for c in range(DIFF_QK_WIDTH // MXU_DIM):
            t = qkv[:, off + c * MXU_DIM: off + (c + 1) * MXU_DIM]
            ss = jnp.dot((t * t).astype(BF16), seg, preferred_element_type=F32)
            tn = t * lax.rsqrt(ss * (1.0 / DIFF_HEAD_DIM) + NORM_EPS) * gain_ref[...]
            for half in range(MXU_DIM // LANES):
                u = tn[:, half * LANES:(half + 1) * LANES]
                lo = c * MXU_DIM + half * LANES
                out_ref[:, lo:lo + LANES] = _rope128(u, cos, sin).astype(BF16)
    _store_v_transposed(qkv[:, 2 * DIFF_QK_WIDTH:], vt_ref, DIFF_HEADS)


def _diff_qkv(x2d, gain, w_in, q_gain, k_gain, cos, sin, seg, seq):
    t = x2d.shape[0]
    tm = ROW_TILE
    pos_tiles = seq // tm
    row = lambda i: (i, 0)
    const = lambda i: (0, 0)
    out = jax.ShapeDtypeStruct((t, DIFF_QK_WIDTH), BF16)
    vt_out = jax.ShapeDtypeStruct((t // seq, pos_tiles, DIFF_HEADS * V_ROWS, tm), BF16)
    vt_spec = pl.BlockSpec((1, 1, DIFF_HEADS * V_ROWS, tm),
                           lambda i: (i // pos_tiles, i % pos_tiles, 0, 0))
    return pl.pallas_call(
        _diff_qkv_kernel,
        grid=(t // tm,),
        in_specs=[
            pl.BlockSpec((tm, D_MODEL), row),
            pl.BlockSpec((1, D_MODEL), const),
            pl.BlockSpec(w_in.shape, const),
            pl.BlockSpec((1, MXU_DIM), const),
            pl.BlockSpec((1, MXU_DIM), const),
            pl.BlockSpec((tm, LANES), lambda i: (i % pos_tiles, 0)),
            pl.BlockSpec((tm, LANES), lambda i: (i % pos_tiles, 0)),
            pl.BlockSpec((MXU_DIM, MXU_DIM), const),
        ],
        out_specs=[pl.BlockSpec((tm, DIFF_QK_WIDTH), row)] * 2 + [vt_spec],
        out_shape=[out, out, vt_out],
        compiler_params=_cparams(("parallel",)),
        name="diff_qkv",
    )(x2d, gain, w_in, q_gain, k_gain, cos, sin, seg)


def _mla_proj_kernel(x_ref, g_ref, wa_ref, qag_ref, kvag_ref, wqb_ref, wkvb_ref,
                     qg_ref, kg_ref, cos_ref, sin_ref, q_ref, k_ref, vt_ref):
    h = _rms(x_ref[...], g_ref[...])
    a = jnp.dot(h.astype(BF16), wa_ref[...], preferred_element_type=F32)
    cq = _rms(a[:, :MLA_Q_LORA], qag_ref[...])
    ckv = _rms(a[:, MLA_Q_LORA:MLA_Q_LORA + MLA_KV_LORA], kvag_ref[...])
    kpe = a[:, MLA_Q_LORA + MLA_KV_LORA:]
    q = jnp.dot(cq.astype(BF16), wqb_ref[...], preferred_element_type=F32)
    kv = jnp.dot(ckv.astype(BF16), wkvb_ref[...], preferred_element_type=F32)
    cos = cos_ref[...]
    sin = sin_ref[...]
    qg = qg_ref[...]
    kg = kg_ref[...]
    kpe_ss = jnp.sum(kpe * kpe, axis=-1, keepdims=True)
    kpe_rot = _rope128(kpe * kg[:, MLA_NOPE:], cos, sin)
    inv_d = 1.0 / MLA_QK_DIM
    for hd in range(MLA_HEADS):
        lo = hd * MLA_QK_PAD
        qh = q[:, lo:lo + MLA_QK_PAD]
        rq = lax.rsqrt(jnp.sum(qh * qh, axis=-1, keepdims=True) * inv_d + NORM_EPS)
        q_ref[:, lo:lo + MLA_NOPE] = (qh[:, :MLA_NOPE] * rq * qg[:, :MLA_NOPE]).astype(BF16)
        q_ref[:, lo + MLA_NOPE:lo + MLA_QK_PAD] = _rope128(
            qh[:, MLA_NOPE:] * rq * qg[:, MLA_NOPE:], cos, sin).astype(BF16)
        kn = kv[:, hd * MLA_NOPE:(hd + 1) * MLA_NOPE]
        rk = lax.rsqrt((jnp.sum(kn * kn, axis=-1, keepdims=True) + kpe_ss) * inv_d + NORM_EPS)
        k_ref[:, lo:lo + MLA_NOPE] = (kn * rk * kg[:, :MLA_NOPE]).astype(BF16)
        k_ref[:, lo + MLA_NOPE:lo + MLA_QK_PAD] = (kpe_rot * rk).astype(BF16)
    _store_v_transposed(kv[:, MLA_HEADS * MLA_NOPE:], vt_ref, MLA_HEADS)


def _mla_proj(x2d, gain, w_a, qa_gain, kva_gain, w_qb, w_kvb, q_gain, k_gain, cos, sin, seq):
    t = x2d.shape[0]
    tm = ROW_TILE
    pos_tiles = seq // tm
    row = lambda i: (i, 0)
    const = lambda i: (0, 0)
    qk_out = jax.ShapeDtypeStruct((t, MLA_HEADS * MLA_QK_PAD), BF16)
    vt_out = jax.ShapeDtypeStruct((t // seq, pos_tiles, MLA_HEADS * V_ROWS, tm), BF16)
    return pl.pallas_call(
        _mla_proj_kernel,
        grid=(t // tm,),
        in_specs=[
            pl.BlockSpec((tm, D_MODEL), row),
            pl.BlockSpec((1, D_MODEL), const),
            pl.BlockSpec(w_a.shape, const),
            pl.BlockSpec((1, MLA_Q_LORA), const),
            pl.BlockSpec((1, MLA_KV_LORA), const),
            pl.BlockSpec(w_qb.shape, const),
            pl.BlockSpec(w_kvb.shape, const),
            pl.BlockSpec((1, MLA_QK_PAD), const),
            pl.BlockSpec((1, MLA_QK_PAD), const),
            pl.BlockSpec((tm, LANES), lambda i: (i % pos_tiles, 0)),
            pl.BlockSpec((tm, LANES), lambda i: (i % pos_tiles, 0)),
        ],
        out_specs=[pl.BlockSpec((tm, MLA_HEADS * MLA_QK_PAD), row),
                   pl.BlockSpec((tm, MLA_HEADS * MLA_QK_PAD), row),
                   pl.BlockSpec((1, 1, MLA_HEADS * V_ROWS, tm),
                                lambda i: (i // pos_tiles, i % pos_tiles, 0, 0))],
        out_shape=[qk_out, qk_out, vt_out],
        compiler_params=_cparams(("parallel",)),
        name="mla_proj",
    )(x2d, gain, w_a, qa_gain, kva_gain, w_qb, w_kvb, q_gain, k_gain, cos, sin)


def _flash_kernel(*refs, diff, lambda_init, tq, tk, n_chunks):
    if diff:
        (lq1_ref, lk1_ref, lq2_ref, lk2_ref, subln_ref, q_ref, k_ref, vt_ref, o_ref,
         qt_ref, m_ref, acc_ref, sa_ref, sb_ref) = refs
    else:
        q_ref, k_ref, vt_ref, o_ref, qt_ref, m_ref, acc_ref, sa_ref, sb_ref = refs
    qi = pl.program_id(2)
    ch = FLASH_CHUNK
    copies = 2 if diff else 1
    chunks_per_copy = n_chunks // copies

    q_t = q_ref[0].astype(F32).T
    if diff:
        feat = lax.broadcasted_iota(jnp.int32, q_t.shape, 0)
        first = ((feat // 32) % 2) == 0
        qt_ref[:, :tq] = jnp.where(first, q_t, 0.0).astype(BF16)
        qt_ref[:, tq:] = jnp.where(first, 0.0, q_t).astype(BF16)
    else:
        qt_ref[...] = q_t.astype(BF16)
    m_ref[...] = jnp.full(m_ref.shape, -jnp.inf, F32)
    acc_ref[...] = jnp.zeros(acc_ref.shape, F32)

    def key_tile(j):
        return k_ref[0, pl.ds(pl.multiple_of(j * tk, tk), tk), :]

    def scores(j, s_ref, chunks):
        k = key_tile(j)
        for c in chunks:
            s_ref[c] = jnp.dot(k, qt_ref[:, c * ch:(c + 1) * ch], preferred_element_type=F32)

    def accumulate(j, s_ref, chunks, masked=()):
        vt = vt_ref[0, j]
        for c in chunks:
            s = s_ref[c]
            if c in masked:
                key = lax.broadcasted_iota(jnp.int32, (tk, ch), 0)
                qry = lax.broadcasted_iota(jnp.int32, (tk, ch), 1)
                s = jnp.where(key <= qry, s, -jnp.inf)
            m_prev = m_ref[c]
            m_new = jnp.maximum(m_prev, jnp.max(s, axis=0, keepdims=True))
            alpha = jnp.exp2(m_prev - m_new)
            p = jnp.exp2(s - m_new).astype(BF16)
            acc_ref[c] = alpha * acc_ref[c] + jnp.dot(vt, p, preferred_element_type=F32)
            m_ref[c] = m_new

    every = tuple(range(n_chunks))
    early = tuple(c for c in every if c % chunks_per_copy == 0)
    late = tuple(c for c in every if c % chunks_per_copy == 1)
    scores(0, sa_ref, every)

    def pair_body(i, carry):
        j = 2 * i
        scores(j + 1, sb_ref, every)
        accumulate(j, sa_ref, every)
        scores(j + 2, sa_ref, every)
        accumulate(j + 1, sb_ref, every)
        return carry

    lax.fori_loop(0, qi, pair_body, 0)
    j = 2 * qi
    scores(j + 1, sb_ref, late)
    accumulate(j, sa_ref, every, masked=early)
    accumulate(j + 1, sb_ref, late, masked=late)

    def normalized(c):
        a = acc_ref[c]
        return a[:LANES] * (1.0 / a[LANES:LANES + 1])

    if diff:
        lam = (jnp.exp(jnp.sum(lq1_ref[...] * lk1_ref[...], axis=-1, keepdims=True))
               - jnp.exp(jnp.sum(lq2_ref[...] * lk2_ref[...], axis=-1, keepdims=True))
               + lambda_init)
        for c in range(chunks_per_copy):
            o_t = normalized(c) - lam * normalized(c + chunks_per_copy)
            o_t = o_t * lax.rsqrt(jnp.mean(o_t * o_t, axis=0, keepdims=True) + NORM_EPS)
            o = o_t.T * (subln_ref[...] * (1.0 - lambda_init))
            o_ref[0, c * ch:(c + 1) * ch, :] = o.astype(o_ref.dtype)
    else:
        for c in range(n_chunks):
            o_ref[0, c * ch:(c + 1) * ch, :] = normalized(c).T.astype(o_ref.dtype)


def _flash(q, k, vt, heads, dk, tq, diff_params=None, lambda_init=0.0):
    b, s, _ = q.shape
    tk = ROW_TILE
    diff = diff_params is not None
    n_chunks = (2 if diff else 1) * tq // FLASH_CHUNK
    q_spec = pl.BlockSpec((1, tq, dk), lambda bi, hi, qi: (bi, qi, hi))
    k_spec = pl.BlockSpec((1, s, dk), lambda bi, hi, qi: (bi, 0, hi))
    v_spec = pl.BlockSpec((1, s // tk, V_ROWS, tk), lambda bi, hi, qi: (bi, 0, hi, 0))
    o_spec = pl.BlockSpec((1, tq, LANES), lambda bi, hi, qi: (bi, qi, hi))
    small = lambda shape: pl.BlockSpec(shape, lambda bi, hi, qi: (0, 0))
    in_specs = [q_spec, k_spec, v_spec]
    args = [q, k, vt]
    if diff:
        in_specs = [small((1, DIFF_HEAD_DIM))] * 4 + [small((1, LANES))] + in_specs
        args = list(diff_params) + args
    assert tk == FLASH_CHUNK and tq == 2 * tk
    scratch = [pltpu.VMEM((dk, n_chunks * FLASH_CHUNK), BF16),
               pltpu.VMEM((n_chunks, 1, FLASH_CHUNK), F32),
               pltpu.VMEM((n_chunks, V_ROWS, FLASH_CHUNK), F32),
               pltpu.VMEM((n_chunks, tk, FLASH_CHUNK), F32),
               pltpu.VMEM((n_chunks, tk, FLASH_CHUNK), F32)]
    return pl.pallas_call(
        functools.partial(_flash_kernel, diff=diff, lambda_init=lambda_init, tq=tq, tk=tk,
                          n_chunks=n_chunks),
        grid=(b, heads, s // tq),
        in_specs=in_specs,
        out_specs=o_spec,
        out_shape=jax.ShapeDtypeStruct((b, s, heads * LANES), BF16),
        scratch_shapes=scratch,
        compiler_params=_cparams(("parallel", "parallel", "arbitrary")),
        name="diff_flash" if diff else "mla_flash",
    )(*args)


def _post_attn_kernel(o_ref, x_ref, wo_ref, fg_ref, wrh_ref, wrl_ref, br_ref,
                      x1_ref, hn_ref, meta_ref, cnt_ref, carry_ref):
    i = pl.program_id(0)

    @pl.when(i == 0)
    def _():
        carry_ref[...] = jnp.zeros(carry_ref.shape, F32)

    x1 = x_ref[...] + jnp.dot(o_ref[...], wo_ref[...], preferred_element_type=F32)
    x1_ref[...] = x1
    hn = _rms(x1, fg_ref[...])
    hn_ref[...] = hn

    hi = hn.astype(BF16)
    lo = (hn - hi.astype(F32)).astype(BF16)
    wrh = wrh_ref[...]
    logits = (jnp.dot(hi, wrh, preferred_element_type=F32)
              + jnp.dot(lo, wrh, preferred_element_type=F32)
              + jnp.dot(hi, wrl_ref[...], preferred_element_type=F32)
              + br_ref[...])
    tm = logits.shape[0]
    lane = lax.broadcasted_iota(jnp.int32, logits.shape, 1)
    lane_f = lane.astype(F32)
    neg = jnp.float32(-jnp.inf)
    big = jnp.float32(1e9)

    is_group = (lane >= N_EXPERTS) & (lane < N_EXPERTS + N_GROUPS)
    gl = jnp.where(is_group, logits, neg)
    gmax = jnp.max(gl, axis=-1, keepdims=True)
    gidx = jnp.min(jnp.where(gl == gmax, lane_f, big), axis=-1, keepdims=True) - N_EXPERTS
    g_w = 1.0 / jnp.sum(jnp.exp(gl - gmax), axis=-1, keepdims=True)

    in_group = (lane < N_EXPERTS) & ((lane // EXPERTS_PER_GROUP).astype(F32) == gidx)
    el = jnp.where(in_group, logits, neg)
    m1 = jnp.max(el, axis=-1, keepdims=True)
    i1 = jnp.min(jnp.where(el == m1, lane_f, big), axis=-1, keepdims=True)
    el2 = jnp.where(lane_f == i1, neg, el)
    m2 = jnp.max(el2, axis=-1, keepdims=True)
    i2 = jnp.min(jnp.where(el2 == m2, lane_f, big), axis=-1, keepdims=True)
    d = jnp.exp(m2 - m1)
    w1 = g_w / (1.0 + d)
    w2 = g_w * d / (1.0 + d)

    oh1 = lane_f == i1
    oh2 = lane_f == i2
    oh = jnp.where(oh1 | oh2, 1.0, 0.0)
    r = lax.broadcasted_iota(jnp.int32, (tm, tm), 0)
    c = lax.broadcasted_iota(jnp.int32, (tm, tm), 1)
    lower = jnp.where(c < r, 1.0, 0.0).astype(BF16)
    prefix = jnp.dot(lower, oh.astype(BF16), preferred_element_type=F32) + carry_ref[...]
    rank1 = jnp.sum(jnp.where(oh1, prefix, 0.0), axis=-1, keepdims=True)
    rank2 = jnp.sum(jnp.where(oh2, prefix, 0.0), axis=-1, keepdims=True)
    carry = carry_ref[...] + jnp.sum(oh, axis=0, keepdims=True)
    carry_ref[...] = carry
    cnt_ref[...] = jnp.broadcast_to(carry, cnt_ref.shape)

    meta = jnp.zeros(logits.shape, F32)
    for ln, val in enumerate((i1, i2, w1, w2, rank1, rank2)):
        meta = jnp.where(lane == ln, val, meta)
    meta_ref[...] = meta


def _post_attn(o2d, x2d, w_out, ffn_gain, wr_hi, wr_lo, b_r):
    t = x2d.shape[0]
    tm = ROW_TILE
    row = lambda i: (i, 0)
    const = lambda i: (0, 0)
    return pl.pallas_call(
        _post_attn_kernel,
        grid=(t // tm,),
        in_specs=[
            pl.BlockSpec((tm, D_MODEL), row),
            pl.BlockSpec((tm, D_MODEL), row),
            pl.BlockSpec((D_MODEL, D_MODEL), const),
            pl.BlockSpec((1, D_MODEL), const),
            pl.BlockSpec((D_MODEL, LANES), const),
            pl.BlockSpec((D_MODEL, LANES), const),
            pl.BlockSpec((1, LANES), const),
        ],
        out_specs=[pl.BlockSpec((tm, D_MODEL), row),
                   pl.BlockSpec((tm, D_MODEL), row),
                   pl.BlockSpec((tm, LANES), row),
                   pl.BlockSpec((8, LANES), const)],
        out_shape=[jax.ShapeDtypeStruct((t, D_MODEL), F32),
                   jax.ShapeDtypeStruct((t, D_MODEL), F32),
                   jax.ShapeDtypeStruct((t, LANES), F32),
                   jax.ShapeDtypeStruct((8, LANES), F32)],
        scratch_shapes=[pltpu.VMEM((1, LANES), F32)],
        compiler_params=_cparams(("arbitrary",)),
        name="post_attn_router",
    )(o2d, x2d, w_out, ffn_gain, wr_hi, wr_lo, b_r)


def _row_copy(src, dst, sem):
    return pltpu.make_async_copy(src, dst, sem)


def _dispatch_kernel(dest_ref, hn_ref, xs_in_ref, xs_ref, sem):
    del xs_in_ref
    tm = hn_ref.shape[0]
    base = pl.program_id(0) * (TOP_K * tm)

    def issue(t, carry):
        for kk in range(TOP_K):
            d = dest_ref[base + TOP_K * t + kk]
            _row_copy(hn_ref.at[pl.ds(t, 1)], xs_ref.at[pl.ds(d, 1)], sem).start()
        return carry

    lax.fori_loop(0, tm, issue, 0, unroll=ISSUE_UNROLL)
    for kk in range(TOP_K):
        _row_copy(hn_ref, xs_ref.at[pl.ds(0, tm)], sem).wait()


def _dispatch(dest, hn, xs_zero):
    t = hn.shape[0]
    tm = ROW_TILE
    grid_spec = pltpu.PrefetchScalarGridSpec(
        num_scalar_prefetch=1,
        grid=(t // tm,),
        in_specs=[pl.BlockSpec((tm, D_MODEL), lambda i, dest: (i, 0)),
                  pl.BlockSpec(memory_space=pl.ANY)],
        out_specs=pl.BlockSpec(memory_space=pl.ANY),
        scratch_shapes=[pltpu.SemaphoreType.DMA(())],
    )
    return pl.pallas_call(
        _dispatch_kernel,
        grid_spec=grid_spec,
        out_shape=jax.ShapeDtypeStruct(xs_zero.shape, xs_zero.dtype),
        input_output_aliases={2: 0},
        compiler_params=_cparams(("arbitrary",)),
        name="moe_dispatch",
    )(dest, hn, xs_zero)


def _expert_kernel(te_ref, tv_ref, xs_ref, wg_ref, wu_ref, wd_ref, ys_ref):
    i = pl.program_id(0)

    @pl.when(tv_ref[i] == 1)
    def _():
        x = xs_ref[...].astype(BF16)
        g = jnp.dot(x, wg_ref[0].astype(BF16), preferred_element_type=F32)
        u = jnp.dot(x, wu_ref[0].astype(BF16), preferred_element_type=F32)
        hid = g * jax.nn.sigmoid(g) * u
        ys_ref[...] = jnp.dot(hid.astype(BF16), wd_ref[0].astype(BF16),
                              preferred_element_type=F32)

    @pl.when(tv_ref[i] == 0)
    def _():
        ys_ref[...] = jnp.zeros(ys_ref.shape, ys_ref.dtype)


def _experts(tile_expert, tile_valid, xs, w_gate, w_up, w_down):
    p = xs.shape[0]
    tile = EXPERT_TILE
    grid_spec = pltpu.PrefetchScalarGridSpec(
        num_scalar_prefetch=2,
        grid=(p // tile,),
        in_specs=[
            pl.BlockSpec((tile, D_MODEL), lambda i, te, tv: (i, 0)),
            pl.BlockSpec((1, D_MODEL, EXPERT_FF), lambda i, te, tv: (te[i], 0, 0)),
            pl.BlockSpec((1, D_MODEL, EXPERT_FF), lambda i, te, tv: (te[i], 0, 0)),
            pl.BlockSpec((1, EXPERT_FF, D_MODEL), lambda i, te, tv: (te[i], 0, 0)),
        ],
        out_specs=pl.BlockSpec((tile, D_MODEL), lambda i, te, tv: (i, 0)),
    )
    return pl.pallas_call(
        _expert_kernel,
        grid_spec=grid_spec,
        out_shape=jax.ShapeDtypeStruct((p, D_MODEL), F32),
        compiler_params=_cparams(("arbitrary",)),
        name="moe_experts",
    )(tile_expert, tile_valid, xs, w_gate, w_up, w_down)


def _combine_kernel(dest_ref, x1_ref, meta_ref, ys_ref, out_ref, buf_ref, sem):
    tm = x1_ref.shape[0]
    base = pl.program_id(0) * (TOP_K * tm)

    def issue(t, carry):
        for kk in range(TOP_K):
            d = dest_ref[base + TOP_K * t + kk]
            _row_copy(ys_ref.at[pl.ds(d, 1)], buf_ref.at[kk, pl.ds(t, 1)], sem).start()
        return carry

    lax.fori_loop(0, tm, issue, 0, unroll=ISSUE_UNROLL)
    for kk in range(TOP_K):
        _row_copy(ys_ref.at[pl.ds(0, tm)], buf_ref.at[kk], sem).wait()

    meta = meta_ref[...]
    out_ref[...] = (x1_ref[...] + meta[:, 2:3] * buf_ref[0] + meta[:, 3:4] * buf_ref[1])


def _combine(dest, x1, meta, ys):
    t = x1.shape[0]
    tm = ROW_TILE
    grid_spec = pltpu.PrefetchScalarGridSpec(
        num_scalar_prefetch=1,
        grid=(t // tm,),
        in_specs=[pl.BlockSpec((tm, D_MODEL), lambda i, dest: (i, 0)),
                  pl.BlockSpec((tm, LANES), lambda i, dest: (i, 0)),
                  pl.BlockSpec(memory_space=pl.ANY)],
        out_specs=pl.BlockSpec((tm, D_MODEL), lambda i, dest: (i, 0)),
        scratch_shapes=[pltpu.VMEM((TOP_K, tm, D_MODEL), F32),
                        pltpu.SemaphoreType.DMA(())],
    )
    return pl.pallas_call(
        _combine_kernel,
        grid_spec=grid_spec,
        out_shape=jax.ShapeDtypeStruct((t, D_MODEL), F32),
        compiler_params=_cparams(("arbitrary",)),
        name="moe_combine",
    )(dest, x1, meta, ys)


def _rope_tables(seq):
    half = DIFF_HEAD_DIM // 2
    inv_freq = 1.0 / (ROPE_THETA ** (jnp.arange(0, DIFF_HEAD_DIM, 2, dtype=F32) / DIFF_HEAD_DIM))
    ang = jnp.arange(seq, dtype=F32)[:, None] * inv_freq[None, :]
    cos = jnp.cos(ang)
    sin = jnp.sin(ang)
    zero = jnp.zeros((seq, half), F32)
    diff_cos = jnp.concatenate([cos, cos, cos, cos], axis=-1)
    diff_sin = jnp.concatenate([-sin, -sin, sin, sin], axis=-1)
    mla_cos = jnp.concatenate([cos, zero, cos, zero], axis=-1)
    mla_sin = jnp.concatenate([-sin, zero, sin, zero], axis=-1)
    return diff_cos, diff_sin, mla_cos, mla_sin


def _diff_head_layout(w):
    lead = w.shape[:-1]
    w = w.reshape(*lead, DIFF_HEADS, 2, 2, DIFF_HEAD_DIM // 2)
    w = jnp.swapaxes(w, -3, -2)
    return w.reshape(*lead, DIFF_QK_WIDTH)


def _diff_gain_layout(g, scale):
    g = (g.astype(F32) * scale).reshape(2, DIFF_HEAD_DIM // 2)
    g = jnp.broadcast_to(g[:, None, :], (2, 2, DIFF_HEAD_DIM // 2))
    return jnp.tile(g.reshape(1, LANES), (1, MXU_DIM // LANES))


def _pad_rope(w):
    half = MLA_ROPE // 2
    z = jnp.zeros(w.shape[:-1] + (half,), w.dtype)
    return jnp.concatenate([w[..., :half], z, w[..., half:], z], axis=-1)


def _mla_qk_layout(w):
    return jnp.concatenate([w[..., :MLA_NOPE], _pad_rope(w[..., MLA_NOPE:])], axis=-1)


def _segment_matrix():
    lane = jnp.arange(MXU_DIM)
    key = (lane // LANES) * 2 + (lane // 32) % 2
    return (key[:, None] == key[None, :]).astype(BF16)


def _moe(x1, hn, meta, cnt, w_gate, w_up, w_down):
    t = x1.shape[0]
    tile = EXPERT_TILE
    n_slots = t * TOP_K + N_EXPERTS * tile
    n_tiles = n_slots // tile
    counts = cnt[0, :N_EXPERTS].astype(jnp.int32)
    padded = ((counts + tile - 1) // tile) * tile
    ends = jnp.cumsum(padded)
    offsets = ends - padded
    tile_start = jnp.arange(n_tiles, dtype=jnp.int32) * tile
    tile_expert = jnp.sum(tile_start[:, None] >= ends[None, :], axis=1).astype(jnp.int32)
    tile_valid = (tile_start < ends[-1]).astype(jnp.int32)
    tile_expert = jnp.minimum(tile_expert, N_EXPERTS - 1)
    ids = meta[:, 0:TOP_K].astype(jnp.int32)
    rank = meta[:, 4:4 + TOP_K].astype(jnp.int32)
    onehot = ids[..., None] == jnp.arange(N_EXPERTS, dtype=jnp.int32)
    dest = (rank + jnp.sum(jnp.where(onehot, offsets, 0), axis=-1)).reshape(-1)
    xs = _dispatch(dest, hn, jnp.zeros((n_slots, D_MODEL), F32))
    ys = _experts(tile_expert, tile_valid, xs, w_gate, w_up, w_down)
    return _combine(dest, x1, meta, ys)


def _router_weights(w_group, b_group, w_expert, b_expert):
    pad = LANES - N_EXPERTS - N_GROUPS
    w = jnp.concatenate([w_expert, w_group, jnp.zeros((D_MODEL, pad), F32)], axis=-1)
    b = jnp.concatenate([b_expert, b_group, jnp.zeros((pad,), F32)]).reshape(1, LANES)
    w_hi = w.astype(BF16)
    w_lo = (w - w_hi.astype(F32)).astype(BF16)
    return w_hi, w_lo, b


def kernel(x, attn_norm, ffn_norm, diff_w_in, diff_q_norm, diff_k_norm, diff_lambda_q1, diff_lambda_k1, diff_lambda_q2, diff_lambda_k2, diff_subln, diff_w_out, mla_w_a, mla_q_a_norm, mla_kv_a_norm, mla_w_qb, mla_w_kvb, mla_q_norm, mla_k_norm, mla_w_out, moe_w_group, moe_b_group, moe_w_expert, moe_b_expert, moe_w_gate, moe_w_up, moe_w_down):
    b, s, d = x.shape
    assert d == D_MODEL and s % Q_TILE == 0 and s % ROW_TILE == 0
    t = b * s
    diff_cos, diff_sin, mla_cos, mla_sin = _rope_tables(s)
    row = lambda v: v.astype(F32).reshape(1, -1)
    x2d = x.reshape(t, d)

    lambda_init = 0.8 - 0.6 * math.exp(-0.3 * 0)
    w_in = diff_w_in[0]
    w_in = jnp.concatenate([_diff_head_layout(w_in[:, :DIFF_QK_WIDTH]),
                            _diff_head_layout(w_in[:, DIFF_QK_WIDTH:2 * DIFF_QK_WIDTH]),
                            w_in[:, 2 * DIFF_QK_WIDTH:]], axis=-1).astype(BF16)
    q, k, vt = _diff_qkv(x2d, row(attn_norm[0]), w_in,
                         _diff_gain_layout(diff_q_norm[0], DIFF_HEAD_DIM ** -0.5 * LOG2E),
                         _diff_gain_layout(diff_k_norm[0], 1.0),
                         diff_cos, diff_sin, _segment_matrix(), s)
    shp = (b, s, DIFF_QK_WIDTH)
    o = _flash(q.reshape(shp), k.reshape(shp), vt, DIFF_HEADS, LANES, Q_TILE,
               diff_params=(row(diff_lambda_q1[0]), row(diff_lambda_k1[0]),
                            row(diff_lambda_q2[0]), row(diff_lambda_k2[0]),
                            row(diff_subln[0])),
               lambda_init=lambda_init)
    wr_hi, wr_lo, b_r = _router_weights(moe_w_group[0], moe_b_group[0],
                                        moe_w_expert[0], moe_b_expert[0])
    x1, hn, meta, cnt = _post_attn(o.reshape(t, d), x2d, diff_w_out[0].astype(BF16),
                                   row(ffn_norm[0]), wr_hi, wr_lo, b_r)
    x2d = _moe(x1, hn, meta, cnt, moe_w_gate[0], moe_w_up[0], moe_w_down[0])

    w_a = mla_w_a[0]
    split = MLA_Q_LORA + MLA_KV_LORA
    w_a = jnp.concatenate([w_a[:, :split], _pad_rope(w_a[:, split:])], axis=-1).astype(BF16)
    w_qb = _mla_qk_layout(mla_w_qb[0].reshape(MLA_Q_LORA, MLA_HEADS, MLA_QK_DIM))
    w_qb = w_qb.reshape(MLA_Q_LORA, MLA_HEADS * MLA_QK_PAD).astype(BF16)
    w_kvb = mla_w_kvb[0].reshape(MLA_KV_LORA, MLA_HEADS, 2 * MLA_NOPE)
    w_kvb = jnp.concatenate([w_kvb[..., :MLA_NOPE].reshape(MLA_KV_LORA, -1),
                             w_kvb[..., MLA_NOPE:].reshape(MLA_KV_LORA, -1)], axis=-1).astype(BF16)
    q_gain = _mla_qk_layout(mla_q_norm[0].astype(F32) * (MLA_QK_DIM ** -0.5 * LOG2E)).reshape(1, -1)
    k_gain = _mla_qk_layout(mla_k_norm[0].astype(F32)).reshape(1, -1)
    q, k, vt = _mla_proj(x2d, row(attn_norm[1]), w_a, row(mla_q_a_norm[0]), row(mla_kv_a_norm[0]),
                        w_qb, w_kvb, q_gain, k_gain, mla_cos, mla_sin, s)
    qk_shp = (b, s, MLA_HEADS * MLA_QK_PAD)
    o = _flash(q.reshape(qk_shp), k.reshape(qk_shp), vt, MLA_HEADS, MLA_QK_PAD, Q_TILE)
    wr_hi, wr_lo, b_r = _router_weights(moe_w_group[1], moe_b_group[1],
                                        moe_w_expert[1], moe_b_expert[1])
    x1, hn, meta, cnt = _post_attn(o.reshape(t, d), x2d, mla_w_out[0].astype(BF16),
                                   row(ffn_norm[1]), wr_hi, wr_lo, b_r)
    x2d = _moe(x1, hn, meta, cnt, moe_w_gate[1], moe_w_up[1], moe_w_down[1])
    return x2d.reshape(b, s, d)
```

```python
import functools
import math

import jax
import jax.numpy as jnp
from jax import lax
from jax.experimental import pallas as pl
from jax.experimental.pallas import tpu as pltpu

F32 = jnp.float32
BF16 = jnp.bfloat16

D_MODEL = 1024
PACKED = D_MODEL // 2
ROPE_THETA = 10000.0
NORM_EPS = 1e-6
LANES = 128
MXU_DIM = 256

DIFF_HEAD_DIM = 64
DIFF_HEADS = 8
DIFF_QK_WIDTH = 1024

MLA_NOPE = 128
MLA_ROPE = 64
MLA_QK_DIM = 192
MLA_QK_PAD = 256
MLA_HEADS = 8
MLA_Q_LORA = 384
MLA_KV_LORA = 256

N_GROUPS = 8
EXPERTS_PER_GROUP = 8
N_EXPERTS = 64
TOP_K = 2
EXPERT_FF = 256

ROW_TILE = 512
Q_TILE = 1024
FLASH_CHUNK = 512
V_ROWS = 144
LOG2E = math.log2(math.e)
EXPERT_TILE = 256
ISSUE_UNROLL = 8
VMEM_LIMIT = 56 * 1024 * 1024


def _cparams(sem):
    return pltpu.CompilerParams(dimension_semantics=sem, vmem_limit_bytes=VMEM_LIMIT)


def _rms(x, gain):
    return x * lax.rsqrt(jnp.mean(x * x, axis=-1, keepdims=True) + NORM_EPS) * gain


def _pack_bf16_pairs(x):
    lo = lax.bitcast_convert_type(x[:, :PACKED].astype(BF16).astype(F32), jnp.uint32)
    hi = lax.bitcast_convert_type(x[:, PACKED:].astype(BF16).astype(F32), jnp.uint32)
    return (lo >> 16) | (hi & jnp.uint32(0xFFFF0000))


def _unpack_bf16_pairs(p):
    lo = lax.bitcast_convert_type(p << 16, F32)
    hi = lax.bitcast_convert_type(p & jnp.uint32(0xFFFF0000), F32)
    return lo, hi


def _store_v_transposed(v, vt_ref, heads):
    tm = v.shape[0]
    ones = jnp.ones((V_ROWS - LANES, tm), BF16)
    for hd in range(heads):
        lo = hd * V_ROWS
        vt_ref[0, 0, lo:lo + LANES, :] = v[:, hd * LANES:(hd + 1) * LANES].T.astype(BF16)
        vt_ref[0, 0, lo + LANES:lo + V_ROWS, :] = ones


def _rope128(u, cos, sin):
    return u * cos + pltpu.roll(u, 64, 1) * sin


def _diff_qkv_kernel(x_ref, g_ref, w_ref, qg_ref, kg_ref, cos_ref, sin_ref, seg_ref,
                     q_ref, k_ref, vt_ref):
    h = _rms(x_ref[...], g_ref[...])
    qkv = jnp.dot(h.astype(BF16), w_ref[...], preferred_element_type=F32)
    cos = cos_ref[...]
    sin = sin_ref[...]
    seg = seg_ref[...]
    for off, gain_ref, out_ref in ((0, qg_ref, q_ref), (DIFF_QK_WIDTH, kg_ref, k_ref)):
        for c in range(DIFF_QK_WIDTH // MXU_DIM):
            t = qkv[:, off + c * MXU_DIM: off + (c + 1) * MXU_DIM]
            ss = jnp.dot((t * t).astype(BF16), seg, preferred_element_type=F32)
            tn = t * lax.rsqrt(ss * (1.0 / DIFF_HEAD_DIM) + NORM_EPS) * gain_ref[...]
            for half in range(MXU_DIM // LANES):
                u = tn[:, half * LANES:(half + 1) * LANES]
                lo = c * MXU_DIM + half * LANES
                out_ref[:, lo:lo + LANES] = _rope128(u, cos, sin).astype(BF16)
    _store_v_transposed(qkv[:, 2 * DIFF_QK_WIDTH:], vt_ref, DIFF_HEADS)


def _diff_qkv(x2d, gain, w_in, q_gain, k_gain, cos, sin, seg, seq):
    t = x2d.shape[0]
    tm = ROW_TILE
    pos_tiles = seq // tm
    row = lambda i: (i, 0)
    const = lambda i: (0, 0)
    out = jax.ShapeDtypeStruct((t, DIFF_QK_WIDTH), BF16)
    vt_out = jax.ShapeDtypeStruct((t // seq, pos_tiles, DIFF_HEADS * V_ROWS, tm), BF16)
    vt_spec = pl.BlockSpec((1, 1, DIFF_HEADS * V_ROWS, tm),
                           lambda i: (i // pos_tiles, i % pos_tiles, 0, 0))
    return pl.pallas_call(
        _diff_qkv_kernel,
        grid=(t // tm,),
        in_specs=[
            pl.BlockSpec((tm, D_MODEL), row),
            pl.BlockSpec((1, D_MODEL), const),
            pl.BlockSpec(w_in.shape, const),
            pl.BlockSpec((1, MXU_DIM), const),
            pl.BlockSpec((1, MXU_DIM), const),
            pl.BlockSpec((tm, LANES), lambda i: (i % pos_tiles, 0)),
            pl.BlockSpec((tm, LANES), lambda i: (i % pos_tiles, 0)),
            pl.BlockSpec((MXU_DIM, MXU_DIM), const),
        ],
        out_specs=[pl.BlockSpec((tm, DIFF_QK_WIDTH), row)] * 2 + [vt_spec],
        out_shape=[out, out, vt_out],
        compiler_params=_cparams(("parallel",)),
        name="diff_qkv",
    )(x2d, gain, w_in, q_gain, k_gain, cos, sin, seg)


def _mla_proj_kernel(x_ref, g_ref, wa_ref, qag_ref, kvag_ref, wqb_ref, wkvb_ref,
                     qg_ref, kg_ref, cos_ref, sin_ref, q_ref, k_ref, vt_ref):
    h = _rms(x_ref[...], g_ref[...])
    a = jnp.dot(h.astype(BF16), wa_ref[...], preferred_element_type=F32)
    cq = _rms(a[:, :MLA_Q_LORA], qag_ref[...])
    ckv = _rms(a[:, MLA_Q_LORA:MLA_Q_LORA + MLA_KV_LORA], kvag_ref[...])
    kpe = a[:, MLA_Q_LORA + MLA_KV_LORA:]
    q = jnp.dot(cq.astype(BF16), wqb_ref[...], preferred_element_type=F32)
    kv = jnp.dot(ckv.astype(BF16), wkvb_ref[...], preferred_element_type=F32)
    cos = cos_ref[...]
    sin = sin_ref[...]
    qg = qg_ref[...]
    kg = kg_ref[...]
    kpe_ss = jnp.sum(kpe * kpe, axis=-1, keepdims=True)
    kpe_rot = _rope128(kpe * kg[:, MLA_NOPE:], cos, sin)
    inv_d = 1.0 / MLA_QK_DIM
    for hd in range(MLA_HEADS):
        lo = hd * MLA_QK_PAD
        qh = q[:, lo:lo + MLA_QK_PAD]
        rq = lax.rsqrt(jnp.sum(qh * qh, axis=-1, keepdims=True) * inv_d + NORM_EPS)
        q_ref[:, lo:lo + MLA_NOPE] = (qh[:, :MLA_NOPE] * rq * qg[:, :MLA_NOPE]).astype(BF16)
        q_ref[:, lo + MLA_NOPE:lo + MLA_QK_PAD] = _rope128(
            qh[:, MLA_NOPE:] * rq * qg[:, MLA_NOPE:], cos, sin).astype(BF16)
        kn = kv[:, hd * MLA_NOPE:(hd + 1) * MLA_NOPE]
        rk = lax.rsqrt((jnp.sum(kn * kn, axis=-1, keepdims=True) + kpe_ss) * inv_d + NORM_EPS)
        k_ref[:, lo:lo + MLA_NOPE] = (kn * rk * kg[:, :MLA_NOPE]).astype(BF16)
        k_ref[:, lo + MLA_NOPE:lo + MLA_QK_PAD] = (kpe_rot * rk).astype(BF16)
    _store_v_transposed(kv[:, MLA_HEADS * MLA_NOPE:], vt_ref, MLA_HEADS)


def _mla_proj(x2d, gain, w_a, qa_gain, kva_gain, w_qb, w_kvb, q_gain, k_gain, cos, sin, seq):
    t = x2d.shape[0]
    tm = ROW_TILE
    pos_tiles = seq // tm
    row = lambda i: (i, 0)
    const = lambda i: (0, 0)
    qk_out = jax.ShapeDtypeStruct((t, MLA_HEADS * MLA_QK_PAD), BF16)
    vt_out = jax.ShapeDtypeStruct((t // seq, pos_tiles, MLA_HEADS * V_ROWS, tm), BF16)
    return pl.pallas_call(
        _mla_proj_kernel,
        grid=(t // tm,),
        in_specs=[
            pl.BlockSpec((tm, D_MODEL), row),
            pl.BlockSpec((1, D_MODEL), const),
            pl.BlockSpec(w_a.shape, const),
            pl.BlockSpec((1, MLA_Q_LORA), const),
            pl.BlockSpec((1, MLA_KV_LORA), const),
            pl.BlockSpec(w_qb.shape, const),
            pl.BlockSpec(w_kvb.shape, const),
            pl.BlockSpec((1, MLA_QK_PAD), const),
            pl.BlockSpec((1, MLA_QK_PAD), const),
            pl.BlockSpec((tm, LANES), lambda i: (i % pos_tiles, 0)),
            pl.BlockSpec((tm, LANES), lambda i: (i % pos_tiles, 0)),
        ],
        out_specs=[pl.BlockSpec((tm, MLA_HEADS * MLA_QK_PAD), row),
                   pl.BlockSpec((tm, MLA_HEADS * MLA_QK_PAD), row),
                   pl.BlockSpec((1, 1, MLA_HEADS * V_ROWS, tm),
                                lambda i: (i // pos_tiles, i % pos_tiles, 0, 0))],
        out_shape=[qk_out, qk_out, vt_out],
        compiler_params=_cparams(("parallel",)),
        name="mla_proj",
    )(x2d, gain, w_a, qa_gain, kva_gain, w_qb, w_kvb, q_gain, k_gain, cos, sin)


def _flash_kernel(*refs, diff, lambda_init, tq, tk, n_chunks):
    if diff:
        (lq1_ref, lk1_ref, lq2_ref, lk2_ref, subln_ref, q_ref, k_ref, vt_ref, o_ref,
         qt_ref, m_ref, acc_ref, sa_ref, sb_ref) = refs
    else:
        q_ref, k_ref, vt_ref, o_ref, qt_ref, m_ref, acc_ref, sa_ref, sb_ref = refs
    qi = pl.program_id(2)
    ch = FLASH_CHUNK
    copies = 2 if diff else 1
    chunks_per_copy = n_chunks // copies

    q_t = q_ref[0].astype(F32).T
    if diff:
        feat = lax.broadcasted_iota(jnp.int32, q_t.shape, 0)
        first = ((feat // 32) % 2) == 0
        qt_ref[:, :tq] = jnp.where(first, q_t, 0.0).astype(BF16)
        qt_ref[:, tq:] = jnp.where(first, 0.0, q_t).astype(BF16)
    else:
        qt_ref[...] = q_t.astype(BF16)
    m_ref[...] = jnp.full(m_ref.shape, -jnp.inf, F32)
    acc_ref[...] = jnp.zeros(acc_ref.shape, F32)

    def key_tile(j):
        return k_ref[0, pl.ds(pl.multiple_of(j * tk, tk), tk), :]

    def scores(j, s_ref, chunks):
        k = key_tile(j)
        for c in chunks:
            s_ref[c] = jnp.dot(k, qt_ref[:, c * ch:(c + 1) * ch], preferred_element_type=F32)

    def accumulate(j, s_ref, chunks, masked=()):
        vt = vt_ref[0, j]
        for c in chunks:
            s = s_ref[c]
            if c in masked:
                key = lax.broadcasted_iota(jnp.int32, (tk, ch), 0)
                qry = lax.broadcasted_iota(jnp.int32, (tk, ch), 1)
                s = jnp.where(key <= qry, s, -jnp.inf)
            m_prev = m_ref[c]
            m_new = jnp.maximum(m_prev, jnp.max(s, axis=0, keepdims=True))
            alpha = jnp.exp2(m_prev - m_new)
            p = jnp.exp2(s - m_new).astype(BF16)
            acc_ref[c] = alpha * acc_ref[c] + jnp.dot(vt, p, preferred_element_type=F32)
            m_ref[c] = m_new

    every = tuple(range(n_chunks))
    early = tuple(c for c in every if c % chunks_per_copy == 0)
    late = tuple(c for c in every if c % chunks_per_copy == 1)
    scores(0, sa_ref, every)

    def pair_body(i, carry):
        j = 2 * i
        scores(j + 1, sb_ref, every)
        accumulate(j, sa_ref, every)
        scores(j + 2, sa_ref, every)
        accumulate(j + 1, sb_ref, every)
        return carry

    lax.fori_loop(0, qi, pair_body, 0)
    j = 2 * qi
    scores(j + 1, sb_ref, late)
    accumulate(j, sa_ref, every, masked=early)
    accumulate(j + 1, sb_ref, late, masked=late)

    def normalized(c):
        a = acc_ref[c]
        return a[:LANES] * (1.0 / a[LANES:LANES + 1])

    if diff:
        lam = (jnp.exp(jnp.sum(lq1_ref[...] * lk1_ref[...], axis=-1, keepdims=True))
               - jnp.exp(jnp.sum(lq2_ref[...] * lk2_ref[...], axis=-1, keepdims=True))
               + lambda_init)
        for c in range(chunks_per_copy):
            o_t = normalized(c) - lam * normalized(c + chunks_per_copy)
            o_t = o_t * lax.rsqrt(jnp.mean(o_t * o_t, axis=0, keepdims=True) + NORM_EPS)
            o = o_t.T * (subln_ref[...] * (1.0 - lambda_init))
            o_ref[0, c * ch:(c + 1) * ch, :] = o.astype(o_ref.dtype)
    else:
        for c in range(n_chunks):
            o_ref[0, c * ch:(c + 1) * ch, :] = normalized(c).T.astype(o_ref.dtype)


def _flash(q, k, vt, heads, dk, tq, diff_params=None, lambda_init=0.0):
    b, s, _ = q.shape
    tk = ROW_TILE
    diff = diff_params is not None
    n_chunks = (2 if diff else 1) * tq // FLASH_CHUNK
    q_spec = pl.BlockSpec((1, tq, dk), lambda bi, hi, qi: (bi, qi, hi))
    k_spec = pl.BlockSpec((1, s, dk), lambda bi, hi, qi: (bi, 0, hi))
    v_spec = pl.BlockSpec((1, s // tk, V_ROWS, tk), lambda bi, hi, qi: (bi, 0, hi, 0))
    o_spec = pl.BlockSpec((1, tq, LANES), lambda bi, hi, qi: (bi, qi, hi))
    small = lambda shape: pl.BlockSpec(shape, lambda bi, hi, qi: (0, 0))
    in_specs = [q_spec, k_spec, v_spec]
    args = [q, k, vt]
    if diff:
        in_specs = [small((1, DIFF_HEAD_DIM))] * 4 + [small((1, LANES))] + in_specs
        args = list(diff_params) + args
    assert tk == FLASH_CHUNK and tq == 2 * tk
    scratch = [pltpu.VMEM((dk, n_chunks * FLASH_CHUNK), BF16),
               pltpu.VMEM((n_chunks, 1, FLASH_CHUNK), F32),
               pltpu.VMEM((n_chunks, V_ROWS, FLASH_CHUNK), F32),
               pltpu.VMEM((n_chunks, tk, FLASH_CHUNK), F32),
               pltpu.VMEM((n_chunks, tk, FLASH_CHUNK), F32)]
    return pl.pallas_call(
        functools.partial(_flash_kernel, diff=diff, lambda_init=lambda_init, tq=tq, tk=tk,
                          n_chunks=n_chunks),
        grid=(b, heads, s // tq),
        in_specs=in_specs,
        out_specs=o_spec,
        out_shape=jax.ShapeDtypeStruct((b, s, heads * LANES), BF16),
        scratch_shapes=scratch,
        compiler_params=_cparams(("parallel", "parallel", "arbitrary")),
        name="diff_flash" if diff else "mla_flash",
    )(*args)


def _post_attn_kernel(o_ref, x_ref, wo_ref, fg_ref, wrh_ref, wrl_ref, br_ref,
                      x1_ref, hn_ref, meta_ref, cnt_ref, carry_ref):
    i = pl.program_id(0)

    @pl.when(i == 0)
    def _():
        carry_ref[...] = jnp.zeros(carry_ref.shape, F32)

    x1 = x_ref[...] + jnp.dot(o_ref[...], wo_ref[...], preferred_element_type=F32)
    x1_ref[...] = x1
    hn = _rms(x1, fg_ref[...])
    hn_ref[...] = _pack_bf16_pairs(hn)

    hi = hn.astype(BF16)
    lo = (hn - hi.astype(F32)).astype(BF16)
    wrh = wrh_ref[...]
    logits = (jnp.dot(hi, wrh, preferred_element_type=F32)
              + jnp.dot(lo, wrh, preferred_element_type=F32)
              + jnp.dot(hi, wrl_ref[...], preferred_element_type=F32)
              + br_ref[...])
    tm = logits.shape[0]
    lane = lax.broadcasted_iota(jnp.int32, logits.shape, 1)
    lane_f = lane.astype(F32)
    neg = jnp.float32(-jnp.inf)
    big = jnp.float32(1e9)

    is_group = (lane >= N_EXPERTS) & (lane < N_EXPERTS + N_GROUPS)
    gl = jnp.where(is_group, logits, neg)
    gmax = jnp.max(gl, axis=-1, keepdims=True)
    gidx = jnp.min(jnp.where(gl == gmax, lane_f, big), axis=-1, keepdims=True) - N_EXPERTS
    g_w = 1.0 / jnp.sum(jnp.exp(gl - gmax), axis=-1, keepdims=True)

    in_group = (lane < N_EXPERTS) & ((lane // EXPERTS_PER_GROUP).astype(F32) == gidx)
    el = jnp.where(in_group, logits, neg)
    m1 = jnp.max(el, axis=-1, keepdims=True)
    i1 = jnp.min(jnp.where(el == m1, lane_f, big), axis=-1, keepdims=True)
    el2 = jnp.where(lane_f == i1, neg, el)
    m2 = jnp.max(el2, axis=-1, keepdims=True)
    i2 = jnp.min(jnp.where(el2 == m2, lane_f, big), axis=-1, keepdims=True)
    d = jnp.exp(m2 - m1)
    w1 = g_w / (1.0 + d)
    w2 = g_w * d / (1.0 + d)

    oh1 = lane_f == i1
    oh2 = lane_f == i2
    oh = jnp.where(oh1 | oh2, 1.0, 0.0)
    r = lax.broadcasted_iota(jnp.int32, (tm, tm), 0)
    c = lax.broadcasted_iota(jnp.int32, (tm, tm), 1)
    lower = jnp.where(c < r, 1.0, 0.0).astype(BF16)
    prefix = jnp.dot(lower, oh.astype(BF16), preferred_element_type=F32) + carry_ref[...]
    rank1 = jnp.sum(jnp.where(oh1, prefix, 0.0), axis=-1, keepdims=True)
    rank2 = jnp.sum(jnp.where(oh2, prefix, 0.0), axis=-1, keepdims=True)
    carry = carry_ref[...] + jnp.sum(oh, axis=0, keepdims=True)
    carry_ref[...] = carry
    cnt_ref[...] = jnp.broadcast_to(carry, cnt_ref.shape)

    meta = jnp.zeros(logits.shape, F32)
    for ln, val in enumerate((i1, i2, w1, w2, rank1, rank2)):
        meta = jnp.where(lane == ln, val, meta)
    meta_ref[...] = meta


def _post_attn(o2d, x2d, w_out, ffn_gain, wr_hi, wr_lo, b_r):
    t = x2d.shape[0]
    tm = ROW_TILE
    row = lambda i: (i, 0)
    const = lambda i: (0, 0)
    return pl.pallas_call(
        _post_attn_kernel,
        grid=(t // tm,),
        in_specs=[
            pl.BlockSpec((tm, D_MODEL), row),
            pl.BlockSpec((tm, D_MODEL), row),
            pl.BlockSpec((D_MODEL, D_MODEL), const),
            pl.BlockSpec((1, D_MODEL), const),
            pl.BlockSpec((D_MODEL, LANES), const),
            pl.BlockSpec((D_MODEL, LANES), const),
            pl.BlockSpec((1, LANES), const),
        ],
        out_specs=[pl.BlockSpec((tm, D_MODEL), row),
                   pl.BlockSpec((tm, PACKED), row),
                   pl.BlockSpec((tm, LANES), row),
                   pl.BlockSpec((8, LANES), const)],
        out_shape=[jax.ShapeDtypeStruct((t, D_MODEL), F32),
                   jax.ShapeDtypeStruct((t, PACKED), jnp.uint32),
                   jax.ShapeDtypeStruct((t, LANES), F32),
                   jax.ShapeDtypeStruct((8, LANES), F32)],
        scratch_shapes=[pltpu.VMEM((1, LANES), F32)],
        compiler_params=_cparams(("arbitrary",)),
        name="post_attn_router",
    )(o2d, x2d, w_out, ffn_gain, wr_hi, wr_lo, b_r)


def _row_copy(src, dst, sem):
    return pltpu.make_async_copy(src, dst, sem)


def _dispatch_kernel(dest_ref, pad_start_ref, pad_count_ref, used_tiles_ref, hn_ref, xs_ref,
                     zeros_ref, sem, pad_sem):
    tm = hn_ref.shape[0]
    tile = zeros_ref.shape[0]
    step = pl.program_id(0)
    base = step * (TOP_K * tm)

    @pl.when(step == 0)
    def _():
        zeros_ref[...] = jnp.zeros(zeros_ref.shape, zeros_ref.dtype)

        def fill_expert(e, carry):
            start = pad_start_ref[e]
            count = pad_count_ref[e]

            def start_row(r, c):
                _row_copy(zeros_ref.at[pl.ds(0, 1)], xs_ref.at[pl.ds(start + r, 1)], pad_sem).start()
                return c

            def wait_row(r, c):
                _row_copy(zeros_ref.at[pl.ds(0, 1)], xs_ref.at[pl.ds(0, 1)], pad_sem).wait()
                return c

            lax.fori_loop(0, count, start_row, 0)
            lax.fori_loop(0, count, wait_row, 0)
            return carry

        lax.fori_loop(0, N_EXPERTS, fill_expert, 0)

        def fill_tile(tl, carry):
            row0 = pl.multiple_of(tl * tile, tile)
            copy = _row_copy(zeros_ref, xs_ref.at[pl.ds(row0, tile)], pad_sem)
            copy.start()
            copy.wait()
            return carry

        lax.fori_loop(used_tiles_ref[0], xs_ref.shape[0] // tile, fill_tile, 0)

    def issue(t, carry):
        for kk in range(TOP_K):
            d = dest_ref[base + TOP_K * t + kk]
            _row_copy(hn_ref.at[pl.ds(t, 1)], xs_ref.at[pl.ds(d, 1)], sem).start()
        return carry

    lax.fori_loop(0, tm, issue, 0, unroll=ISSUE_UNROLL)
    for kk in range(TOP_K):
        _row_copy(hn_ref, xs_ref.at[pl.ds(0, tm)], sem).wait()


def _dispatch(dest, pad_start, pad_count, used_tiles, hn, n_slots):
    t = hn.shape[0]
    tm = ROW_TILE
    grid_spec = pltpu.PrefetchScalarGridSpec(
        num_scalar_prefetch=4,
        grid=(t // tm,),
        in_specs=[pl.BlockSpec((tm, PACKED), lambda i, *_: (i, 0))],
        out_specs=pl.BlockSpec(memory_space=pl.ANY),
        scratch_shapes=[pltpu.VMEM((EXPERT_TILE, PACKED), jnp.uint32),
                        pltpu.SemaphoreType.DMA(()),
                        pltpu.SemaphoreType.DMA(())],
    )
    return pl.pallas_call(
        _dispatch_kernel,
        grid_spec=grid_spec,
        out_shape=jax.ShapeDtypeStruct((n_slots, PACKED), jnp.uint32),
        compiler_params=_cparams(("arbitrary",)),
        name="moe_dispatch",
    )(dest, pad_start, pad_count, used_tiles, hn)


def _expert_kernel(te_ref, tv_ref, xs_ref, wg_ref, wu_ref, wd_ref, ys_ref):
    i = pl.program_id(0)

    @pl.when(tv_ref[i] == 1)
    def _():
        x_lo, x_hi = _unpack_bf16_pairs(xs_ref[...])
        x = jnp.concatenate([x_lo.astype(BF16), x_hi.astype(BF16)], axis=-1)
        g = jnp.dot(x, wg_ref[0, 0].astype(BF16), preferred_element_type=F32)
        u = jnp.dot(x, wu_ref[0, 0].astype(BF16), preferred_element_type=F32)
        hid = g * jax.nn.sigmoid(g) * u
        y = jnp.dot(hid.astype(BF16), wd_ref[0, 0].astype(BF16), preferred_element_type=F32)
        ys_ref[...] = _pack_bf16_pairs(y)

    @pl.when(tv_ref[i] == 0)
    def _():
        ys_ref[...] = jnp.zeros(ys_ref.shape, ys_ref.dtype)


def _experts(tile_expert, tile_valid, xs, w_gate, w_up, w_down, layer):
    p = xs.shape[0]
    tile = EXPERT_TILE
    grid_spec = pltpu.PrefetchScalarGridSpec(
        num_scalar_prefetch=2,
        grid=(p // tile,),
        in_specs=[
            pl.BlockSpec((tile, PACKED), lambda i, te, tv: (i, 0)),
            pl.BlockSpec((1, 1, D_MODEL, EXPERT_FF), lambda i, te, tv: (layer, te[i], 0, 0)),
            pl.BlockSpec((1, 1, D_MODEL, EXPERT_FF), lambda i, te, tv: (layer, te[i], 0, 0)),
            pl.BlockSpec((1, 1, EXPERT_FF, D_MODEL), lambda i, te, tv: (layer, te[i], 0, 0)),
        ],
        out_specs=pl.BlockSpec((tile, PACKED), lambda i, te, tv: (i, 0)),
    )
    return pl.pallas_call(
        _expert_kernel,
        grid_spec=grid_spec,
        out_shape=jax.ShapeDtypeStruct((p, PACKED), jnp.uint32),
        compiler_params=_cparams(("arbitrary",)),
        name="moe_experts",
    )(tile_expert, tile_valid, xs, w_gate, w_up, w_down)


def _combine_kernel(dest_ref, x1_ref, meta_ref, ys_ref, out_ref, buf_ref, sem):
    tm = x1_ref.shape[0]
    base = pl.program_id(0) * (TOP_K * tm)

    def issue(t, carry):
        for kk in range(TOP_K):
            d = dest_ref[base + TOP_K * t + kk]
            _row_copy(ys_ref.at[pl.ds(d, 1)], buf_ref.at[kk, pl.ds(t, 1)], sem).start()
        return carry

    lax.fori_loop(0, tm, issue, 0, unroll=ISSUE_UNROLL)
    for kk in range(TOP_K):
        _row_copy(ys_ref.at[pl.ds(0, tm)], buf_ref.at[kk], sem).wait()

    meta = meta_ref[...]
    w1 = meta[:, 2:3]
    w2 = meta[:, 3:4]
    lo1, hi1 = _unpack_bf16_pairs(buf_ref[0])
    lo2, hi2 = _unpack_bf16_pairs(buf_ref[1])
    out_ref[:, :PACKED] = x1_ref[:, :PACKED] + w1 * lo1 + w2 * lo2
    out_ref[:, PACKED:] = x1_ref[:, PACKED:] + w1 * hi1 + w2 * hi2


def _combine(dest, x1, meta, ys):
    t = x1.shape[0]
    tm = ROW_TILE
    grid_spec = pltpu.PrefetchScalarGridSpec(
        num_scalar_prefetch=1,
        grid=(t // tm,),
        in_specs=[pl.BlockSpec((tm, D_MODEL), lambda i, dest: (i, 0)),
                  pl.BlockSpec((tm, LANES), lambda i, dest: (i, 0)),
                  pl.BlockSpec(memory_space=pl.ANY)],
        out_specs=pl.BlockSpec((tm, D_MODEL), lambda i, dest: (i, 0)),
        scratch_shapes=[pltpu.VMEM((TOP_K, tm, PACKED), jnp.uint32),
                        pltpu.SemaphoreType.DMA(())],
    )
    return pl.pallas_call(
        _combine_kernel,
        grid_spec=grid_spec,
        out_shape=jax.ShapeDtypeStruct((t, D_MODEL), F32),
        compiler_params=_cparams(("arbitrary",)),
        name="moe_combine",
    )(dest, x1, meta, ys)


def _rope_tables(seq):
    inv_freq = 1.0 / (ROPE_THETA ** (jnp.arange(0, DIFF_HEAD_DIM, 2, dtype=F32) / DIFF_HEAD_DIM))
    ang = jnp.arange(seq, dtype=F32)[:, None] * jnp.tile(inv_freq, 4)[None, :]
    sign = jnp.where(jnp.arange(LANES) < LANES // 2, -1.0, 1.0).astype(F32)
    return jnp.cos(ang), jnp.sin(ang) * sign[None, :]


def _diff_head_layout(w):
    lead = w.shape[:-1]
    w = w.reshape(*lead, DIFF_HEADS, 2, 2, DIFF_HEAD_DIM // 2)
    w = jnp.swapaxes(w, -3, -2)
    return w.reshape(*lead, DIFF_QK_WIDTH)


def _diff_gain_layout(g, scale):
    g = (g.astype(F32) * scale).reshape(2, DIFF_HEAD_DIM // 2)
    g = jnp.broadcast_to(g[:, None, :], (2, 2, DIFF_HEAD_DIM // 2))
    return jnp.tile(g.reshape(1, LANES), (1, MXU_DIM // LANES))


def _pad_rope(w):
    half = MLA_ROPE // 2
    z = jnp.zeros(w.shape[:-1] + (half,), w.dtype)
    return jnp.concatenate([w[..., :half], z, w[..., half:], z], axis=-1)


def _mla_qk_layout(w):
    return jnp.concatenate([w[..., :MLA_NOPE], _pad_rope(w[..., MLA_NOPE:])], axis=-1)


def _segment_matrix():
    lane = jnp.arange(MXU_DIM)
    key = (lane // LANES) * 2 + (lane // 32) % 2
    return (key[:, None] == key[None, :]).astype(BF16)


def _moe(x1, hn, meta, cnt, w_gate, w_up, w_down, layer):
    t = x1.shape[0]
    tile = EXPERT_TILE
    n_slots = t * TOP_K + N_EXPERTS * tile
    n_tiles = n_slots // tile
    counts = cnt[0, :N_EXPERTS].astype(jnp.int32)
    padded = ((counts + tile - 1) // tile) * tile
    ends = jnp.cumsum(padded)
    offsets = ends - padded
    tile_start = jnp.arange(n_tiles, dtype=jnp.int32) * tile
    tile_expert = jnp.sum(tile_start[:, None] >= ends[None, :], axis=1).astype(jnp.int32)
    tile_valid = (tile_start < ends[-1]).astype(jnp.int32)
    tile_expert = jnp.minimum(tile_expert, N_EXPERTS - 1)
    ids = meta[:, 0:TOP_K].astype(jnp.int32)
    rank = meta[:, 4:4 + TOP_K].astype(jnp.int32)
    onehot = ids[..., None] == jnp.arange(N_EXPERTS, dtype=jnp.int32)
    dest = (rank + jnp.sum(jnp.where(onehot, offsets, 0), axis=-1)).reshape(-1)
    used_tiles = (ends[-1:] // tile).astype(jnp.int32)
    xs = _dispatch(dest, offsets + counts, padded - counts, used_tiles, hn, n_slots)
    ys = _experts(tile_expert, tile_valid, xs, w_gate, w_up, w_down, layer)
    return _combine(dest, x1, meta, ys)


def _router_weights(w_group, b_group, w_expert, b_expert):
    pad = LANES - N_EXPERTS - N_GROUPS
    w = jnp.concatenate([w_expert, w_group, jnp.zeros((D_MODEL, pad), F32)], axis=-1)
    b = jnp.concatenate([b_expert, b_group, jnp.zeros((pad,), F32)]).reshape(1, LANES)
    w_hi = w.astype(BF16)
    w_lo = (w - w_hi.astype(F32)).astype(BF16)
    return w_hi, w_lo, b


def kernel(x, attn_norm, ffn_norm, diff_w_in, diff_q_norm, diff_k_norm, diff_lambda_q1, diff_lambda_k1, diff_lambda_q2, diff_lambda_k2, diff_subln, diff_w_out, mla_w_a, mla_q_a_norm, mla_kv_a_norm, mla_w_qb, mla_w_kvb, mla_q_norm, mla_k_norm, mla_w_out, moe_w_group, moe_b_group, moe_w_expert, moe_b_expert, moe_w_gate, moe_w_up, moe_w_down):
    b, s, d = x.shape
    assert d == D_MODEL and s % Q_TILE == 0 and s % ROW_TILE == 0
    t = b * s
    rope_cos, rope_sin = _rope_tables(s)
    row = lambda v: v.astype(F32).reshape(1, -1)
    x2d = x.reshape(t, d)

    lambda_init = 0.8 - 0.6 * math.exp(-0.3 * 0)
    w_in = diff_w_in[0]
    w_in = jnp.concatenate([_diff_head_layout(w_in[:, :DIFF_QK_WIDTH]),
                            _diff_head_layout(w_in[:, DIFF_QK_WIDTH:2 * DIFF_QK_WIDTH]),
                            w_in[:, 2 * DIFF_QK_WIDTH:]], axis=-1).astype(BF16)
    q, k, vt = _diff_qkv(x2d, row(attn_norm[0]), w_in,
                         _diff_gain_layout(diff_q_norm[0], DIFF_HEAD_DIM ** -0.5 * LOG2E),
                         _diff_gain_layout(diff_k_norm[0], 1.0),
                         rope_cos, rope_sin, _segment_matrix(), s)
    shp = (b, s, DIFF_QK_WIDTH)
    o = _flash(q.reshape(shp), k.reshape(shp), vt, DIFF_HEADS, LANES, Q_TILE,
               diff_params=(row(diff_lambda_q1[0]), row(diff_lambda_k1[0]),
                            row(diff_lambda_q2[0]), row(diff_lambda_k2[0]),
                            row(diff_subln[0])),
               lambda_init=lambda_init)
    wr_hi, wr_lo, b_r = _router_weights(moe_w_group[0], moe_b_group[0],
                                        moe_w_expert[0], moe_b_expert[0])
    x1, hn, meta, cnt = _post_attn(o.reshape(t, d), x2d, diff_w_out[0].astype(BF16),
                                   row(ffn_norm[0]), wr_hi, wr_lo, b_r)
    x2d = _moe(x1, hn, meta, cnt, moe_w_gate, moe_w_up, moe_w_down, 0)

    w_a = mla_w_a[0]
    split = MLA_Q_LORA + MLA_KV_LORA
    w_a = jnp.concatenate([w_a[:, :split], _pad_rope(w_a[:, split:])], axis=-1).astype(BF16)
    w_qb = _mla_qk_layout(mla_w_qb[0].reshape(MLA_Q_LORA, MLA_HEADS, MLA_QK_DIM))
    w_qb = w_qb.reshape(MLA_Q_LORA, MLA_HEADS * MLA_QK_PAD).astype(BF16)
    w_kvb = mla_w_kvb[0].reshape(MLA_KV_LORA, MLA_HEADS, 2 * MLA_NOPE)
    w_kvb = jnp.concatenate([w_kvb[..., :MLA_NOPE].reshape(MLA_KV_LORA, -1),
                             w_kvb[..., MLA_NOPE:].reshape(MLA_KV_LORA, -1)], axis=-1).astype(BF16)
    q_gain = _mla_qk_layout(mla_q_norm[0].astype(F32) * (MLA_QK_DIM ** -0.5 * LOG2E)).reshape(1, -1)
    k_gain = _mla_qk_layout(mla_k_norm[0].astype(F32)).reshape(1, -1)
    q, k, vt = _mla_proj(x2d, row(attn_norm[1]), w_a, row(mla_q_a_norm[0]), row(mla_kv_a_norm[0]),
                        w_qb, w_kvb, q_gain, k_gain, rope_cos, rope_sin, s)
    qk_shp = (b, s, MLA_HEADS * MLA_QK_PAD)
    o = _flash(q.reshape(qk_shp), k.reshape(qk_shp), vt, MLA_HEADS, MLA_QK_PAD, Q_TILE)
    wr_hi, wr_lo, b_r = _router_weights(moe_w_group[1], moe_b_group[1],
                                        moe_w_expert[1], moe_b_expert[1])
    x1, hn, meta, cnt = _post_attn(o.reshape(t, d), x2d, mla_w_out[0].astype(BF16),
                                   row(ffn_norm[1]), wr_hi, wr_lo, b_r)
    x2d = _moe(x1, hn, meta, cnt, moe_w_gate, moe_w_up, moe_w_down, 1)
    return x2d.reshape(b, s, d)
```

```python
import functools
import math

import jax
import jax.numpy as jnp
from jax import lax
from jax.experimental import pallas as pl
from jax.experimental.pallas import tpu as pltpu

F32 = jnp.float32
BF16 = jnp.bfloat16

D_MODEL = 1024
PACKED = D_MODEL // 2
ROPE_THETA = 10000.0
NORM_EPS = 1e-6
LANES = 128
MXU_DIM = 256

DIFF_HEAD_DIM = 64
DIFF_HEADS = 8
DIFF_QK_WIDTH = 1024

MLA_NOPE = 128
MLA_ROPE = 64
MLA_QK_DIM = 192
MLA_QK_PAD = 256
MLA_HEADS = 8
MLA_Q_LORA = 384
MLA_KV_LORA = 256

N_GROUPS = 8
EXPERTS_PER_GROUP = 8
N_EXPERTS = 64
TOP_K = 2
EXPERT_FF = 256

ROW_TILE = 512
Q_TILE = 1024
FLASH_CHUNK = 512
V_ROWS = 144
LOG2E = math.log2(math.e)
EXPERT_TILE = 256
SUBLANES = 8
VMEM_LIMIT = 56 * 1024 * 1024


def _cparams(sem):
    return pltpu.CompilerParams(dimension_semantics=sem, vmem_limit_bytes=VMEM_LIMIT)


def _rms(x, gain):
    return x * lax.rsqrt(jnp.mean(x * x, axis=-1, keepdims=True) + NORM_EPS) * gain


def _pack_bf16_pairs(x):
    lo = lax.bitcast_convert_type(x[:, :PACKED].astype(BF16).astype(F32), jnp.uint32)
    hi = lax.bitcast_convert_type(x[:, PACKED:].astype(BF16).astype(F32), jnp.uint32)
    return (lo >> 16) | (hi & jnp.uint32(0xFFFF0000))


def _unpack_bf16_pairs(p):
    lo = lax.bitcast_convert_type(p << 16, F32)
    hi = lax.bitcast_convert_type(p & jnp.uint32(0xFFFF0000), F32)
    return lo, hi


def _store_v_transposed(v, vt_ref, heads):
    tm = v.shape[0]
    ones = jnp.ones((V_ROWS - LANES, tm), BF16)
    for hd in range(heads):
        lo = hd * V_ROWS
        vt_ref[0, 0, lo:lo + LANES, :] = v[:, hd * LANES:(hd + 1) * LANES].T.astype(BF16)
        vt_ref[0, 0, lo + LANES:lo + V_ROWS, :] = ones


def _rope128(u, cos, sin):
    return u * cos + pltpu.roll(u, 64, 1) * sin


def _diff_qkv_kernel(x_ref, g_ref, w_ref, qg_ref, kg_ref, cos_ref, sin_ref, seg_ref,
                     q_ref, k_ref, vt_ref):
    h = _rms(x_ref[...], g_ref[...])
    qkv = jnp.dot(h.astype(BF16), w_ref[...], preferred_element_type=F32)
    cos = cos_ref[...]
    sin = sin_ref[...]
    seg = seg_ref[...]
    for off, gain_ref, out_ref in ((0, qg_ref, q_ref), (DIFF_QK_WIDTH, kg_ref, k_ref)):
        for c in range(DIFF_QK_WIDTH // MXU_DIM):
            t = qkv[:, off + c * MXU_DIM: off + (c + 1) * MXU_DIM]
            ss = jnp.dot((t * t).astype(BF16), seg, preferred_element_type=F32)
            tn = t * lax.rsqrt(ss * (1.0 / DIFF_HEAD_DIM) + NORM_EPS) * gain_ref[...]
            for half in range(MXU_DIM // LANES):
                u = tn[:, half * LANES:(half + 1) * LANES]
                lo = c * MXU_DIM + half * LANES
                out_ref[:, lo:lo + LANES] = _rope128(u, cos, sin).astype(BF16)
    _store_v_transposed(qkv[:, 2 * DIFF_QK_WIDTH:], vt_ref, DIFF_HEADS)


def _diff_qkv(x2d, gain, w_in, q_gain, k_gain, cos, sin, seg, seq):
    t = x2d.shape[0]
    tm = ROW_TILE
    pos_tiles = seq // tm
    row = lambda i: (i, 0)
    const = lambda i: (0, 0)
    out = jax.ShapeDtypeStruct((t, DIFF_QK_WIDTH), BF16)
    vt_out = jax.ShapeDtypeStruct((t // seq, pos_tiles, DIFF_HEADS * V_ROWS, tm), BF16)
    vt_spec = pl.BlockSpec((1, 1, DIFF_HEADS * V_ROWS, tm),
                           lambda i: (i // pos_tiles, i % pos_tiles, 0, 0))
    return pl.pallas_call(
        _diff_qkv_kernel,
        grid=(t // tm,),
        in_specs=[
            pl.BlockSpec((tm, D_MODEL), row),
            pl.BlockSpec((1, D_MODEL), const),
            pl.BlockSpec(w_in.shape, const),
            pl.BlockSpec((1, MXU_DIM), const),
            pl.BlockSpec((1, MXU_DIM), const),
            pl.BlockSpec((tm, LANES), lambda i: (i % pos_tiles, 0)),
            pl.BlockSpec((tm, LANES), lambda i: (i % pos_tiles, 0)),
            pl.BlockSpec((MXU_DIM, MXU_DIM), const),
        ],
        out_specs=[pl.BlockSpec((tm, DIFF_QK_WIDTH), row)] * 2 + [vt_spec],
        out_shape=[out, out, vt_out],
        compiler_params=_cparams(("parallel",)),
        name="diff_qkv",
    )(x2d, gain, w_in, q_gain, k_gain, cos, sin, seg)


def _mla_proj_kernel(x_ref, g_ref, wa_ref, qag_ref, kvag_ref, wqb_ref, wkvb_ref,
                     qg_ref, kg_ref, cos_ref, sin_ref, q_ref, k_ref, vt_ref):
    h = _rms(x_ref[...], g_ref[...])
    a = jnp.dot(h.astype(BF16), wa_ref[...], preferred_element_type=F32)
    cq = _rms(a[:, :MLA_Q_LORA], qag_ref[...])
    ckv = _rms(a[:, MLA_Q_LORA:MLA_Q_LORA + MLA_KV_LORA], kvag_ref[...])
    kpe = a[:, MLA_Q_LORA + MLA_KV_LORA:]
    q = jnp.dot(cq.astype(BF16), wqb_ref[...], preferred_element_type=F32)
    kv = jnp.dot(ckv.astype(BF16), wkvb_ref[...], preferred_element_type=F32)
    cos = cos_ref[...]
    sin = sin_ref[...]
    qg = qg_ref[...]
    kg = kg_ref[...]
    kpe_ss = jnp.sum(kpe * kpe, axis=-1, keepdims=True)
    kpe_rot = _rope128(kpe * kg[:, MLA_NOPE:], cos, sin)
    inv_d = 1.0 / MLA_QK_DIM
    for hd in range(MLA_HEADS):
        lo = hd * MLA_QK_PAD
        qh = q[:, lo:lo + MLA_QK_PAD]
        rq = lax.rsqrt(jnp.sum(qh * qh, axis=-1, keepdims=True) * inv_d + NORM_EPS)
        q_ref[:, lo:lo + MLA_NOPE] = (qh[:, :MLA_NOPE] * rq * qg[:, :MLA_NOPE]).astype(BF16)
        q_ref[:, lo + MLA_NOPE:lo + MLA_QK_PAD] = _rope128(
            qh[:, MLA_NOPE:] * rq * qg[:, MLA_NOPE:], cos, sin).astype(BF16)
        kn = kv[:, hd * MLA_NOPE:(hd + 1) * MLA_NOPE]
        rk = lax.rsqrt((jnp.sum(kn * kn, axis=-1, keepdims=True) + kpe_ss) * inv_d + NORM_EPS)
        k_ref[:, lo:lo + MLA_NOPE] = (kn * rk * kg[:, :MLA_NOPE]).astype(BF16)
        k_ref[:, lo + MLA_NOPE:lo + MLA_QK_PAD] = (kpe_rot * rk).astype(BF16)
    _store_v_transposed(kv[:, MLA_HEADS * MLA_NOPE:], vt_ref, MLA_HEADS)


def _mla_proj(x2d, gain, w_a, qa_gain, kva_gain, w_qb, w_kvb, q_gain, k_gain, cos, sin, seq):
    t = x2d.shape[0]
    tm = ROW_TILE
    pos_tiles = seq // tm
    row = lambda i: (i, 0)
    const = lambda i: (0, 0)
    qk_out = jax.ShapeDtypeStruct((t, MLA_HEADS * MLA_QK_PAD), BF16)
    vt_out = jax.ShapeDtypeStruct((t // seq, pos_tiles, MLA_HEADS * V_ROWS, tm), BF16)
    return pl.pallas_call(
        _mla_proj_kernel,
        grid=(t // tm,),
        in_specs=[
            pl.BlockSpec((tm, D_MODEL), row),
            pl.BlockSpec((1, D_MODEL), const),
            pl.BlockSpec(w_a.shape, const),
            pl.BlockSpec((1, MLA_Q_LORA), const),
            pl.BlockSpec((1, MLA_KV_LORA), const),
            pl.BlockSpec(w_qb.shape, const),
            pl.BlockSpec(w_kvb.shape, const),
            pl.BlockSpec((1, MLA_QK_PAD), const),
            pl.BlockSpec((1, MLA_QK_PAD), const),
            pl.BlockSpec((tm, LANES), lambda i: (i % pos_tiles, 0)),
            pl.BlockSpec((tm, LANES), lambda i: (i % pos_tiles, 0)),
        ],
        out_specs=[pl.BlockSpec((tm, MLA_HEADS * MLA_QK_PAD), row),
                   pl.BlockSpec((tm, MLA_HEADS * MLA_QK_PAD), row),
                   pl.BlockSpec((1, 1, MLA_HEADS * V_ROWS, tm),
                                lambda i: (i // pos_tiles, i % pos_tiles, 0, 0))],
        out_shape=[qk_out, qk_out, vt_out],
        compiler_params=_cparams(("parallel",)),
        name="mla_proj",
    )(x2d, gain, w_a, qa_gain, kva_gain, w_qb, w_kvb, q_gain, k_gain, cos, sin)


def _flash_kernel(*refs, diff, lambda_init, tq, tk, n_chunks):
    if diff:
        (lq1_ref, lk1_ref, lq2_ref, lk2_ref, subln_ref, q_ref, k_ref, vt_ref, o_ref,
         qt_ref, m_ref, acc_ref, sa_ref, sb_ref) = refs
    else:
        q_ref, k_ref, vt_ref, o_ref, qt_ref, m_ref, acc_ref, sa_ref, sb_ref = refs
    qi = pl.program_id(2)
    ch = FLASH_CHUNK
    copies = 2 if diff else 1
    chunks_per_copy = n_chunks // copies

    q_t = q_ref[0].astype(F32).T
    if diff:
        feat = lax.broadcasted_iota(jnp.int32, q_t.shape, 0)
        first = ((feat // 32) % 2) == 0
        qt_ref[:, :tq] = jnp.where(first, q_t, 0.0).astype(BF16)
        qt_ref[:, tq:] = jnp.where(first, 0.0, q_t).astype(BF16)
    else:
        qt_ref[...] = q_t.astype(BF16)
    m_ref[...] = jnp.full(m_ref.shape, -jnp.inf, F32)
    acc_ref[...] = jnp.zeros(acc_ref.shape, F32)

    def key_tile(j):
        return k_ref[0, pl.ds(pl.multiple_of(j * tk, tk), tk), :]

    def scores(j, s_ref, chunks):
        k = key_tile(j)
        for c in chunks:
            s_ref[c] = jnp.dot(k, qt_ref[:, c * ch:(c + 1) * ch], preferred_element_type=F32)

    def accumulate(j, s_ref, chunks, masked=()):
        vt = vt_ref[0, j]
        for c in chunks:
            s = s_ref[c]
            if c in masked:
                key = lax.broadcasted_iota(jnp.int32, (tk, ch), 0)
                qry = lax.broadcasted_iota(jnp.int32, (tk, ch), 1)
                s = jnp.where(key <= qry, s, -jnp.inf)
            m_prev = m_ref[c]
            m_new = jnp.maximum(m_prev, jnp.max(s, axis=0, keepdims=True))
            alpha = jnp.exp2(m_prev - m_new)
            p = jnp.exp2(s - m_new).astype(BF16)
            acc_ref[c] = alpha * acc_ref[c] + jnp.dot(vt, p, preferred_element_type=F32)
            m_ref[c] = m_new

    every = tuple(range(n_chunks))
    early = tuple(c for c in every if c % chunks_per_copy == 0)
    late = tuple(c for c in every if c % chunks_per_copy == 1)
    scores(0, sa_ref, every)

    def pair(j):
        scores(j + 1, sb_ref, every)
        accumulate(j, sa_ref, every)
        scores(j + 2, sa_ref, every)
        accumulate(j + 1, sb_ref, every)

    def two_pairs(i, carry):
        pair(4 * i)
        pair(4 * i + 2)
        return carry

    lax.fori_loop(0, qi // 2, two_pairs, 0)

    @pl.when(qi % 2 == 1)
    def _():
        pair(2 * (qi - 1))

    j = 2 * qi
    scores(j + 1, sb_ref, late)
    accumulate(j, sa_ref, every, masked=early)
    accumulate(j + 1, sb_ref, late, masked=late)

    def normalized(c):
        a = acc_ref[c]
        return a[:LANES] * (1.0 / a[LANES:LANES + 1])

    if diff:
        lam = (jnp.exp(jnp.sum(lq1_ref[...] * lk1_ref[...], axis=-1, keepdims=True))
               - jnp.exp(jnp.sum(lq2_ref[...] * lk2_ref[...], axis=-1, keepdims=True))
               + lambda_init)
        for c in range(chunks_per_copy):
            o_t = normalized(c) - lam * normalized(c + chunks_per_copy)
            o_t = o_t * lax.rsqrt(jnp.mean(o_t * o_t, axis=0, keepdims=True) + NORM_EPS)
            o = o_t.T * (subln_ref[...] * (1.0 - lambda_init))
            o_ref[0, c * ch:(c + 1) * ch, :] = o.astype(o_ref.dtype)
    else:
        for c in range(n_chunks):
            o_ref[0, c * ch:(c + 1) * ch, :] = normalized(c).T.astype(o_ref.dtype)


def _flash(q, k, vt, heads, dk, tq, diff_params=None, lambda_init=0.0):
    b, s, _ = q.shape
    tk = ROW_TILE
    diff = diff_params is not None
    n_chunks = (2 if diff else 1) * tq // FLASH_CHUNK
    q_spec = pl.BlockSpec((1, tq, dk), lambda bi, hi, qi: (bi, qi, hi))
    k_spec = pl.BlockSpec((1, s, dk), lambda bi, hi, qi: (bi, 0, hi))
    v_spec = pl.BlockSpec((1, s // tk, V_ROWS, tk), lambda bi, hi, qi: (bi, 0, hi, 0))
    o_spec = pl.BlockSpec((1, tq, LANES), lambda bi, hi, qi: (bi, qi, hi))
    small = lambda shape: pl.BlockSpec(shape, lambda bi, hi, qi: (0, 0))
    in_specs = [q_spec, k_spec, v_spec]
    args = [q, k, vt]
    if diff:
        in_specs = [small((1, DIFF_HEAD_DIM))] * 4 + [small((1, LANES))] + in_specs
        args = list(diff_params) + args
    assert tk == FLASH_CHUNK and tq == 2 * tk
    scratch = [pltpu.VMEM((dk, n_chunks * FLASH_CHUNK), BF16),
               pltpu.VMEM((n_chunks, 1, FLASH_CHUNK), F32),
               pltpu.VMEM((n_chunks, V_ROWS, FLASH_CHUNK), F32),
               pltpu.VMEM((n_chunks, tk, FLASH_CHUNK), F32),
               pltpu.VMEM((n_chunks, tk, FLASH_CHUNK), F32)]
    return pl.pallas_call(
        functools.partial(_flash_kernel, diff=diff, lambda_init=lambda_init, tq=tq, tk=tk,
                          n_chunks=n_chunks),
        grid=(b, heads, s // tq),
        in_specs=in_specs,
        out_specs=o_spec,
        out_shape=jax.ShapeDtypeStruct((b, s, heads * LANES), BF16),
        scratch_shapes=scratch,
        compiler_params=_cparams(("parallel", "parallel", "arbitrary")),
        name="diff_flash" if diff else "mla_flash",
    )(*args)


def _post_attn_kernel(o_ref, x_ref, wo_ref, fg_ref, wrh_ref, wrl_ref, br_ref,
                      x1_ref, hn_ref, meta_ref, cnt_ref, carry_ref):
    i = pl.program_id(0)

    @pl.when(i == 0)
    def _():
        carry_ref[...] = jnp.zeros(carry_ref.shape, F32)

    x1 = x_ref[...] + jnp.dot(o_ref[...], wo_ref[...], preferred_element_type=F32)
    x1_ref[...] = x1
    hn = _rms(x1, fg_ref[...])
    hn_ref[...] = _pack_bf16_pairs(hn)

    hi = hn.astype(BF16)
    lo = (hn - hi.astype(F32)).astype(BF16)
    wrh = wrh_ref[...]
    logits = (jnp.dot(hi, wrh, preferred_element_type=F32)
              + jnp.dot(lo, wrh, preferred_element_type=F32)
              + jnp.dot(hi, wrl_ref[...], preferred_element_type=F32)
              + br_ref[...])
    tm = logits.shape[0]
    lane = lax.broadcasted_iota(jnp.int32, logits.shape, 1)
    lane_f = lane.astype(F32)
    neg = jnp.float32(-jnp.inf)
    big = jnp.float32(1e9)

    is_group = (lane >= N_EXPERTS) & (lane < N_EXPERTS + N_GROUPS)
    gl = jnp.where(is_group, logits, neg)
    gmax = jnp.max(gl, axis=-1, keepdims=True)
    gidx = jnp.min(jnp.where(gl == gmax, lane_f, big), axis=-1, keepdims=True) - N_EXPERTS
    g_w = 1.0 / jnp.sum(jnp.exp(gl - gmax), axis=-1, keepdims=True)

    in_group = (lane < N_EXPERTS) & ((lane // EXPERTS_PER_GROUP).astype(F32) == gidx)
    el = jnp.where(in_group, logits, neg)
    m1 = jnp.max(el, axis=-1, keepdims=True)
    i1 = jnp.min(jnp.where(el == m1, lane_f, big), axis=-1, keepdims=True)
    el2 = jnp.where(lane_f == i1, neg, el)
    m2 = jnp.max(el2, axis=-1, keepdims=True)
    i2 = jnp.min(jnp.where(el2 == m2, lane_f, big), axis=-1, keepdims=True)
    d = jnp.exp(m2 - m1)
    w1 = g_w / (1.0 + d)
    w2 = g_w * d / (1.0 + d)

    oh1 = lane_f == i1
    oh2 = lane_f == i2
    oh = jnp.where(oh1 | oh2, 1.0, 0.0)
    r = lax.broadcasted_iota(jnp.int32, (tm, tm), 0)
    c = lax.broadcasted_iota(jnp.int32, (tm, tm), 1)
    lower = jnp.where(c < r, 1.0, 0.0).astype(BF16)
    prefix = jnp.dot(lower, oh.astype(BF16), preferred_element_type=F32) + carry_ref[...]
    rank1 = jnp.sum(jnp.where(oh1, prefix, 0.0), axis=-1, keepdims=True)
    rank2 = jnp.sum(jnp.where(oh2, prefix, 0.0), axis=-1, keepdims=True)
    carry = carry_ref[...] + jnp.sum(oh, axis=0, keepdims=True)
    carry_ref[...] = carry
    cnt_ref[...] = jnp.broadcast_to(carry, cnt_ref.shape)

    meta = jnp.zeros(logits.shape, F32)
    for ln, val in enumerate((i1, i2, w1, w2, rank1, rank2)):
        meta = jnp.where(lane == ln, val, meta)
    meta_ref[...] = meta


def _post_attn(o2d, x2d, w_out, ffn_gain, wr_hi, wr_lo, b_r):
    t = x2d.shape[0]
    tm = ROW_TILE
    row = lambda i: (i, 0)
    const = lambda i: (0, 0)
    return pl.pallas_call(
        _post_attn_kernel,
        grid=(t // tm,),
        in_specs=[
            pl.BlockSpec((tm, D_MODEL), row),
            pl.BlockSpec((tm, D_MODEL), row),
            pl.BlockSpec((D_MODEL, D_MODEL), const),
            pl.BlockSpec((1, D_MODEL), const),
            pl.BlockSpec((D_MODEL, LANES), const),
            pl.BlockSpec((D_MODEL, LANES), const),
            pl.BlockSpec((1, LANES), const),
        ],
        out_specs=[pl.BlockSpec((tm, D_MODEL), row),
                   pl.BlockSpec((tm, PACKED), row),
                   pl.BlockSpec((tm, LANES), row),
                   pl.BlockSpec((8, LANES), const)],
        out_shape=[jax.ShapeDtypeStruct((t, D_MODEL), F32),
                   jax.ShapeDtypeStruct((t, PACKED), jnp.uint32),
                   jax.ShapeDtypeStruct((t, LANES), F32),
                   jax.ShapeDtypeStruct((8, LANES), F32)],
        scratch_shapes=[pltpu.VMEM((1, LANES), F32)],
        compiler_params=_cparams(("arbitrary",)),
        name="post_attn_router",
    )(o2d, x2d, w_out, ffn_gain, wr_hi, wr_lo, b_r)


def _row_copy(src, dst, sem):
    return pltpu.make_async_copy(src, dst, sem)


def _dispatch_kernel(dest_ref, zero_tile_ref, hn_ref, xs_ref, zeros_ref, sem, pad_sem):
    tm = hn_ref.shape[0]
    tile = zeros_ref.shape[0]
    step = pl.program_id(0)
    base = step * (TOP_K * tm)

    @pl.when(step == 0)
    def _():
        zeros_ref[...] = jnp.zeros(zeros_ref.shape, zeros_ref.dtype)

        def tile_copy(tl):
            row0 = pl.multiple_of(tl * tile, tile)
            return _row_copy(zeros_ref, xs_ref.at[pl.ds(row0, tile)], pad_sem)

        def start_tile(tl, carry):
            @pl.when(zero_tile_ref[tl] == 1)
            def _():
                tile_copy(tl).start()
            return carry

        def wait_tile(tl, carry):
            @pl.when(zero_tile_ref[tl] == 1)
            def _():
                tile_copy(tl).wait()
            return carry

        n_tiles = xs_ref.shape[0] // tile
        lax.fori_loop(0, n_tiles, start_tile, 0)
        lax.fori_loop(0, n_tiles, wait_tile, 0)

    def issue(g, carry):
        t0 = pl.multiple_of(g * SUBLANES, SUBLANES)
        for u in range(SUBLANES):
            for kk in range(TOP_K):
                d = dest_ref[base + TOP_K * (t0 + u) + kk]
                _row_copy(hn_ref.at[pl.ds(t0 + u, 1)], xs_ref.at[pl.ds(d, 1)], sem).start()
        return carry

    lax.fori_loop(0, tm // SUBLANES, issue, 0)
    for kk in range(TOP_K):
        _row_copy(hn_ref, xs_ref.at[pl.ds(0, tm)], sem).wait()


def _dispatch(dest, zero_tile, hn, n_slots):
    t = hn.shape[0]
    tm = ROW_TILE
    grid_spec = pltpu.PrefetchScalarGridSpec(
        num_scalar_prefetch=2,
        grid=(t // tm,),
        in_specs=[pl.BlockSpec((tm, PACKED), lambda i, *_: (i, 0))],
        out_specs=pl.BlockSpec(memory_space=pl.ANY),
        scratch_shapes=[pltpu.VMEM((EXPERT_TILE, PACKED), jnp.uint32),
                        pltpu.SemaphoreType.DMA(()),
                        pltpu.SemaphoreType.DMA(())],
    )
    return pl.pallas_call(
        _dispatch_kernel,
        grid_spec=grid_spec,
        out_shape=jax.ShapeDtypeStruct((n_slots, PACKED), jnp.uint32),
        compiler_params=_cparams(("arbitrary",)),
        name="moe_dispatch",
    )(dest, zero_tile, hn)


def _expert_kernel(te_ref, tv_ref, xs_ref, wg_ref, wu_ref, wd_ref, ys_ref,
                   wg_bf_ref, wu_bf_ref, wd_bf_ref):
    i = pl.program_id(0)

    @pl.when((i == 0) | (te_ref[i] != te_ref[jnp.maximum(i - 1, 0)]))
    def _():
        wg_bf_ref[...] = wg_ref[0, 0].astype(BF16)
        wu_bf_ref[...] = wu_ref[0, 0].astype(BF16)
        wd_bf_ref[...] = wd_ref[0, 0].astype(BF16)

    @pl.when(tv_ref[i] == 1)
    def _():
        x_lo, x_hi = _unpack_bf16_pairs(xs_ref[...])
        x = jnp.concatenate([x_lo.astype(BF16), x_hi.astype(BF16)], axis=-1)
        g = jnp.dot(x, wg_bf_ref[...], preferred_element_type=F32)
        u = jnp.dot(x, wu_bf_ref[...], preferred_element_type=F32)
        hid = g * jax.nn.sigmoid(g) * u
        y = jnp.dot(hid.astype(BF16), wd_bf_ref[...], preferred_element_type=F32)
        ys_ref[...] = _pack_bf16_pairs(y)

    @pl.when(tv_ref[i] == 0)
    def _():
        ys_ref[...] = jnp.zeros(ys_ref.shape, ys_ref.dtype)


def _experts(tile_expert, tile_valid, xs, w_gate, w_up, w_down, layer):
    p = xs.shape[0]
    tile = EXPERT_TILE
    grid_spec = pltpu.PrefetchScalarGridSpec(
        num_scalar_prefetch=2,
        grid=(p // tile,),
        in_specs=[
            pl.BlockSpec((tile, PACKED), lambda i, te, tv: (i, 0)),
            pl.BlockSpec((1, 1, D_MODEL, EXPERT_FF), lambda i, te, tv: (layer, te[i], 0, 0)),
            pl.BlockSpec((1, 1, D_MODEL, EXPERT_FF), lambda i, te, tv: (layer, te[i], 0, 0)),
            pl.BlockSpec((1, 1, EXPERT_FF, D_MODEL), lambda i, te, tv: (layer, te[i], 0, 0)),
        ],
        out_specs=pl.BlockSpec((tile, PACKED), lambda i, te, tv: (i, 0)),
        scratch_shapes=[pltpu.VMEM((D_MODEL, EXPERT_FF), BF16),
                        pltpu.VMEM((D_MODEL, EXPERT_FF), BF16),
                        pltpu.VMEM((EXPERT_FF, D_MODEL), BF16)],
    )
    return pl.pallas_call(
        _expert_kernel,
        grid_spec=grid_spec,
        out_shape=jax.ShapeDtypeStruct((p, PACKED), jnp.uint32),
        compiler_params=_cparams(("arbitrary",)),
        name="moe_experts",
    )(tile_expert, tile_valid, xs, w_gate, w_up, w_down)


def _combine_kernel(dest_ref, x1_ref, meta_ref, ys_ref, out_ref, buf_ref, sem):
    tm = x1_ref.shape[0]
    base = pl.program_id(0) * (TOP_K * tm)

    def issue(g, carry):
        t0 = pl.multiple_of(g * SUBLANES, SUBLANES)
        for u in range(SUBLANES):
            for kk in range(TOP_K):
                d = dest_ref[base + TOP_K * (t0 + u) + kk]
                _row_copy(ys_ref.at[pl.ds(d, 1)], buf_ref.at[kk, pl.ds(t0 + u, 1)], sem).start()
        return carry

    lax.fori_loop(0, tm // SUBLANES, issue, 0)
    for kk in range(TOP_K):
        _row_copy(ys_ref.at[pl.ds(0, tm)], buf_ref.at[kk], sem).wait()

    meta = meta_ref[...]
    w1 = meta[:, 2:3]
    w2 = meta[:, 3:4]
    lo1, hi1 = _unpack_bf16_pairs(buf_ref[0])
    lo2, hi2 = _unpack_bf16_pairs(buf_ref[1])
    out_ref[:, :PACKED] = x1_ref[:, :PACKED] + w1 * lo1 + w2 * lo2
    out_ref[:, PACKED:] = x1_ref[:, PACKED:] + w1 * hi1 + w2 * hi2


def _combine(dest, x1, meta, ys):
    t = x1.shape[0]
    tm = ROW_TILE
    grid_spec = pltpu.PrefetchScalarGridSpec(
        num_scalar_prefetch=1,
        grid=(t // tm,),
        in_specs=[pl.BlockSpec((tm, D_MODEL), lambda i, dest: (i, 0)),
                  pl.BlockSpec((tm, LANES), lambda i, dest: (i, 0)),
                  pl.BlockSpec(memory_space=pl.ANY)],
        out_specs=pl.BlockSpec((tm, D_MODEL), lambda i, dest: (i, 0)),
        scratch_shapes=[pltpu.VMEM((TOP_K, tm, PACKED), jnp.uint32),
                        pltpu.SemaphoreType.DMA(())],
    )
    return pl.pallas_call(
        _combine_kernel,
        grid_spec=grid_spec,
        out_shape=jax.ShapeDtypeStruct((t, D_MODEL), F32),
        compiler_params=_cparams(("arbitrary",)),
        name="moe_combine",
    )(dest, x1, meta, ys)


def _rope_tables(seq):
    inv_freq = 1.0 / (ROPE_THETA ** (jnp.arange(0, DIFF_HEAD_DIM, 2, dtype=F32) / DIFF_HEAD_DIM))
    ang = jnp.arange(seq, dtype=F32)[:, None] * jnp.tile(inv_freq, 4)[None, :]
    sign = jnp.where(jnp.arange(LANES) < LANES // 2, -1.0, 1.0).astype(F32)
    return jnp.cos(ang), jnp.sin(ang) * sign[None, :]


def _diff_head_layout(w):
    lead = w.shape[:-1]
    w = w.reshape(*lead, DIFF_HEADS, 2, 2, DIFF_HEAD_DIM // 2)
    w = jnp.swapaxes(w, -3, -2)
    return w.reshape(*lead, DIFF_QK_WIDTH)


def _diff_gain_layout(g, scale):
    g = (g.astype(F32) * scale).reshape(2, DIFF_HEAD_DIM // 2)
    g = jnp.broadcast_to(g[:, None, :], (2, 2, DIFF_HEAD_DIM // 2))
    return jnp.tile(g.reshape(1, LANES), (1, MXU_DIM // LANES))


def _pad_rope(w):
    half = MLA_ROPE // 2
    z = jnp.zeros(w.shape[:-1] + (half,), w.dtype)
    return jnp.concatenate([w[..., :half], z, w[..., half:], z], axis=-1)


def _mla_qk_layout(w):
    return jnp.concatenate([w[..., :MLA_NOPE], _pad_rope(w[..., MLA_NOPE:])], axis=-1)


def _segment_matrix():
    lane = jnp.arange(MXU_DIM)
    key = (lane // LANES) * 2 + (lane // 32) % 2
    return (key[:, None] == key[None, :]).astype(BF16)


def _moe(x1, hn, meta, cnt, w_gate, w_up, w_down, layer):
    t = x1.shape[0]
    tile = EXPERT_TILE
    n_slots = t * TOP_K + N_EXPERTS * tile
    n_tiles = n_slots // tile
    counts = cnt[0, :N_EXPERTS].astype(jnp.int32)
    padded = ((counts + tile - 1) // tile) * tile
    ends = jnp.cumsum(padded)
    offsets = ends - padded
    tile_start = jnp.arange(n_tiles, dtype=jnp.int32) * tile
    tile_expert = jnp.sum(tile_start[:, None] >= ends[None, :], axis=1).astype(jnp.int32)
    tile_valid = (tile_start < ends[-1]).astype(jnp.int32)
    tile_expert = jnp.minimum(tile_expert, N_EXPERTS - 1)
    ids = meta[:, 0:TOP_K].astype(jnp.int32)
    rank = meta[:, 4:4 + TOP_K].astype(jnp.int32)
    onehot = ids[..., None] == jnp.arange(N_EXPERTS, dtype=jnp.int32)
    dest = (rank + jnp.sum(jnp.where(onehot, offsets, 0), axis=-1)).reshape(-1)
    partly_filled = (tile_start + tile == ends[tile_expert]) & ((padded - counts)[tile_expert] > 0)
    zero_tile = jnp.where(tile_valid == 1, partly_filled, True).astype(jnp.int32)
    xs = _dispatch(dest, zero_tile, hn, n_slots)
    ys = _experts(tile_expert, tile_valid, xs, w_gate, w_up, w_down, layer)
    return _combine(dest, x1, meta, ys)


def _router_weights(w_group, b_group, w_expert, b_expert):
    pad = LANES - N_EXPERTS - N_GROUPS
    w = jnp.concatenate([w_expert, w_group, jnp.zeros((D_MODEL, pad), F32)], axis=-1)
    b = jnp.concatenate([b_expert, b_group, jnp.zeros((pad,), F32)]).reshape(1, LANES)
    w_hi = w.astype(BF16)
    w_lo = (w - w_hi.astype(F32)).astype(BF16)
    return w_hi, w_lo, b


def kernel(x, attn_norm, ffn_norm, diff_w_in, diff_q_norm, diff_k_norm, diff_lambda_q1, diff_lambda_k1, diff_lambda_q2, diff_lambda_k2, diff_subln, diff_w_out, mla_w_a, mla_q_a_norm, mla_kv_a_norm, mla_w_qb, mla_w_kvb, mla_q_norm, mla_k_norm, mla_w_out, moe_w_group, moe_b_group, moe_w_expert, moe_b_expert, moe_w_gate, moe_w_up, moe_w_down):
    b, s, d = x.shape
    assert d == D_MODEL and s % Q_TILE == 0 and s % ROW_TILE == 0
    t = b * s
    rope_cos, rope_sin = _rope_tables(s)
    row = lambda v: v.astype(F32).reshape(1, -1)
    x2d = x.reshape(t, d)

    lambda_init = 0.8 - 0.6 * math.exp(-0.3 * 0)
    w_in = diff_w_in[0]
    w_in = jnp.concatenate([_diff_head_layout(w_in[:, :DIFF_QK_WIDTH]),
                            _diff_head_layout(w_in[:, DIFF_QK_WIDTH:2 * DIFF_QK_WIDTH]),
                            w_in[:, 2 * DIFF_QK_WIDTH:]], axis=-1).astype(BF16)
    q, k, vt = _diff_qkv(x2d, row(attn_norm[0]), w_in,
                         _diff_gain_layout(diff_q_norm[0], DIFF_HEAD_DIM ** -0.5 * LOG2E),
                         _diff_gain_layout(diff_k_norm[0], 1.0),
                         rope_cos, rope_sin, _segment_matrix(), s)
    shp = (b, s, DIFF_QK_WIDTH)
    o = _flash(q.reshape(shp), k.reshape(shp), vt, DIFF_HEADS, LANES, Q_TILE,
               diff_params=(row(diff_lambda_q1[0]), row(diff_lambda_k1[0]),
                            row(diff_lambda_q2[0]), row(diff_lambda_k2[0]),
                            row(diff_subln[0])),
               lambda_init=lambda_init)
    wr_hi, wr_lo, b_r = _router_weights(moe_w_group[0], moe_b_group[0],
                                        moe_w_expert[0], moe_b_expert[0])
    x1, hn, meta, cnt = _post_attn(o.reshape(t, d), x2d, diff_w_out[0].astype(BF16),
                                   row(ffn_norm[0]), wr_hi, wr_lo, b_r)
    x2d = _moe(x1, hn, meta, cnt, moe_w_gate, moe_w_up, moe_w_down, 0)

    w_a = mla_w_a[0]
    split = MLA_Q_LORA + MLA_KV_LORA
    w_a = jnp.concatenate([w_a[:, :split], _pad_rope(w_a[:, split:])], axis=-1).astype(BF16)
    w_qb = _mla_qk_layout(mla_w_qb[0].reshape(MLA_Q_LORA, MLA_HEADS, MLA_QK_DIM))
    w_qb = w_qb.reshape(MLA_Q_LORA, MLA_HEADS * MLA_QK_PAD).astype(BF16)
    w_kvb = mla_w_kvb[0].reshape(MLA_KV_LORA, MLA_HEADS, 2 * MLA_NOPE)
    w_kvb = jnp.concatenate([w_kvb[..., :MLA_NOPE].reshape(MLA_KV_LORA, -1),
                             w_kvb[..., MLA_NOPE:].reshape(MLA_KV_LORA, -1)], axis=-1).astype(BF16)
    q_gain = _mla_qk_layout(mla_q_norm[0].astype(F32) * (MLA_QK_DIM ** -0.5 * LOG2E)).reshape(1, -1)
    k_gain = _mla_qk_layout(mla_k_norm[0].astype(F32)).reshape(1, -1)
    q, k, vt = _mla_proj(x2d, row(attn_norm[1]), w_a, row(mla_q_a_norm[0]), row(mla_kv_a_norm[0]),
                        w_qb, w_kvb, q_gain, k_gain, rope_cos, rope_sin, s)
    qk_shp = (b, s, MLA_HEADS * MLA_QK_PAD)
    o = _flash(q.reshape(qk_shp), k.reshape(qk_shp), vt, MLA_HEADS, MLA_QK_PAD, Q_TILE)
    wr_hi, wr_lo, b_r = _router_weights(moe_w_group[1], moe_b_group[1],
                                        moe_w_expert[1], moe_b_expert[1])
    x1, hn, meta, cnt = _post_attn(o.reshape(t, d), x2d, mla_w_out[0].astype(BF16),
                                   row(ffn_norm[1]), wr_hi, wr_lo, b_r)
    x2d = _moe(x1, hn, meta, cnt, moe_w_gate, moe_w_up, moe_w_down, 1)
    return x2d.reshape(b, s, d)
```

```python
import functools
import math

import jax
import jax.numpy as jnp
from jax import lax
from jax.experimental import pallas as pl
from jax.experimental.pallas import tpu as pltpu

F32 = jnp.float32
BF16 = jnp.bfloat16

D_MODEL = 1024
PACKED = D_MODEL // 2
ROW_WORDS = PACKED + 128
ROPE_THETA = 10000.0
NORM_EPS = 1e-6
LANES = 128
MXU_DIM = 256

DIFF_HEAD_DIM = 64
DIFF_HEADS = 8
DIFF_QK_WIDTH = 1024

MLA_NOPE = 128
MLA_ROPE = 64
MLA_QK_DIM = 192
MLA_QK_PAD = 256
MLA_HEADS = 8
MLA_Q_LORA = 384
MLA_KV_LORA = 256

N_GROUPS = 8
EXPERTS_PER_GROUP = 8
N_EXPERTS = 64
TOP_K = 2
EXPERT_FF = 256

ROW_TILE = 512
Q_TILE = 1024
FLASH_CHUNK = 512
V_ROWS = 144
LOG2E = math.log2(math.e)
EXPERT_TILE = 256
ISSUE_UNROLL = 8
VMEM_LIMIT = 56 * 1024 * 1024


def _cparams(sem):
    return pltpu.CompilerParams(dimension_semantics=sem, vmem_limit_bytes=VMEM_LIMIT)


def _rms(x, gain):
    return x * lax.rsqrt(jnp.mean(x * x, axis=-1, keepdims=True) + NORM_EPS) * gain


def _pack_bf16_pairs(x):
    lo = lax.bitcast_convert_type(x[:, :PACKED].astype(BF16).astype(F32), jnp.uint32)
    hi = lax.bitcast_convert_type(x[:, PACKED:].astype(BF16).astype(F32), jnp.uint32)
    return (lo >> 16) | (hi & jnp.uint32(0xFFFF0000))


def _unpack_bf16_pairs(p):
    lo = lax.bitcast_convert_type(p << 16, F32)
    hi = lax.bitcast_convert_type(p & jnp.uint32(0xFFFF0000), F32)
    return lo, hi


def _store_v_transposed(v, vt_ref, heads):
    tm = v.shape[0]
    ones = jnp.ones((V_ROWS - LANES, tm), BF16)
    for hd in range(heads):
        lo = hd * V_ROWS
        vt_ref[0, 0, lo:lo + LANES, :] = v[:, hd * LANES:(hd + 1) * LANES].T.astype(BF16)
        vt_ref[0, 0, lo + LANES:lo + V_ROWS, :] = ones


def _rope128(u, cos, sin):
    return u * cos + pltpu.roll(u, 64, 1) * sin


def _diff_qkv_kernel(x_ref, g_ref, w_ref, qg_ref, kg_ref, cos_ref, sin_ref, seg_ref,
                     q_ref, k_ref, vt_ref):
    h = _rms(x_ref[...], g_ref[...])
    qkv = jnp.dot(h.astype(BF16), w_ref[...], preferred_element_type=F32)
    cos = cos_ref[...]
    sin = sin_ref[...]
    seg = seg_ref[...]
    for off, gain_ref, out_ref in ((0, qg_ref, q_ref), (DIFF_QK_WIDTH, kg_ref, k_ref)):
        for c in range(DIFF_QK_WIDTH // MXU_DIM):
            t = qkv[:, off + c * MXU_DIM: off + (c + 1) * MXU_DIM]
            ss = jnp.dot((t * t).astype(BF16), seg, preferred_element_type=F32)
            tn = t * lax.rsqrt(ss * (1.0 / DIFF_HEAD_DIM) + NORM_EPS) * gain_ref[...]
            for half in range(MXU_DIM // LANES):
                u = tn[:, half * LANES:(half + 1) * LANES]
                lo = c * MXU_DIM + half * LANES
                out_ref[:, lo:lo + LANES] = _rope128(u, cos, sin).astype(BF16)
    _store_v_transposed(qkv[:, 2 * DIFF_QK_WIDTH:], vt_ref, DIFF_HEADS)


def _diff_qkv(x2d, gain, w_in, q_gain, k_gain, cos, sin, seg, seq):
    t = x2d.shape[0]
    tm = ROW_TILE
    pos_tiles = seq // tm
    row = lambda i: (i, 0)
    const = lambda i: (0, 0)
    out = jax.ShapeDtypeStruct((t, DIFF_QK_WIDTH), BF16)
    vt_out = jax.ShapeDtypeStruct((t // seq, pos_tiles, DIFF_HEADS * V_ROWS, tm), BF16)
    vt_spec = pl.BlockSpec((1, 1, DIFF_HEADS * V_ROWS, tm),
                           lambda i: (i // pos_tiles, i % pos_tiles, 0, 0))
    return pl.pallas_call(
        _diff_qkv_kernel,
        grid=(t // tm,),
        in_specs=[
            pl.BlockSpec((tm, D_MODEL), row),
            pl.BlockSpec((1, D_MODEL), const),
            pl.BlockSpec(w_in.shape, const),
            pl.BlockSpec((1, MXU_DIM), const),
            pl.BlockSpec((1, MXU_DIM), const),
            pl.BlockSpec((tm, LANES), lambda i: (i % pos_tiles, 0)),
            pl.BlockSpec((tm, LANES), lambda i: (i % pos_tiles, 0)),
            pl.BlockSpec((MXU_DIM, MXU_DIM), const),
        ],
        out_specs=[pl.BlockSpec((tm, DIFF_QK_WIDTH), row)] * 2 + [vt_spec],
        out_shape=[out, out, vt_out],
        compiler_params=_cparams(("parallel",)),
        name="diff_qkv",
    )(x2d, gain, w_in, q_gain, k_gain, cos, sin, seg)


def _mla_proj_kernel(x_ref, g_ref, wa_ref, qag_ref, kvag_ref, wqb_ref, wkvb_ref,
                     qg_ref, kg_ref, cos_ref, sin_ref, q_ref, k_ref, vt_ref):
    h = _rms(x_ref[...], g_ref[...])
    a = jnp.dot(h.astype(BF16), wa_ref[...], preferred_element_type=F32)
    cq = _rms(a[:, :MLA_Q_LORA], qag_ref[...])
    ckv = _rms(a[:, MLA_Q_LORA:MLA_Q_LORA + MLA_KV_LORA], kvag_ref[...])
    kpe = a[:, MLA_Q_LORA + MLA_KV_LORA:]
    q = jnp.dot(cq.astype(BF16), wqb_ref[...], preferred_element_type=F32)
    kv = jnp.dot(ckv.astype(BF16), wkvb_ref[...], preferred_element_type=F32)
    cos = cos_ref[...]
    sin = sin_ref[...]
    qg = qg_ref[...]
    kg = kg_ref[...]
    kpe_ss = jnp.sum(kpe * kpe, axis=-1, keepdims=True)
    kpe_rot = _rope128(kpe * kg[:, MLA_NOPE:], cos, sin)
    inv_d = 1.0 / MLA_QK_DIM
    for hd in range(MLA_HEADS):
        lo = hd * MLA_QK_PAD
        qh = q[:, lo:lo + MLA_QK_PAD]
        rq = lax.rsqrt(jnp.sum(qh * qh, axis=-1, keepdims=True) * inv_d + NORM_EPS)
        q_ref[:, lo:lo + MLA_NOPE] = (qh[:, :MLA_NOPE] * rq * qg[:, :MLA_NOPE]).astype(BF16)
        q_ref[:, lo + MLA_NOPE:lo + MLA_QK_PAD] = _rope128(
            qh[:, MLA_NOPE:] * rq * qg[:, MLA_NOPE:], cos, sin).astype(BF16)
        kn = kv[:, hd * MLA_NOPE:(hd + 1) * MLA_NOPE]
        rk = lax.rsqrt((jnp.sum(kn * kn, axis=-1, keepdims=True) + kpe_ss) * inv_d + NORM_EPS)
        k_ref[:, lo:lo + MLA_NOPE] = (kn * rk * kg[:, :MLA_NOPE]).astype(BF16)
        k_ref[:, lo + MLA_NOPE:lo + MLA_QK_PAD] = (kpe_rot * rk).astype(BF16)
    _store_v_transposed(kv[:, MLA_HEADS * MLA_NOPE:], vt_ref, MLA_HEADS)


def _mla_proj(x2d, gain, w_a, qa_gain, kva_gain, w_qb, w_kvb, q_gain, k_gain, cos, sin, seq):
    t = x2d.shape[0]
    tm = ROW_TILE
    pos_tiles = seq // tm
    row = lambda i: (i, 0)
    const = lambda i: (0, 0)
    qk_out = jax.ShapeDtypeStruct((t, MLA_HEADS * MLA_QK_PAD), BF16)
    vt_out = jax.ShapeDtypeStruct((t // seq, pos_tiles, MLA_HEADS * V_ROWS, tm), BF16)
    return pl.pallas_call(
        _mla_proj_kernel,
        grid=(t // tm,),
        in_specs=[
            pl.BlockSpec((tm, D_MODEL), row),
            pl.BlockSpec((1, D_MODEL), const),
            pl.BlockSpec(w_a.shape, const),
            pl.BlockSpec((1, MLA_Q_LORA), const),
            pl.BlockSpec((1, MLA_KV_LORA), const),
            pl.BlockSpec(w_qb.shape, const),
            pl.BlockSpec(w_kvb.shape, const),
            pl.BlockSpec((1, MLA_QK_PAD), const),
            pl.BlockSpec((1, MLA_QK_PAD), const),
            pl.BlockSpec((tm, LANES), lambda i: (i % pos_tiles, 0)),
            pl.BlockSpec((tm, LANES), lambda i: (i % pos_tiles, 0)),
        ],
        out_specs=[pl.BlockSpec((tm, MLA_HEADS * MLA_QK_PAD), row),
                   pl.BlockSpec((tm, MLA_HEADS * MLA_QK_PAD), row),
                   pl.BlockSpec((1, 1, MLA_HEADS * V_ROWS, tm),
                                lambda i: (i // pos_tiles, i % pos_tiles, 0, 0))],
        out_shape=[qk_out, qk_out, vt_out],
        compiler_params=_cparams(("parallel",)),
        name="mla_proj",
    )(x2d, gain, w_a, qa_gain, kva_gain, w_qb, w_kvb, q_gain, k_gain, cos, sin)


def _flash_kernel(*refs, diff, lambda_init, tq, tk, n_chunks):
    if diff:
        (lq1_ref, lk1_ref, lq2_ref, lk2_ref, subln_ref, q_ref, k_ref, vt_ref, o_ref,
         qt_ref, m_ref, acc_ref, sa_ref, sb_ref) = refs
    else:
        q_ref, k_ref, vt_ref, o_ref, qt_ref, m_ref, acc_ref, sa_ref, sb_ref = refs
    qi = pl.program_id(2)
    ch = FLASH_CHUNK
    copies = 2 if diff else 1
    chunks_per_copy = n_chunks // copies

    q_t = q_ref[0].astype(F32).T
    if diff:
        feat = lax.broadcasted_iota(jnp.int32, q_t.shape, 0)
        first = ((feat // 32) % 2) == 0
        qt_ref[:, :tq] = jnp.where(first, q_t, 0.0).astype(BF16)
        qt_ref[:, tq:] = jnp.where(first, 0.0, q_t).astype(BF16)
    else:
        qt_ref[...] = q_t.astype(BF16)
    m_ref[...] = jnp.full(m_ref.shape, -jnp.inf, F32)
    acc_ref[...] = jnp.zeros(acc_ref.shape, F32)

    def key_tile(j):
        return k_ref[0, pl.ds(pl.multiple_of(j * tk, tk), tk), :]

    def scores(j, s_ref, chunks):
        k = key_tile(j)
        for c in chunks:
            s_ref[c] = jnp.dot(k, qt_ref[:, c * ch:(c + 1) * ch], preferred_element_type=F32)

    def accumulate(j, s_ref, chunks, masked=()):
        vt = vt_ref[0, j]
        for c in chunks:
            s = s_ref[c]
            if c in masked:
                key = lax.broadcasted_iota(jnp.int32, (tk, ch), 0)
                qry = lax.broadcasted_iota(jnp.int32, (tk, ch), 1)
                s = jnp.where(key <= qry, s, -jnp.inf)
            m_prev = m_ref[c]
            m_new = jnp.maximum(m_prev, jnp.max(s, axis=0, keepdims=True))
            alpha = jnp.exp2(m_prev - m_new)
            p = jnp.exp2(s - m_new).astype(BF16)
            acc_ref[c] = alpha * acc_ref[c] + jnp.dot(vt, p, preferred_element_type=F32)
            m_ref[c] = m_new

    every = tuple(range(n_chunks))
    early = tuple(c for c in every if c % chunks_per_copy == 0)
    late = tuple(c for c in every if c % chunks_per_copy == 1)
    scores(0, sa_ref, every)

    def pair(j):
        scores(j + 1, sb_ref, every)
        accumulate(j, sa_ref, every)
        scores(j + 2, sa_ref, every)
        accumulate(j + 1, sb_ref, every)

    def two_pairs(i, carry):
        pair(4 * i)
        pair(4 * i + 2)
        return carry

    lax.fori_loop(0, qi // 2, two_pairs, 0)

    @pl.when(qi % 2 == 1)
    def _():
        pair(2 * (qi - 1))

    j = 2 * qi
    scores(j + 1, sb_ref, late)
    accumulate(j, sa_ref, every, masked=early)
    accumulate(j + 1, sb_ref, late, masked=late)

    def normalized(c):
        a = acc_ref[c]
        return a[:LANES] * (1.0 / a[LANES:LANES + 1])

    if diff:
        lam = (jnp.exp(jnp.sum(lq1_ref[...] * lk1_ref[...], axis=-1, keepdims=True))
               - jnp.exp(jnp.sum(lq2_ref[...] * lk2_ref[...], axis=-1, keepdims=True))
               + lambda_init)
        for c in range(chunks_per_copy):
            o_t = normalized(c) - lam * normalized(c + chunks_per_copy)
            o_t = o_t * lax.rsqrt(jnp.mean(o_t * o_t, axis=0, keepdims=True) + NORM_EPS)
            o = o_t.T * (subln_ref[...] * (1.0 - lambda_init))
            o_ref[0, c * ch:(c + 1) * ch, :] = o.astype(o_ref.dtype)
    else:
        for c in range(n_chunks):
            o_ref[0, c * ch:(c + 1) * ch, :] = normalized(c).T.astype(o_ref.dtype)


def _flash(q, k, vt, heads, dk, tq, diff_params=None, lambda_init=0.0):
    b, s, _ = q.shape
    tk = ROW_TILE
    diff = diff_params is not None
    n_chunks = (2 if diff else 1) * tq // FLASH_CHUNK
    q_spec = pl.BlockSpec((1, tq, dk), lambda bi, hi, qi: (bi, qi, hi))
    k_spec = pl.BlockSpec((1, s, dk), lambda bi, hi, qi: (bi, 0, hi))
    v_spec = pl.BlockSpec((1, s // tk, V_ROWS, tk), lambda bi, hi, qi: (bi, 0, hi, 0))
    o_spec = pl.BlockSpec((1, tq, LANES), lambda bi, hi, qi: (bi, qi, hi))
    small = lambda shape: pl.BlockSpec(shape, lambda bi, hi, qi: (0, 0))
    in_specs = [q_spec, k_spec, v_spec]
    args = [q, k, vt]
    if diff:
        in_specs = [small((1, DIFF_HEAD_DIM))] * 4 + [small((1, LANES))] + in_specs
        args = list(diff_params) + args
    assert tk == FLASH_CHUNK and tq == 2 * tk
    scratch = [pltpu.VMEM((dk, n_chunks * FLASH_CHUNK), BF16),
               pltpu.VMEM((n_chunks, 1, FLASH_CHUNK), F32),
               pltpu.VMEM((n_chunks, V_ROWS, FLASH_CHUNK), F32),
               pltpu.VMEM((n_chunks, tk, FLASH_CHUNK), F32),
               pltpu.VMEM((n_chunks, tk, FLASH_CHUNK), F32)]
    return pl.pallas_call(
        functools.partial(_flash_kernel, diff=diff, lambda_init=lambda_init, tq=tq, tk=tk,
                          n_chunks=n_chunks),
        grid=(b, heads, s // tq),
        in_specs=in_specs,
        out_specs=o_spec,
        out_shape=jax.ShapeDtypeStruct((b, s, heads * LANES), BF16),
        scratch_shapes=scratch,
        compiler_params=_cparams(("parallel", "parallel", "arbitrary")),
        name="diff_flash" if diff else "mla_flash",
    )(*args)


def _post_attn_kernel(o_ref, x_ref, wo_ref, fg_ref, wrh_ref, wrl_ref, br_ref,
                      x1_ref, hn_ref, meta_ref, cnt_ref, carry_ref):
    i = pl.program_id(0)

    @pl.when(i == 0)
    def _():
        carry_ref[...] = jnp.zeros(carry_ref.shape, F32)

    x1 = x_ref[...] + jnp.dot(o_ref[...], wo_ref[...], preferred_element_type=F32)
    x1_ref[...] = x1
    hn = _rms(x1, fg_ref[...])
    hn_ref[:, :PACKED] = _pack_bf16_pairs(hn)

    hi = hn.astype(BF16)
    lo = (hn - hi.astype(F32)).astype(BF16)
    wrh = wrh_ref[...]
    logits = (jnp.dot(hi, wrh, preferred_element_type=F32)
              + jnp.dot(lo, wrh, preferred_element_type=F32)
              + jnp.dot(hi, wrl_ref[...], preferred_element_type=F32)
              + br_ref[...])
    tm = logits.shape[0]
    lane = lax.broadcasted_iota(jnp.int32, logits.shape, 1)
    lane_f = lane.astype(F32)
    neg = jnp.float32(-jnp.inf)
    big = jnp.float32(1e9)

    is_group = (lane >= N_EXPERTS) & (lane < N_EXPERTS + N_GROUPS)
    gl = jnp.where(is_group, logits, neg)
    gmax = jnp.max(gl, axis=-1, keepdims=True)
    gidx = jnp.min(jnp.where(gl == gmax, lane_f, big), axis=-1, keepdims=True) - N_EXPERTS
    g_w = 1.0 / jnp.sum(jnp.exp(gl - gmax), axis=-1, keepdims=True)

    in_group = (lane < N_EXPERTS) & ((lane // EXPERTS_PER_GROUP).astype(F32) == gidx)
    el = jnp.where(in_group, logits, neg)
    m1 = jnp.max(el, axis=-1, keepdims=True)
    i1 = jnp.min(jnp.where(el == m1, lane_f, big), axis=-1, keepdims=True)
    el2 = jnp.where(lane_f == i1, neg, el)
    m2 = jnp.max(el2, axis=-1, keepdims=True)
    i2 = jnp.min(jnp.where(el2 == m2, lane_f, big), axis=-1, keepdims=True)
    d = jnp.exp(m2 - m1)
    w1 = g_w / (1.0 + d)
    w2 = g_w * d / (1.0 + d)

    first_expert = gidx * EXPERTS_PER_GROUP
    wv = (jnp.where(lane_f == i1 - first_expert, w1, 0.0)
          + jnp.where(lane_f == i2 - first_expert, w2, 0.0))
    hn_ref[:, PACKED:] = lax.bitcast_convert_type(wv, jnp.uint32)

    in_grp = lane_f == gidx + N_EXPERTS
    oh = jnp.where(in_grp, 1.0, 0.0)
    r = lax.broadcasted_iota(jnp.int32, (tm, tm), 0)
    c = lax.broadcasted_iota(jnp.int32, (tm, tm), 1)
    lower = jnp.where(c < r, 1.0, 0.0).astype(BF16)
    prefix = jnp.dot(lower, oh.astype(BF16), preferred_element_type=F32) + carry_ref[...]
    rank = jnp.sum(jnp.where(in_grp, prefix, 0.0), axis=-1, keepdims=True)
    carry = carry_ref[...] + jnp.sum(oh, axis=0, keepdims=True)
    carry_ref[...] = carry
    cnt_ref[...] = jnp.broadcast_to(carry, cnt_ref.shape)
    meta_ref[...] = jnp.where(lane == 0, gidx, jnp.where(lane == 1, rank, 0.0))


def _post_attn(o2d, x2d, w_out, ffn_gain, wr_hi, wr_lo, b_r):
    t = x2d.shape[0]
    tm = ROW_TILE
    row = lambda i: (i, 0)
    const = lambda i: (0, 0)
    return pl.pallas_call(
        _post_attn_kernel,
        grid=(t // tm,),
        in_specs=[
            pl.BlockSpec((tm, D_MODEL), row),
            pl.BlockSpec((tm, D_MODEL), row),
            pl.BlockSpec((D_MODEL, D_MODEL), const),
            pl.BlockSpec((1, D_MODEL), const),
            pl.BlockSpec((D_MODEL, LANES), const),
            pl.BlockSpec((D_MODEL, LANES), const),
            pl.BlockSpec((1, LANES), const),
        ],
        out_specs=[pl.BlockSpec((tm, D_MODEL), row),
                   pl.BlockSpec((tm, ROW_WORDS), row),
                   pl.BlockSpec((tm, LANES), row),
                   pl.BlockSpec((8, LANES), const)],
        out_shape=[jax.ShapeDtypeStruct((t, D_MODEL), F32),
                   jax.ShapeDtypeStruct((t, ROW_WORDS), jnp.uint32),
                   jax.ShapeDtypeStruct((t, LANES), F32),
                   jax.ShapeDtypeStruct((8, LANES), F32)],
        scratch_shapes=[pltpu.VMEM((1, LANES), F32)],
        compiler_params=_cparams(("arbitrary",)),
        name="post_attn_router",
    )(o2d, x2d, w_out, ffn_gain, wr_hi, wr_lo, b_r)


def _row_copy(src, dst, sem):
    return pltpu.make_async_copy(src, dst, sem)


def _dispatch_kernel(dest_ref, zero_tile_ref, hn_ref, xs_ref, zeros_ref, sem, pad_sem):
    tm = hn_ref.shape[0]
    tile = zeros_ref.shape[0]
    step = pl.program_id(0)
    base = step * tm

    @pl.when(step == 0)
    def _():
        zeros_ref[...] = jnp.zeros(zeros_ref.shape, zeros_ref.dtype)

        def tile_copy(tl):
            row0 = pl.multiple_of(tl * tile, tile)
            return _row_copy(zeros_ref, xs_ref.at[pl.ds(row0, tile)], pad_sem)

        def start_tile(tl, carry):
            @pl.when(zero_tile_ref[tl] == 1)
            def _():
                tile_copy(tl).start()
            return carry

        def wait_tile(tl, carry):
            @pl.when(zero_tile_ref[tl] == 1)
            def _():
                tile_copy(tl).wait()
            return carry

        n_tiles = xs_ref.shape[0] // tile
        lax.fori_loop(0, n_tiles, start_tile, 0)
        lax.fori_loop(0, n_tiles, wait_tile, 0)

    def issue(t, carry):
        d = dest_ref[base + t]
        _row_copy(hn_ref.at[pl.ds(t, 1)], xs_ref.at[pl.ds(d, 1)], sem).start()
        return carry

    lax.fori_loop(0, tm, issue, 0, unroll=ISSUE_UNROLL)
    _row_copy(hn_ref, xs_ref.at[pl.ds(0, tm)], sem).wait()


def _dispatch(dest, zero_tile, hn, n_slots):
    t = hn.shape[0]
    tm = ROW_TILE
    grid_spec = pltpu.PrefetchScalarGridSpec(
        num_scalar_prefetch=2,
        grid=(t // tm,),
        in_specs=[pl.BlockSpec((tm, ROW_WORDS), lambda i, *_: (i, 0))],
        out_specs=pl.BlockSpec(memory_space=pl.ANY),
        scratch_shapes=[pltpu.VMEM((EXPERT_TILE, ROW_WORDS), jnp.uint32),
                        pltpu.SemaphoreType.DMA(()),
                        pltpu.SemaphoreType.DMA(())],
    )
    return pl.pallas_call(
        _dispatch_kernel,
        grid_spec=grid_spec,
        out_shape=jax.ShapeDtypeStruct((n_slots, ROW_WORDS), jnp.uint32),
        compiler_params=_cparams(("arbitrary",)),
        name="moe_dispatch",
    )(dest, zero_tile, hn)


def _expert_kernel(tg_ref, tnext_ref, tv_ref, xs_ref, wg_hbm, wu_hbm, wd_hbm, ys_ref,
                   stage_g, stage_u, stage_d, wg_bf, wu_bf, wd_bf, sems, *, layer):
    i = pl.program_id(0)
    grp = tg_ref[i]
    first = i == 0
    changed = first | (grp != tg_ref[jnp.maximum(i - 1, 0)])
    ff = EXPERT_FF

    def weight_copies(g):
        e0 = g * EXPERTS_PER_GROUP
        return (_row_copy(wg_hbm.at[layer, pl.ds(e0, EXPERTS_PER_GROUP)], stage_g, sems.at[0]),
                _row_copy(wu_hbm.at[layer, pl.ds(e0, EXPERTS_PER_GROUP)], stage_u, sems.at[1]),
                _row_copy(wd_hbm.at[layer, pl.ds(e0, EXPERTS_PER_GROUP)], stage_d, sems.at[2]))

    @pl.when(first)
    def _():
        for cp in weight_copies(grp):
            cp.start()

    @pl.when(changed)
    def _():
        for cp in weight_copies(grp):
            cp.wait()
        for e in range(EXPERTS_PER_GROUP):
            wg_bf[:, e * ff:(e + 1) * ff] = stage_g[e].astype(BF16)
            wu_bf[:, e * ff:(e + 1) * ff] = stage_u[e].astype(BF16)
            wd_bf[e * ff:(e + 1) * ff, :] = stage_d[e].astype(BF16)
        nxt = tnext_ref[i]

        @pl.when(nxt != grp)
        def _():
            for cp in weight_copies(nxt):
                cp.start()

    @pl.when(tv_ref[i] == 1)
    def _():
        x_lo, x_hi = _unpack_bf16_pairs(xs_ref[:, :PACKED])
        x = jnp.concatenate([x_lo.astype(BF16), x_hi.astype(BF16)], axis=-1)
        wv = lax.bitcast_convert_type(xs_ref[:, PACKED:], F32)
        g = jnp.dot(x, wg_bf[...], preferred_element_type=F32)
        u = jnp.dot(x, wu_bf[...], preferred_element_type=F32)
        hid = g * jax.nn.sigmoid(g) * u
        hid = jnp.concatenate(
            [(hid[:, e * ff:(e + 1) * ff] * wv[:, e:e + 1]).astype(BF16)
             for e in range(EXPERTS_PER_GROUP)], axis=-1)
        y = jnp.dot(hid, wd_bf[...], preferred_element_type=F32)
        ys_ref[...] = _pack_bf16_pairs(y)

    @pl.when(tv_ref[i] == 0)
    def _():
        ys_ref[...] = jnp.zeros(ys_ref.shape, ys_ref.dtype)


def _experts(tile_group, tile_next, tile_valid, xs, w_gate, w_up, w_down, layer):
    p = xs.shape[0]
    tile = EXPERT_TILE
    group_ff = EXPERTS_PER_GROUP * EXPERT_FF
    grid_spec = pltpu.PrefetchScalarGridSpec(
        num_scalar_prefetch=3,
        grid=(p // tile,),
        in_specs=[
            pl.BlockSpec((tile, ROW_WORDS), lambda i, *_: (i, 0)),
            pl.BlockSpec(memory_space=pl.ANY),
            pl.BlockSpec(memory_space=pl.ANY),
            pl.BlockSpec(memory_space=pl.ANY),
        ],
        out_specs=pl.BlockSpec((tile, PACKED), lambda i, *_: (i, 0)),
        scratch_shapes=[pltpu.VMEM((EXPERTS_PER_GROUP, D_MODEL, EXPERT_FF), F32),
                        pltpu.VMEM((EXPERTS_PER_GROUP, D_MODEL, EXPERT_FF), F32),
                        pltpu.VMEM((EXPERTS_PER_GROUP, EXPERT_FF, D_MODEL), F32),
                        pltpu.VMEM((D_MODEL, group_ff), BF16),
                        pltpu.VMEM((D_MODEL, group_ff), BF16),
                        pltpu.VMEM((group_ff, D_MODEL), BF16),
                        pltpu.SemaphoreType.DMA((3,))],
    )
    return pl.pallas_call(
        functools.partial(_expert_kernel, layer=layer),
        grid_spec=grid_spec,
        out_shape=jax.ShapeDtypeStruct((p, PACKED), jnp.uint32),
        compiler_params=_cparams(("arbitrary",)),
        name="moe_experts",
    )(tile_group, tile_next, tile_valid, xs, w_gate, w_up, w_down)


def _combine_kernel(dest_ref, x1_ref, ys_ref, out_ref, buf_ref, sem):
    tm = x1_ref.shape[0]
    base = pl.program_id(0) * tm

    def issue(t, carry):
        d = dest_ref[base + t]
        _row_copy(ys_ref.at[pl.ds(d, 1)], buf_ref.at[pl.ds(t, 1)], sem).start()
        return carry

    lax.fori_loop(0, tm, issue, 0, unroll=ISSUE_UNROLL)
    _row_copy(ys_ref.at[pl.ds(0, tm)], buf_ref, sem).wait()

    lo, hi = _unpack_bf16_pairs(buf_ref[...])
    out_ref[:, :PACKED] = x1_ref[:, :PACKED] + lo
    out_ref[:, PACKED:] = x1_ref[:, PACKED:] + hi


def _combine(dest, x1, ys):
    t = x1.shape[0]
    tm = ROW_TILE
    grid_spec = pltpu.PrefetchScalarGridSpec(
        num_scalar_prefetch=1,
        grid=(t // tm,),
        in_specs=[pl.BlockSpec((tm, D_MODEL), lambda i, dest: (i, 0)),
                  pl.BlockSpec(memory_space=pl.ANY)],
        out_specs=pl.BlockSpec((tm, D_MODEL), lambda i, dest: (i, 0)),
        scratch_shapes=[pltpu.VMEM((tm, PACKED), jnp.uint32),
                        pltpu.SemaphoreType.DMA(())],
    )
    return pl.pallas_call(
        _combine_kernel,
        grid_spec=grid_spec,
        out_shape=jax.ShapeDtypeStruct((t, D_MODEL), F32),
        compiler_params=_cparams(("arbitrary",)),
        name="moe_combine",
    )(dest, x1, ys)


def _rope_tables(seq):
    inv_freq = 1.0 / (ROPE_THETA ** (jnp.arange(0, DIFF_HEAD_DIM, 2, dtype=F32) / DIFF_HEAD_DIM))
    ang = jnp.arange(seq, dtype=F32)[:, None] * jnp.tile(inv_freq, 4)[None, :]
    sign = jnp.where(jnp.arange(LANES) < LANES // 2, -1.0, 1.0).astype(F32)
    return jnp.cos(ang), jnp.sin(ang) * sign[None, :]


def _diff_head_layout(w):
    lead = w.shape[:-1]
    w = w.reshape(*lead, DIFF_HEADS, 2, 2, DIFF_HEAD_DIM // 2)
    w = jnp.swapaxes(w, -3, -2)
    return w.reshape(*lead, DIFF_QK_WIDTH)


def _diff_gain_layout(g, scale):
    g = (g.astype(F32) * scale).reshape(2, DIFF_HEAD_DIM // 2)
    g = jnp.broadcast_to(g[:, None, :], (2, 2, DIFF_HEAD_DIM // 2))
    return jnp.tile(g.reshape(1, LANES), (1, MXU_DIM // LANES))


def _pad_rope(w):
    half = MLA_ROPE // 2
    z = jnp.zeros(w.shape[:-1] + (half,), w.dtype)
    return jnp.concatenate([w[..., :half], z, w[..., half:], z], axis=-1)


def _mla_qk_layout(w):
    return jnp.concatenate([w[..., :MLA_NOPE], _pad_rope(w[..., MLA_NOPE:])], axis=-1)


def _segment_matrix():
    lane = jnp.arange(MXU_DIM)
    key = (lane // LANES) * 2 + (lane // 32) % 2
    return (key[:, None] == key[None, :]).astype(BF16)


def _moe(x1, hn, meta, cnt, w_gate, w_up, w_down, layer):
    t = x1.shape[0]
    tile = EXPERT_TILE
    n_slots = t + N_GROUPS * tile
    n_tiles = n_slots // tile
    groups = jnp.arange(N_GROUPS, dtype=jnp.int32)
    counts = cnt[0, N_EXPERTS:N_EXPERTS + N_GROUPS].astype(jnp.int32)
    padded = ((counts + tile - 1) // tile) * tile
    ends = jnp.cumsum(padded)
    offsets = ends - padded
    tile_start = jnp.arange(n_tiles, dtype=jnp.int32) * tile
    tile_valid = (tile_start < ends[-1]).astype(jnp.int32)
    tile_group = jnp.sum(tile_start[:, None] >= ends[None, :], axis=1).astype(jnp.int32)
    last_group = jnp.max(jnp.where(counts > 0, groups, 0))
    tile_group = jnp.where(tile_valid == 1, tile_group, last_group)
    later = (groups[None, :] > groups[:, None]) & (counts > 0)[None, :]
    next_of = jnp.min(jnp.where(later, groups[None, :], N_GROUPS), axis=1)
    next_of = jnp.where(next_of == N_GROUPS, groups, next_of)
    onehot_tile = tile_group[:, None] == groups[None, :]
    tile_next = jnp.sum(jnp.where(onehot_tile, next_of[None, :], 0), axis=1).astype(jnp.int32)
    partly_filled = jnp.any((tile_start[:, None] + tile == ends[None, :])
                            & (padded > counts)[None, :], axis=1)
    zero_tile = jnp.where(tile_valid == 1, partly_filled, True).astype(jnp.int32)
    grp = meta[:, 0].astype(jnp.int32)
    rank = meta[:, 1].astype(jnp.int32)
    dest = rank + jnp.sum(jnp.where(grp[:, None] == groups[None, :], offsets[None, :], 0), axis=-1)
    xs = _dispatch(dest, zero_tile, hn, n_slots)
    ys = _experts(tile_group, tile_next, tile_valid, xs, w_gate, w_up, w_down, layer)
    return _combine(dest, x1, ys)


def _router_weights(w_group, b_group, w_expert, b_expert):
    pad = LANES - N_EXPERTS - N_GROUPS
    w = jnp.concatenate([w_expert, w_group, jnp.zeros((D_MODEL, pad), F32)], axis=-1)
    b = jnp.concatenate([b_expert, b_group, jnp.zeros((pad,), F32)]).reshape(1, LANES)
    w_hi = w.astype(BF16)
    w_lo = (w - w_hi.astype(F32)).astype(BF16)
    return w_hi, w_lo, b


def kernel(x, attn_norm, ffn_norm, diff_w_in, diff_q_norm, diff_k_norm, diff_lambda_q1, diff_lambda_k1, diff_lambda_q2, diff_lambda_k2, diff_subln, diff_w_out, mla_w_a, mla_q_a_norm, mla_kv_a_norm, mla_w_qb, mla_w_kvb, mla_q_norm, mla_k_norm, mla_w_out, moe_w_group, moe_b_group, moe_w_expert, moe_b_expert, moe_w_gate, moe_w_up, moe_w_down):
    b, s, d = x.shape
    assert d == D_MODEL and s % Q_TILE == 0 and s % ROW_TILE == 0
    t = b * s
    rope_cos, rope_sin = _rope_tables(s)
    row = lambda v: v.astype(F32).reshape(1, -1)
    x2d = x.reshape(t, d)

    lambda_init = 0.8 - 0.6 * math.exp(-0.3 * 0)
    w_in = diff_w_in[0]
    w_in = jnp.concatenate([_diff_head_layout(w_in[:, :DIFF_QK_WIDTH]),
                            _diff_head_layout(w_in[:, DIFF_QK_WIDTH:2 * DIFF_QK_WIDTH]),
                            w_in[:, 2 * DIFF_QK_WIDTH:]], axis=-1).astype(BF16)
    q, k, vt = _diff_qkv(x2d, row(attn_norm[0]), w_in,
                         _diff_gain_layout(diff_q_norm[0], DIFF_HEAD_DIM ** -0.5 * LOG2E),
                         _diff_gain_layout(diff_k_norm[0], 1.0),
                         rope_cos, rope_sin, _segment_matrix(), s)
    shp = (b, s, DIFF_QK_WIDTH)
    o = _flash(q.reshape(shp), k.reshape(shp), vt, DIFF_HEADS, LANES, Q_TILE,
               diff_params=(row(diff_lambda_q1[0]), row(diff_lambda_k1[0]),
                            row(diff_lambda_q2[0]), row(diff_lambda_k2[0]),
                            row(diff_subln[0])),
               lambda_init=lambda_init)
    wr_hi, wr_lo, b_r = _router_weights(moe_w_group[0], moe_b_group[0],
                                        moe_w_expert[0], moe_b_expert[0])
    x1, hn, meta, cnt = _post_attn(o.reshape(t, d), x2d, diff_w_out[0].astype(BF16),
                                   row(ffn_norm[0]), wr_hi, wr_lo, b_r)
    x2d = _moe(x1, hn, meta, cnt, moe_w_gate, moe_w_up, moe_w_down, 0)

    w_a = mla_w_a[0]
    split = MLA_Q_LORA + MLA_KV_LORA
    w_a = jnp.concatenate([w_a[:, :split], _pad_rope(w_a[:, split:])], axis=-1).astype(BF16)
    w_qb = _mla_qk_layout(mla_w_qb[0].reshape(MLA_Q_LORA, MLA_HEADS, MLA_QK_DIM))
    w_qb = w_qb.reshape(MLA_Q_LORA, MLA_HEADS * MLA_QK_PAD).astype(BF16)
    w_kvb = mla_w_kvb[0].reshape(MLA_KV_LORA, MLA_HEADS, 2 * MLA_NOPE)
    w_kvb = jnp.concatenate([w_kvb[..., :MLA_NOPE].reshape(MLA_KV_LORA, -1),
                             w_kvb[..., MLA_NOPE:].reshape(MLA_KV_LORA, -1)], axis=-1).astype(BF16)
    q_gain = _mla_qk_layout(mla_q_norm[0].astype(F32) * (MLA_QK_DIM ** -0.5 * LOG2E)).reshape(1, -1)
    k_gain = _mla_qk_layout(mla_k_norm[0].astype(F32)).reshape(1, -1)
    q, k, vt = _mla_proj(x2d, row(attn_norm[1]), w_a, row(mla_q_a_norm[0]), row(mla_kv_a_norm[0]),
                        w_qb, w_kvb, q_gain, k_gain, rope_cos, rope_sin, s)
    qk_shp = (b, s, MLA_HEADS * MLA_QK_PAD)
    o = _flash(q.reshape(qk_shp), k.reshape(qk_shp), vt, MLA_HEADS, MLA_QK_PAD, Q_TILE)
    wr_hi, wr_lo, b_r = _router_weights(moe_w_group[1], moe_b_group[1],
                                        moe_w_expert[1], moe_b_expert[1])
    x1, hn, meta, cnt = _post_attn(o.reshape(t, d), x2d, mla_w_out[0].astype(BF16),
                                   row(ffn_norm[1]), wr_hi, wr_lo, b_r)
    x2d = _moe(x1, hn, meta, cnt, moe_w_gate, moe_w_up, moe_w_down, 1)
    return x2d.reshape(b, s, d)
```

```python
import functools
import math

import jax
import jax.numpy as jnp
from jax import lax
from jax.experimental import pallas as pl
from jax.experimental.pallas import tpu as pltpu

F32 = jnp.float32
BF16 = jnp.bfloat16

D_MODEL = 1024
PACKED = D_MODEL // 2
ROW_WORDS = PACKED + 128
ROPE_THETA = 10000.0
NORM_EPS = 1e-6
LANES = 128
MXU_DIM = 256

DIFF_HEAD_DIM = 64
DIFF_HEADS = 8
DIFF_QK_WIDTH = 1024

MLA_NOPE = 128
MLA_ROPE = 64
MLA_QK_DIM = 192
MLA_QK_PAD = 256
MLA_HEADS = 8
MLA_Q_LORA = 384
MLA_KV_LORA = 256

N_GROUPS = 8
EXPERTS_PER_GROUP = 8
N_EXPERTS = 64
TOP_K = 2
EXPERT_FF = 256

ROW_TILE = 512
Q_TILE = 1024
FLASH_CHUNK = 512
V_ROWS = 144
LOG2E = math.log2(math.e)
EXPERT_TILE = 256
PAIRS_PER_ITER = {2: 4, 4: 2}
PROJ_PARTS = 2
COPY_TILE = 2048
ISSUE_UNROLL = 8
VMEM_LIMIT = 56 * 1024 * 1024


def _cparams(sem):
    return pltpu.CompilerParams(dimension_semantics=sem, vmem_limit_bytes=VMEM_LIMIT)


def _rms(x, gain):
    return x * lax.rsqrt(jnp.mean(x * x, axis=-1, keepdims=True) + NORM_EPS) * gain


def _pack_bf16_pairs(x):
    lo = lax.bitcast_convert_type(x[:, :PACKED].astype(BF16).astype(F32), jnp.uint32)
    hi = lax.bitcast_convert_type(x[:, PACKED:].astype(BF16).astype(F32), jnp.uint32)
    return (lo >> 16) | (hi & jnp.uint32(0xFFFF0000))


def _unpack_bf16_pairs(p):
    lo = lax.bitcast_convert_type(p << 16, F32)
    hi = lax.bitcast_convert_type(p & jnp.uint32(0xFFFF0000), F32)
    return lo, hi


def _store_v_transposed(v, vt_ref, heads, cols=slice(None)):
    rows = v.shape[0]
    ones = jnp.ones((V_ROWS - LANES, rows), BF16)
    for hd in range(heads):
        lo = hd * V_ROWS
        vt_ref[0, 0, lo:lo + LANES, cols] = v[:, hd * LANES:(hd + 1) * LANES].T.astype(BF16)
        vt_ref[0, 0, lo + LANES:lo + V_ROWS, cols] = ones


def _rope128(u, cos, sin):
    return u * cos + pltpu.roll(u, 64, 1) * sin


def _diff_qkv_kernel(x_ref, g_ref, w_ref, qg_ref, kg_ref, cos_ref, sin_ref, seg_ref,
                     q_ref, k_ref, vt_ref):
    tm = x_ref.shape[0]
    seg = seg_ref[...]

    def project(rows):
        h = _rms(x_ref[rows, :], g_ref[...])
        return jnp.dot(h.astype(BF16), w_ref[...], preferred_element_type=F32)

    def heads(rows, qkv):
        cos = cos_ref[rows, :]
        sin = sin_ref[rows, :]
        for off, gain_ref, out_ref in ((0, qg_ref, q_ref), (DIFF_QK_WIDTH, kg_ref, k_ref)):
            for c in range(DIFF_QK_WIDTH // MXU_DIM):
                t = qkv[:, off + c * MXU_DIM: off + (c + 1) * MXU_DIM]
                ss = jnp.dot((t * t).astype(BF16), seg, preferred_element_type=F32)
                tn = t * lax.rsqrt(ss * (1.0 / DIFF_HEAD_DIM) + NORM_EPS) * gain_ref[...]
                for half in range(MXU_DIM // LANES):
                    u = tn[:, half * LANES:(half + 1) * LANES]
                    lo = c * MXU_DIM + half * LANES
                    out_ref[rows, lo:lo + LANES] = _rope128(u, cos, sin).astype(BF16)
        _store_v_transposed(qkv[:, 2 * DIFF_QK_WIDTH:], vt_ref, DIFF_HEADS, rows)

    part = tm // PROJ_PARTS
    rows = [pl.ds(p * part, part) for p in range(PROJ_PARTS)]
    staged = project(rows[0])
    for p in range(PROJ_PARTS):
        upcoming = project(rows[p + 1]) if p + 1 < PROJ_PARTS else None
        heads(rows[p], staged)
        staged = upcoming


def _diff_qkv(x2d, gain, w_in, q_gain, k_gain, cos, sin, seg, seq):
    t = x2d.shape[0]
    tm = ROW_TILE
    pos_tiles = seq // tm
    row = lambda i: (i, 0)
    const = lambda i: (0, 0)
    out = jax.ShapeDtypeStruct((t, DIFF_QK_WIDTH), BF16)
    vt_out = jax.ShapeDtypeStruct((t // seq, pos_tiles, DIFF_HEADS * V_ROWS, tm), BF16)
    vt_spec = pl.BlockSpec((1, 1, DIFF_HEADS * V_ROWS, tm),
                           lambda i: (i // pos_tiles, i % pos_tiles, 0, 0))
    return pl.pallas_call(
        _diff_qkv_kernel,
        grid=(t // tm,),
        in_specs=[
            pl.BlockSpec((tm, D_MODEL), row),
            pl.BlockSpec((1, D_MODEL), const),
            pl.BlockSpec(w_in.shape, const),
            pl.BlockSpec((1, MXU_DIM), const),
            pl.BlockSpec((1, MXU_DIM), const),
            pl.BlockSpec((tm, LANES), lambda i: (i % pos_tiles, 0)),
            pl.BlockSpec((tm, LANES), lambda i: (i % pos_tiles, 0)),
            pl.BlockSpec((MXU_DIM, MXU_DIM), const),
        ],
        out_specs=[pl.BlockSpec((tm, DIFF_QK_WIDTH), row)] * 2 + [vt_spec],
        out_shape=[out, out, vt_out],
        compiler_params=_cparams(("parallel",)),
        name="diff_qkv",
    )(x2d, gain, w_in, q_gain, k_gain, cos, sin, seg)


def _mla_proj_kernel(x_ref, g_ref, wa_ref, qag_ref, kvag_ref, wqb_ref, wkvb_ref,
                     qg_ref, kg_ref, cos_ref, sin_ref, q_ref, k_ref, vt_ref):
    tm = x_ref.shape[0]
    qg = qg_ref[...]
    kg = kg_ref[...]
    inv_d = 1.0 / MLA_QK_DIM

    def low_rank(rows):
        h = _rms(x_ref[rows, :], g_ref[...])
        return jnp.dot(h.astype(BF16), wa_ref[...], preferred_element_type=F32)

    def up_project(a):
        cq = _rms(a[:, :MLA_Q_LORA], qag_ref[...])
        ckv = _rms(a[:, MLA_Q_LORA:MLA_Q_LORA + MLA_KV_LORA], kvag_ref[...])
        q = jnp.dot(cq.astype(BF16), wqb_ref[...], preferred_element_type=F32)
        kv = jnp.dot(ckv.astype(BF16), wkvb_ref[...], preferred_element_type=F32)
        return q, kv, a[:, MLA_Q_LORA + MLA_KV_LORA:]

    def heads(rows, q, kv, kpe):
        cos = cos_ref[rows, :]
        sin = sin_ref[rows, :]
        kpe_ss = jnp.sum(kpe * kpe, axis=-1, keepdims=True)
        kpe_rot = _rope128(kpe * kg[:, MLA_NOPE:], cos, sin)
        for hd in range(MLA_HEADS):
            lo = hd * MLA_QK_PAD
            qh = q[:, lo:lo + MLA_QK_PAD]
            rq = lax.rsqrt(jnp.sum(qh * qh, axis=-1, keepdims=True) * inv_d + NORM_EPS)
            q_ref[rows, lo:lo + MLA_NOPE] = (qh[:, :MLA_NOPE] * rq * qg[:, :MLA_NOPE]).astype(BF16)
            q_ref[rows, lo + MLA_NOPE:lo + MLA_QK_PAD] = _rope128(
                qh[:, MLA_NOPE:] * rq * qg[:, MLA_NOPE:], cos, sin).astype(BF16)
            kn = kv[:, hd * MLA_NOPE:(hd + 1) * MLA_NOPE]
            rk = lax.rsqrt((jnp.sum(kn * kn, axis=-1, keepdims=True) + kpe_ss) * inv_d + NORM_EPS)
            k_ref[rows, lo:lo + MLA_NOPE] = (kn * rk * kg[:, :MLA_NOPE]).astype(BF16)
            k_ref[rows, lo + MLA_NOPE:lo + MLA_QK_PAD] = (kpe_rot * rk).astype(BF16)
        _store_v_transposed(kv[:, MLA_HEADS * MLA_NOPE:], vt_ref, MLA_HEADS, rows)

    part = tm // PROJ_PARTS
    rows = [pl.ds(p * part, part) for p in range(PROJ_PARTS)]
    staged = up_project(low_rank(rows[0]))
    for p in range(PROJ_PARTS):
        upcoming = up_project(low_rank(rows[p + 1])) if p + 1 < PROJ_PARTS else None
        heads(rows[p], *staged)
        staged = upcoming


def _mla_proj(x2d, gain, w_a, qa_gain, kva_gain, w_qb, w_kvb, q_gain, k_gain, cos, sin, seq):
    t = x2d.shape[0]
    tm = ROW_TILE
    pos_tiles = seq // tm
    row = lambda i: (i, 0)
    const = lambda i: (0, 0)
    qk_out = jax.ShapeDtypeStruct((t, MLA_HEADS * MLA_QK_PAD), BF16)
    vt_out = jax.ShapeDtypeStruct((t // seq, pos_tiles, MLA_HEADS * V_ROWS, tm), BF16)
    return pl.pallas_call(
        _mla_proj_kernel,
        grid=(t // tm,),
        in_specs=[
            pl.BlockSpec((tm, D_MODEL), row),
            pl.BlockSpec((1, D_MODEL), const),
            pl.BlockSpec(w_a.shape, const),
            pl.BlockSpec((1, MLA_Q_LORA), const),
            pl.BlockSpec((1, MLA_KV_LORA), const),
            pl.BlockSpec(w_qb.shape, const),
            pl.BlockSpec(w_kvb.shape, const),
            pl.BlockSpec((1, MLA_QK_PAD), const),
            pl.BlockSpec((1, MLA_QK_PAD), const),
            pl.BlockSpec((tm, LANES), lambda i: (i % pos_tiles, 0)),
            pl.BlockSpec((tm, LANES), lambda i: (i % pos_tiles, 0)),
        ],
        out_specs=[pl.BlockSpec((tm, MLA_HEADS * MLA_QK_PAD), row),
                   pl.BlockSpec((tm, MLA_HEADS * MLA_QK_PAD), row),
                   pl.BlockSpec((1, 1, MLA_HEADS * V_ROWS, tm),
                                lambda i: (i // pos_tiles, i % pos_tiles, 0, 0))],
        out_shape=[qk_out, qk_out, vt_out],
        compiler_params=_cparams(("parallel",)),
        name="mla_proj",
    )(x2d, gain, w_a, qa_gain, kva_gain, w_qb, w_kvb, q_gain, k_gain, cos, sin)


def _flash_kernel(*refs, diff, lambda_init, tq, tk, n_chunks):
    if diff:
        (lq1_ref, lk1_ref, lq2_ref, lk2_ref, subln_ref, q_ref, k_ref, vt_ref, o_ref,
         qt_ref, m_ref, acc_ref, sa_ref, sb_ref) = refs
    else:
        q_ref, k_ref, vt_ref, o_ref, qt_ref, m_ref, acc_ref, sa_ref, sb_ref = refs
    qi = pl.program_id(2)
    ch = FLASH_CHUNK
    copies = 2 if diff else 1
    chunks_per_copy = n_chunks // copies

    q_t = q_ref[0].astype(F32).T
    if diff:
        feat = lax.broadcasted_iota(jnp.int32, q_t.shape, 0)
        first = ((feat // 32) % 2) == 0
        qt_ref[:, :tq] = jnp.where(first, q_t, 0.0).astype(BF16)
        qt_ref[:, tq:] = jnp.where(first, 0.0, q_t).astype(BF16)
    else:
        qt_ref[...] = q_t.astype(BF16)
    m_ref[...] = jnp.full(m_ref.shape, -jnp.inf, F32)
    acc_ref[...] = jnp.zeros(acc_ref.shape, F32)

    def key_tile(j):
        return k_ref[0, pl.ds(pl.multiple_of(j * tk, tk), tk), :]

    def scores(j, s_ref, chunks):
        k = key_tile(j)
        for c in chunks:
            s_ref[c] = jnp.dot(k, qt_ref[:, c * ch:(c + 1) * ch], preferred_element_type=F32)

    def accumulate(j, s_ref, chunks, masked=()):
        vt = vt_ref[0, j]
        for c in chunks:
            s = s_ref[c]
            if c in masked:
                key = lax.broadcasted_iota(jnp.int32, (tk, ch), 0)
                qry = lax.broadcasted_iota(jnp.int32, (tk, ch), 1)
                s = jnp.where(key <= qry, s, -jnp.inf)
            m_prev = m_ref[c]
            m_new = jnp.maximum(m_prev, jnp.max(s, axis=0, keepdims=True))
            alpha = jnp.exp2(m_prev - m_new)
            p = jnp.exp2(s - m_new).astype(BF16)
            acc_ref[c] = alpha * acc_ref[c] + jnp.dot(vt, p, preferred_element_type=F32)
            m_ref[c] = m_new

    every = tuple(range(n_chunks))
    early = tuple(c for c in every if c % chunks_per_copy == 0)
    late = tuple(c for c in every if c % chunks_per_copy == 1)
    scores(0, sa_ref, every)

    def pair(j):
        scores(j + 1, sb_ref, every)
        accumulate(j, sa_ref, every)
        scores(j + 2, sa_ref, every)
        accumulate(j + 1, sb_ref, every)

    def pairs(first_pair, count):
        for r in range(count):
            pair(2 * (first_pair + r))

    per_iter = PAIRS_PER_ITER[n_chunks]

    def main_body(i, carry):
        pairs(per_iter * i, per_iter)
        return carry

    lax.fori_loop(0, qi // per_iter, main_body, 0)
    block = per_iter // 2
    while block >= 1:
        @pl.when(qi % (2 * block) >= block)
        def _(block=block):
            pairs(qi - qi % (2 * block), block)
        block //= 2

    j = 2 * qi
    scores(j + 1, sb_ref, late)
    accumulate(j, sa_ref, every, masked=early)
    accumulate(j + 1, sb_ref, late, masked=late)

    def normalized(c):
        a = acc_ref[c]
        return a[:LANES] * (1.0 / a[LANES:LANES + 1])

    if diff:
        lam = (jnp.exp(jnp.sum(lq1_ref[...] * lk1_ref[...], axis=-1, keepdims=True))
               - jnp.exp(jnp.sum(lq2_ref[...] * lk2_ref[...], axis=-1, keepdims=True))
               + lambda_init)
        for c in range(chunks_per_copy):
            o_t = normalized(c) - lam * normalized(c + chunks_per_copy)
            o_t = o_t * lax.rsqrt(jnp.mean(o_t * o_t, axis=0, keepdims=True) + NORM_EPS)
            o = o_t.T * (subln_ref[...] * (1.0 - lambda_init))
            o_ref[0, c * ch:(c + 1) * ch, :] = o.astype(o_ref.dtype)
    else:
        for c in range(n_chunks):
            o_ref[0, c * ch:(c + 1) * ch, :] = normalized(c).T.astype(o_ref.dtype)


def _flash(q, k, vt, heads, dk, tq, diff_params=None, lambda_init=0.0):
    b, s, _ = q.shape
    tk = ROW_TILE
    diff = diff_params is not None
    n_chunks = (2 if diff else 1) * tq // FLASH_CHUNK
    q_spec = pl.BlockSpec((1, tq, dk), lambda bi, hi, qi: (bi, qi, hi))
    k_spec = pl.BlockSpec((1, s, dk), lambda bi, hi, qi: (bi, 0, hi))
    v_spec = pl.BlockSpec((1, s // tk, V_ROWS, tk), lambda bi, hi, qi: (bi, 0, hi, 0))
    o_spec = pl.BlockSpec((1, tq, LANES), lambda bi, hi, qi: (bi, qi, hi))
    small = lambda shape: pl.BlockSpec(shape, lambda bi, hi, qi: (0, 0))
    in_specs = [q_spec, k_spec, v_spec]
    args = [q, k, vt]
    if diff:
        in_specs = [small((1, DIFF_HEAD_DIM))] * 4 + [small((1, LANES))] + in_specs
        args = list(diff_params) + args
    assert tk == FLASH_CHUNK and tq == 2 * tk
    scratch = [pltpu.VMEM((dk, n_chunks * FLASH_CHUNK), BF16),
               pltpu.VMEM((n_chunks, 1, FLASH_CHUNK), F32),
               pltpu.VMEM((n_chunks, V_ROWS, FLASH_CHUNK), F32),
               pltpu.VMEM((n_chunks, tk, FLASH_CHUNK), F32),
               pltpu.VMEM((n_chunks, tk, FLASH_CHUNK), F32)]
    return pl.pallas_call(
        functools.partial(_flash_kernel, diff=diff, lambda_init=lambda_init, tq=tq, tk=tk,
                          n_chunks=n_chunks),
        grid=(b, heads, s // tq),
        in_specs=in_specs,
        out_specs=o_spec,
        out_shape=jax.ShapeDtypeStruct((b, s, heads * LANES), BF16),
        scratch_shapes=scratch,
        compiler_params=_cparams(("parallel", "parallel", "arbitrary")),
        name="diff_flash" if diff else "mla_flash",
    )(*args)


def _post_attn_kernel(o_ref, x_ref, wo_ref, fg_ref, wrh_ref, wrl_ref, br_ref,
                      x1_ref, hn_ref, meta_ref, cnt_ref, carry_ref):
    i = pl.program_id(0)

    @pl.when(i == 0)
    def _():
        carry_ref[...] = jnp.zeros(carry_ref.shape, F32)

    x1 = x_ref[...] + jnp.dot(o_ref[...], wo_ref[...], preferred_element_type=F32)
    x1_ref[...] = x1
    hn = _rms(x1, fg_ref[...])
    hn_ref[:, :PACKED] = _pack_bf16_pairs(hn)

    hi = hn.astype(BF16)
    lo = (hn - hi.astype(F32)).astype(BF16)
    wrh = wrh_ref[...]
    logits = (jnp.dot(hi, wrh, preferred_element_type=F32)
              + jnp.dot(lo, wrh, preferred_element_type=F32)
              + jnp.dot(hi, wrl_ref[...], preferred_element_type=F32)
              + br_ref[...])
    tm = logits.shape[0]
    lane = lax.broadcasted_iota(jnp.int32, logits.shape, 1)
    lane_f = lane.astype(F32)
    neg = jnp.float32(-jnp.inf)
    big = jnp.float32(1e9)

    is_group = (lane >= N_EXPERTS) & (lane < N_EXPERTS + N_GROUPS)
    gl = jnp.where(is_group, logits, neg)
    gmax = jnp.max(gl, axis=-1, keepdims=True)
    gidx = jnp.min(jnp.where(gl == gmax, lane_f, big), axis=-1, keepdims=True) - N_EXPERTS
    g_w = 1.0 / jnp.sum(jnp.exp(gl - gmax), axis=-1, keepdims=True)

    in_group = (lane < N_EXPERTS) & ((lane // EXPERTS_PER_GROUP).astype(F32) == gidx)
    el = jnp.where(in_group, logits, neg)
    m1 = jnp.max(el, axis=-1, keepdims=True)
    i1 = jnp.min(jnp.where(el == m1, lane_f, big), axis=-1, keepdims=True)
    el2 = jnp.where(lane_f == i1, neg, el)
    m2 = jnp.max(el2, axis=-1, keepdims=True)
    i2 = jnp.min(jnp.where(el2 == m2, lane_f, big), axis=-1, keepdims=True)
    d = jnp.exp(m2 - m1)
    w1 = g_w / (1.0 + d)
    w2 = g_w * d / (1.0 + d)

    first_expert = gidx * EXPERTS_PER_GROUP
    wv = (jnp.where(lane_f == i1 - first_expert, w1, 0.0)
          + jnp.where(lane_f == i2 - first_expert, w2, 0.0))
    hn_ref[:, PACKED:] = lax.bitcast_convert_type(wv, jnp.uint32)

    in_grp = lane_f == gidx + N_EXPERTS
    oh = jnp.where(in_grp, 1.0, 0.0)
    r = lax.broadcasted_iota(jnp.int32, (tm, tm), 0)
    c = lax.broadcasted_iota(jnp.int32, (tm, tm), 1)
    lower = jnp.where(c < r, 1.0, 0.0).astype(BF16)
    prefix = jnp.dot(lower, oh.astype(BF16), preferred_element_type=F32) + carry_ref[...]
    rank = jnp.sum(jnp.where(in_grp, prefix, 0.0), axis=-1, keepdims=True)
    carry = carry_ref[...] + jnp.sum(oh, axis=0, keepdims=True)
    carry_ref[...] = carry
    cnt_ref[...] = jnp.broadcast_to(carry, cnt_ref.shape)
    meta_ref[...] = jnp.where(lane == 0, gidx, jnp.where(lane == 1, rank, 0.0))


def _post_attn(o2d, x2d, w_out, ffn_gain, wr_hi, wr_lo, b_r):
    t = x2d.shape[0]
    tm = ROW_TILE
    row = lambda i: (i, 0)
    const = lambda i: (0, 0)
    return pl.pallas_call(
        _post_attn_kernel,
        grid=(t // tm,),
        in_specs=[
            pl.BlockSpec((tm, D_MODEL), row),
            pl.BlockSpec((tm, D_MODEL), row),
            pl.BlockSpec((D_MODEL, D_MODEL), const),
            pl.BlockSpec((1, D_MODEL), const),
            pl.BlockSpec((D_MODEL, LANES), const),
            pl.BlockSpec((D_MODEL, LANES), const),
            pl.BlockSpec((1, LANES), const),
        ],
        out_specs=[pl.BlockSpec((tm, D_MODEL), row),
                   pl.BlockSpec((tm, ROW_WORDS), row),
                   pl.BlockSpec((tm, LANES), row),
                   pl.BlockSpec((8, LANES), const)],
        out_shape=[jax.ShapeDtypeStruct((t, D_MODEL), F32),
                   jax.ShapeDtypeStruct((t, ROW_WORDS), jnp.uint32),
                   jax.ShapeDtypeStruct((t, LANES), F32),
                   jax.ShapeDtypeStruct((8, LANES), F32)],
        scratch_shapes=[pltpu.VMEM((1, LANES), F32)],
        compiler_params=_cparams(("arbitrary",)),
        name="post_attn_router",
    )(o2d, x2d, w_out, ffn_gain, wr_hi, wr_lo, b_r)


def _row_copy(src, dst, sem):
    return pltpu.make_async_copy(src, dst, sem)


def _dispatch_kernel(dest_ref, zero_tile_ref, hn_ref, xs_ref, zeros_ref, sem, pad_sem):
    tm = hn_ref.shape[0]
    tile = zeros_ref.shape[0]
    step = pl.program_id(0)
    base = step * tm

    @pl.when(step == 0)
    def _():
        zeros_ref[...] = jnp.zeros(zeros_ref.shape, zeros_ref.dtype)

        def tile_copy(tl):
            row0 = pl.multiple_of(tl * tile, tile)
            return _row_copy(zeros_ref, xs_ref.at[pl.ds(row0, tile)], pad_sem)

        def start_tile(tl, carry):
            @pl.when(zero_tile_ref[tl] == 1)
            def _():
                tile_copy(tl).start()
            return carry

        def wait_tile(tl, carry):
            @pl.when(zero_tile_ref[tl] == 1)
            def _():
                tile_copy(tl).wait()
            return carry

        n_tiles = xs_ref.shape[0] // tile
        lax.fori_loop(0, n_tiles, start_tile, 0)
        lax.fori_loop(0, n_tiles, wait_tile, 0)

    def issue(t, carry):
        d = dest_ref[base + t]
        _row_copy(hn_ref.at[pl.ds(t, 1)], xs_ref.at[pl.ds(d, 1)], sem).start()
        return carry

    lax.fori_loop(0, tm, issue, 0, unroll=ISSUE_UNROLL)
    _row_copy(hn_ref, xs_ref.at[pl.ds(0, tm)], sem).wait()


def _dispatch(dest, zero_tile, hn, n_slots):
    t = hn.shape[0]
    tm = COPY_TILE
    grid_spec = pltpu.PrefetchScalarGridSpec(
        num_scalar_prefetch=2,
        grid=(t // tm,),
        in_specs=[pl.BlockSpec((tm, ROW_WORDS), lambda i, *_: (i, 0))],
        out_specs=pl.BlockSpec(memory_space=pl.ANY),
        scratch_shapes=[pltpu.VMEM((EXPERT_TILE, ROW_WORDS), jnp.uint32),
                        pltpu.SemaphoreType.DMA(()),
                        pltpu.SemaphoreType.DMA(())],
    )
    return pl.pallas_call(
        _dispatch_kernel,
        grid_spec=grid_spec,
        out_shape=jax.ShapeDtypeStruct((n_slots, ROW_WORDS), jnp.uint32),
        compiler_params=_cparams(("arbitrary",)),
        name="moe_dispatch",
    )(dest, zero_tile, hn)


def _expert_kernel(tg_ref, tnext_ref, tv_ref, xs_ref, wg_hbm, wu_hbm, wd_hbm, ys_ref,
                   stage_g, stage_u, stage_d, wg_bf, wu_bf, wd_bf, sems, *, layer):
    i = pl.program_id(0)
    grp = tg_ref[i]
    first = i == 0
    changed = first | (grp != tg_ref[jnp.maximum(i - 1, 0)])
    ff = EXPERT_FF

    def weight_copies(g):
        e0 = g * EXPERTS_PER_GROUP
        return (_row_copy(wg_hbm.at[layer, pl.ds(e0, EXPERTS_PER_GROUP)], stage_g, sems.at[0]),
                _row_copy(wu_hbm.at[layer, pl.ds(e0, EXPERTS_PER_GROUP)], stage_u, sems.at[1]),
                _row_copy(wd_hbm.at[layer, pl.ds(e0, EXPERTS_PER_GROUP)], stage_d, sems.at[2]))

    @pl.when(first)
    def _():
        for cp in weight_copies(grp):
            cp.start()

    @pl.when(changed)
    def _():
        for cp in weight_copies(grp):
            cp.wait()
        for e in range(EXPERTS_PER_GROUP):
            wg_bf[:, e * ff:(e + 1) * ff] = stage_g[e].astype(BF16)
            wu_bf[:, e * ff:(e + 1) * ff] = stage_u[e].astype(BF16)
            wd_bf[e * ff:(e + 1) * ff, :] = stage_d[e].astype(BF16)
        nxt = tnext_ref[i]

        @pl.when(nxt != grp)
        def _():
            for cp in weight_copies(nxt):
                cp.start()

    @pl.when(tv_ref[i] == 1)
    def _():
        x_lo, x_hi = _unpack_bf16_pairs(xs_ref[:, :PACKED])
        x = jnp.concatenate([x_lo.astype(BF16), x_hi.astype(BF16)], axis=-1)
        wv = lax.bitcast_convert_type(xs_ref[:, PACKED:], F32)
        g = jnp.dot(x, wg_bf[...], preferred_element_type=F32)
        u = jnp.dot(x, wu_bf[...], preferred_element_type=F32)
        hid = g * jax.nn.sigmoid(g) * u
        hid = jnp.concatenate(
            [(hid[:, e * ff:(e + 1) * ff] * wv[:, e:e + 1]).astype(BF16)
             for e in range(EXPERTS_PER_GROUP)], axis=-1)
        y = jnp.dot(hid, wd_bf[...], preferred_element_type=F32)
        ys_ref[...] = _pack_bf16_pairs(y)

    @pl.when(tv_ref[i] == 0)
    def _():
        ys_ref[...] = jnp.zeros(ys_ref.shape, ys_ref.dtype)


def _experts(tile_group, tile_next, tile_valid, xs, w_gate, w_up, w_down, layer):
    p = xs.shape[0]
    tile = EXPERT_TILE
    group_ff = EXPERTS_PER_GROUP * EXPERT_FF
    grid_spec = pltpu.PrefetchScalarGridSpec(
        num_scalar_prefetch=3,
        grid=(p // tile,),
        in_specs=[
            pl.BlockSpec((tile, ROW_WORDS), lambda i, *_: (i, 0)),
            pl.BlockSpec(memory_space=pl.ANY),
            pl.BlockSpec(memory_space=pl.ANY),
            pl.BlockSpec(memory_space=pl.ANY),
        ],
        out_specs=pl.BlockSpec((tile, PACKED), lambda i, *_: (i, 0)),
        scratch_shapes=[pltpu.VMEM((EXPERTS_PER_GROUP, D_MODEL, EXPERT_FF), F32),
                        pltpu.VMEM((EXPERTS_PER_GROUP, D_MODEL, EXPERT_FF), F32),
                        pltpu.VMEM((EXPERTS_PER_GROUP, EXPERT_FF, D_MODEL), F32),
                        pltpu.VMEM((D_MODEL, group_ff), BF16),
                        pltpu.VMEM((D_MODEL, group_ff), BF16),
                        pltpu.VMEM((group_ff, D_MODEL), BF16),
                        pltpu.SemaphoreType.DMA((3,))],
    )
    return pl.pallas_call(
        functools.partial(_expert_kernel, layer=layer),
        grid_spec=grid_spec,
        out_shape=jax.ShapeDtypeStruct((p, PACKED), jnp.uint32),
        compiler_params=_cparams(("arbitrary",)),
        name="moe_experts",
    )(tile_group, tile_next, tile_valid, xs, w_gate, w_up, w_down)


def _combine_kernel(dest_ref, x1_ref, ys_ref, out_ref, buf_ref, sem):
    tm = x1_ref.shape[0]
    base = pl.program_id(0) * tm

    def issue(t, carry):
        d = dest_ref[base + t]
        _row_copy(ys_ref.at[pl.ds(d, 1)], buf_ref.at[pl.ds(t, 1)], sem).start()
        return carry

    lax.fori_loop(0, tm, issue, 0, unroll=ISSUE_UNROLL)
    _row_copy(ys_ref.at[pl.ds(0, tm)], buf_ref, sem).wait()

    lo, hi = _unpack_bf16_pairs(buf_ref[...])
    out_ref[:, :PACKED] = x1_ref[:, :PACKED] + lo
    out_ref[:, PACKED:] = x1_ref[:, PACKED:] + hi


def _combine(dest, x1, ys):
    t = x1.shape[0]
    tm = COPY_TILE
    grid_spec = pltpu.PrefetchScalarGridSpec(
        num_scalar_prefetch=1,
        grid=(t // tm,),
        in_specs=[pl.BlockSpec((tm, D_MODEL), lambda i, dest: (i, 0)),
                  pl.BlockSpec(memory_space=pl.ANY)],
        out_specs=pl.BlockSpec((tm, D_MODEL), lambda i, dest: (i, 0)),
        scratch_shapes=[pltpu.VMEM((tm, PACKED), jnp.uint32),
                        pltpu.SemaphoreType.DMA(())],
    )
    return pl.pallas_call(
        _combine_kernel,
        grid_spec=grid_spec,
        out_shape=jax.ShapeDtypeStruct((t, D_MODEL), F32),
        compiler_params=_cparams(("arbitrary",)),
        name="moe_combine",
    )(dest, x1, ys)


def _rope_tables(seq):
    inv_freq = 1.0 / (ROPE_THETA ** (jnp.arange(0, DIFF_HEAD_DIM, 2, dtype=F32) / DIFF_HEAD_DIM))
    ang = jnp.arange(seq, dtype=F32)[:, None] * jnp.tile(inv_freq, 4)[None, :]
    sign = jnp.where(jnp.arange(LANES) < LANES // 2, -1.0, 1.0).astype(F32)
    return jnp.cos(ang), jnp.sin(ang) * sign[None, :]


def _diff_head_layout(w):
    lead = w.shape[:-1]
    w = w.reshape(*lead, DIFF_HEADS, 2, 2, DIFF_HEAD_DIM // 2)
    w = jnp.swapaxes(w, -3, -2)
    return w.reshape(*lead, DIFF_QK_WIDTH)


def _diff_gain_layout(g, scale):
    g = (g.astype(F32) * scale).reshape(2, DIFF_HEAD_DIM // 2)
    g = jnp.broadcast_to(g[:, None, :], (2, 2, DIFF_HEAD_DIM // 2))
    return jnp.tile(g.reshape(1, LANES), (1, MXU_DIM // LANES))


def _pad_rope(w):
    half = MLA_ROPE // 2
    z = jnp.zeros(w.shape[:-1] + (half,), w.dtype)
    return jnp.concatenate([w[..., :half], z, w[..., half:], z], axis=-1)


def _mla_qk_layout(w):
    return jnp.concatenate([w[..., :MLA_NOPE], _pad_rope(w[..., MLA_NOPE:])], axis=-1)


def _segment_matrix():
    lane = jnp.arange(MXU_DIM)
    key = (lane // LANES) * 2 + (lane // 32) % 2
    return (key[:, None] == key[None, :]).astype(BF16)


def _moe(x1, hn, meta, cnt, w_gate, w_up, w_down, layer):
    t = x1.shape[0]
    tile = EXPERT_TILE
    n_slots = t + N_GROUPS * tile
    n_tiles = n_slots // tile
    groups = jnp.arange(N_GROUPS, dtype=jnp.int32)
    counts = cnt[0, N_EXPERTS:N_EXPERTS + N_GROUPS].astype(jnp.int32)
    padded = ((counts + tile - 1) // tile) * tile
    ends = jnp.cumsum(padded)
    offsets = ends - padded
    tile_start = jnp.arange(n_tiles, dtype=jnp.int32) * tile
    tile_valid = (tile_start < ends[-1]).astype(jnp.int32)
    tile_group = jnp.sum(tile_start[:, None] >= ends[None, :], axis=1).astype(jnp.int32)
    last_group = jnp.max(jnp.where(counts > 0, groups, 0))
    tile_group = jnp.where(tile_valid == 1, tile_group, last_group)
    later = (groups[None, :] > groups[:, None]) & (counts > 0)[None, :]
    next_of = jnp.min(jnp.where(later, groups[None, :], N_GROUPS), axis=1)
    next_of = jnp.where(next_of == N_GROUPS, groups, next_of)
    onehot_tile = tile_group[:, None] == groups[None, :]
    tile_next = jnp.sum(jnp.where(onehot_tile, next_of[None, :], 0), axis=1).astype(jnp.int32)
    partly_filled = jnp.any((tile_start[:, None] + tile == ends[None, :])
                            & (padded > counts)[None, :], axis=1)
    zero_tile = jnp.where(tile_valid == 1, partly_filled, True).astype(jnp.int32)
    grp = meta[:, 0].astype(jnp.int32)
    rank = meta[:, 1].astype(jnp.int32)
    dest = rank + jnp.sum(jnp.where(grp[:, None] == groups[None, :], offsets[None, :], 0), axis=-1)
    xs = _dispatch(dest, zero_tile, hn, n_slots)
    ys = _experts(tile_group, tile_next, tile_valid, xs, w_gate, w_up, w_down, layer)
    return _combine(dest, x1, ys)


def _router_weights(w_group, b_group, w_expert, b_expert):
    pad = LANES - N_EXPERTS - N_GROUPS
    w = jnp.concatenate([w_expert, w_group, jnp.zeros((D_MODEL, pad), F32)], axis=-1)
    b = jnp.concatenate([b_expert, b_group, jnp.zeros((pad,), F32)]).reshape(1, LANES)
    w_hi = w.astype(BF16)
    w_lo = (w - w_hi.astype(F32)).astype(BF16)
    return w_hi, w_lo, b


def kernel(x, attn_norm, ffn_norm, diff_w_in, diff_q_norm, diff_k_norm, diff_lambda_q1, diff_lambda_k1, diff_lambda_q2, diff_lambda_k2, diff_subln, diff_w_out, mla_w_a, mla_q_a_norm, mla_kv_a_norm, mla_w_qb, mla_w_kvb, mla_q_norm, mla_k_norm, mla_w_out, moe_w_group, moe_b_group, moe_w_expert, moe_b_expert, moe_w_gate, moe_w_up, moe_w_down):
    b, s, d = x.shape
    assert d == D_MODEL and s % Q_TILE == 0 and s % ROW_TILE == 0 and (b * s) % COPY_TILE == 0
    t = b * s
    rope_cos, rope_sin = _rope_tables(s)
    row = lambda v: v.astype(F32).reshape(1, -1)
    x2d = x.reshape(t, d)

    lambda_init = 0.8 - 0.6 * math.exp(-0.3 * 0)
    w_in = diff_w_in[0]
    w_in = jnp.concatenate([_diff_head_layout(w_in[:, :DIFF_QK_WIDTH]),
                            _diff_head_layout(w_in[:, DIFF_QK_WIDTH:2 * DIFF_QK_WIDTH]),
                            w_in[:, 2 * DIFF_QK_WIDTH:]], axis=-1).astype(BF16)
    q, k, vt = _diff_qkv(x2d, row(attn_norm[0]), w_in,
                         _diff_gain_layout(diff_q_norm[0], DIFF_HEAD_DIM ** -0.5 * LOG2E),
                         _diff_gain_layout(diff_k_norm[0], 1.0),
                         rope_cos, rope_sin, _segment_matrix(), s)
    shp = (b, s, DIFF_QK_WIDTH)
    o = _flash(q.reshape(shp), k.reshape(shp), vt, DIFF_HEADS, LANES, Q_TILE,
               diff_params=(row(diff_lambda_q1[0]), row(diff_lambda_k1[0]),
                            row(diff_lambda_q2[0]), row(diff_lambda_k2[0]),
                            row(diff_subln[0])),
               lambda_init=lambda_init)
    wr_hi, wr_lo, b_r = _router_weights(moe_w_group[0], moe_b_group[0],
                                        moe_w_expert[0], moe_b_expert[0])
    x1, hn, meta, cnt = _post_attn(o.reshape(t, d), x2d, diff_w_out[0].astype(BF16),
                                   row(ffn_norm[0]), wr_hi, wr_lo, b_r)
    x2d = _moe(x1, hn, meta, cnt, moe_w_gate, moe_w_up, moe_w_down, 0)

    w_a = mla_w_a[0]
    split = MLA_Q_LORA + MLA_KV_LORA
    w_a = jnp.concatenate([w_a[:, :split], _pad_rope(w_a[:, split:])], axis=-1).astype(BF16)
    w_qb = _mla_qk_layout(mla_w_qb[0].reshape(MLA_Q_LORA, MLA_HEADS, MLA_QK_DIM))
    w_qb = w_qb.reshape(MLA_Q_LORA, MLA_HEADS * MLA_QK_PAD).astype(BF16)
    w_kvb = mla_w_kvb[0].reshape(MLA_KV_LORA, MLA_HEADS, 2 * MLA_NOPE)
    w_kvb = jnp.concatenate([w_kvb[..., :MLA_NOPE].reshape(MLA_KV_LORA, -1),
                             w_kvb[..., MLA_NOPE:].reshape(MLA_KV_LORA, -1)], axis=-1).astype(BF16)
    q_gain = _mla_qk_layout(mla_q_norm[0].astype(F32) * (MLA_QK_DIM ** -0.5 * LOG2E)).reshape(1, -1)
    k_gain = _mla_qk_layout(mla_k_norm[0].astype(F32)).reshape(1, -1)
    q, k, vt = _mla_proj(x2d, row(attn_norm[1]), w_a, row(mla_q_a_norm[0]), row(mla_kv_a_norm[0]),
                        w_qb, w_kvb, q_gain, k_gain, rope_cos, rope_sin, s)
    qk_shp = (b, s, MLA_HEADS * MLA_QK_PAD)
    o = _flash(q.reshape(qk_shp), k.reshape(qk_shp), vt, MLA_HEADS, MLA_QK_PAD, Q_TILE)
    wr_hi, wr_lo, b_r = _router_weights(moe_w_group[1], moe_b_group[1],
                                        moe_w_expert[1], moe_b_expert[1])
    x1, hn, meta, cnt = _post_attn(o.reshape(t, d), x2d, mla_w_out[0].astype(BF16),
                                   row(ffn_norm[1]), wr_hi, wr_lo, b_r)
    x2d = _moe(x1, hn, meta, cnt, moe_w_gate, moe_w_up, moe_w_down, 1)
    return x2d.reshape(b, s, d)
```

```python
import functools
import math

import jax
import jax.numpy as jnp
from jax import lax
from jax.experimental import pallas as pl
from jax.experimental.pallas import tpu as pltpu

F32 = jnp.float32
BF16 = jnp.bfloat16

D_MODEL = 1024
PACKED = D_MODEL // 2
ROW_WORDS = PACKED + 128
ROPE_THETA = 10000.0
NORM_EPS = 1e-6
LANES = 128
MXU_DIM = 256

DIFF_HEAD_DIM = 64
DIFF_HEADS = 8
DIFF_QK_WIDTH = 1024

MLA_NOPE = 128
MLA_ROPE = 64
MLA_QK_DIM = 192
MLA_QK_PAD = 256
MLA_HEADS = 8
MLA_Q_LORA = 384
MLA_KV_LORA = 256

N_GROUPS = 8
EXPERTS_PER_GROUP = 8
N_EXPERTS = 64
TOP_K = 2
EXPERT_FF = 256

ROW_TILE = 512
Q_TILE = 1024
FLASH_CHUNK = 512
V_ROWS = 144
LOG2E = math.log2(math.e)
EXPERT_TILE = 256
PAIRS_PER_ITER = {2: 8, 4: 4}
PROJ_PARTS = 2
COPY_TILE = 2048
ISSUE_UNROLL = 8
VMEM_LIMIT = 56 * 1024 * 1024


def _cparams(sem):
    return pltpu.CompilerParams(dimension_semantics=sem, vmem_limit_bytes=VMEM_LIMIT)


def _rms(x, gain):
    return x * lax.rsqrt(jnp.mean(x * x, axis=-1, keepdims=True) + NORM_EPS) * gain


def _pack_bf16_pairs(x):
    lo = lax.bitcast_convert_type(x[:, :PACKED].astype(BF16).astype(F32), jnp.uint32)
    hi = lax.bitcast_convert_type(x[:, PACKED:].astype(BF16).astype(F32), jnp.uint32)
    return (lo >> 16) | (hi & jnp.uint32(0xFFFF0000))


def _unpack_bf16_pairs(p):
    lo = lax.bitcast_convert_type(p << 16, F32)
    hi = lax.bitcast_convert_type(p & jnp.uint32(0xFFFF0000), F32)
    return lo, hi


def _store_v_transposed(v, vt_ref, heads, cols=slice(None)):
    rows = v.shape[0]
    ones = jnp.ones((V_ROWS - LANES, rows), BF16)
    for hd in range(heads):
        lo = hd * V_ROWS
        vt_ref[0, 0, lo:lo + LANES, cols] = v[:, hd * LANES:(hd + 1) * LANES].T.astype(BF16)
        vt_ref[0, 0, lo + LANES:lo + V_ROWS, cols] = ones


def _rope128(u, cos, sin):
    return u * cos + pltpu.roll(u, 64, 1) * sin


def _diff_qkv_kernel(x_ref, g_ref, w_ref, qg_ref, kg_ref, cos_ref, sin_ref, seg_ref,
                     q_ref, k_ref, vt_ref):
    tm = x_ref.shape[0]
    seg = seg_ref[...]

    def project(rows):
        h = _rms(x_ref[rows, :], g_ref[...])
        return jnp.dot(h.astype(BF16), w_ref[...], preferred_element_type=F32)

    def heads(rows, qkv):
        cos = cos_ref[rows, :]
        sin = sin_ref[rows, :]
        for off, gain_ref, out_ref in ((0, qg_ref, q_ref), (DIFF_QK_WIDTH, kg_ref, k_ref)):
            for c in range(DIFF_QK_WIDTH // MXU_DIM):
                t = qkv[:, off + c * MXU_DIM: off + (c + 1) * MXU_DIM]
                ss = jnp.dot((t * t).astype(BF16), seg, preferred_element_type=F32)
                tn = t * lax.rsqrt(ss * (1.0 / DIFF_HEAD_DIM) + NORM_EPS) * gain_ref[...]
                for half in range(MXU_DIM // LANES):
                    u = tn[:, half * LANES:(half + 1) * LANES]
                    lo = c * MXU_DIM + half * LANES
                    out_ref[rows, lo:lo + LANES] = _rope128(u, cos, sin).astype(BF16)
        _store_v_transposed(qkv[:, 2 * DIFF_QK_WIDTH:], vt_ref, DIFF_HEADS, rows)

    part = tm // PROJ_PARTS
    rows = [pl.ds(p * part, part) for p in range(PROJ_PARTS)]
    staged = project(rows[0])
    for p in range(PROJ_PARTS):
        upcoming = project(rows[p + 1]) if p + 1 < PROJ_PARTS else None
        heads(rows[p], staged)
        staged = upcoming


def _diff_qkv(x2d, gain, w_in, q_gain, k_gain, cos, sin, seg, seq):
    t = x2d.shape[0]
    tm = ROW_TILE
    pos_tiles = seq // tm
    row = lambda i: (i, 0)
    const = lambda i: (0, 0)
    out = jax.ShapeDtypeStruct((t, DIFF_QK_WIDTH), BF16)
    vt_out = jax.ShapeDtypeStruct((t // seq, pos_tiles, DIFF_HEADS * V_ROWS, tm), BF16)
    vt_spec = pl.BlockSpec((1, 1, DIFF_HEADS * V_ROWS, tm),
                           lambda i: (i // pos_tiles, i % pos_tiles, 0, 0))
    return pl.pallas_call(
        _diff_qkv_kernel,
        grid=(t // tm,),
        in_specs=[
            pl.BlockSpec((tm, D_MODEL), row),
            pl.BlockSpec((1, D_MODEL), const),
            pl.BlockSpec(w_in.shape, const),
            pl.BlockSpec((1, MXU_DIM), const),
            pl.BlockSpec((1, MXU_DIM), const),
            pl.BlockSpec((tm, LANES), lambda i: (i % pos_tiles, 0)),
            pl.BlockSpec((tm, LANES), lambda i: (i % pos_tiles, 0)),
            pl.BlockSpec((MXU_DIM, MXU_DIM), const),
        ],
        out_specs=[pl.BlockSpec((tm, DIFF_QK_WIDTH), row)] * 2 + [vt_spec],
        out_shape=[out, out, vt_out],
        compiler_params=_cparams(("parallel",)),
        name="diff_qkv",
    )(x2d, gain, w_in, q_gain, k_gain, cos, sin, seg)


def _mla_proj_kernel(x_ref, g_ref, wa_ref, qag_ref, kvag_ref, wqb_ref, wkvb_ref,
                     qg_ref, kg_ref, cos_ref, sin_ref, q_ref, k_ref, vt_ref):
    tm = x_ref.shape[0]
    qg = qg_ref[...]
    kg = kg_ref[...]
    inv_d = 1.0 / MLA_QK_DIM

    def low_rank(rows):
        h = _rms(x_ref[rows, :], g_ref[...])
        return jnp.dot(h.astype(BF16), wa_ref[...], preferred_element_type=F32)

    def up_project(a):
        cq = _rms(a[:, :MLA_Q_LORA], qag_ref[...])
        ckv = _rms(a[:, MLA_Q_LORA:MLA_Q_LORA + MLA_KV_LORA], kvag_ref[...])
        q = jnp.dot(cq.astype(BF16), wqb_ref[...], preferred_element_type=F32)
        kv = jnp.dot(ckv.astype(BF16), wkvb_ref[...], preferred_element_type=F32)
        return q, kv, a[:, MLA_Q_LORA + MLA_KV_LORA:]

    def heads(rows, q, kv, kpe):
        cos = cos_ref[rows, :]
        sin = sin_ref[rows, :]
        kpe_ss = jnp.sum(kpe * kpe, axis=-1, keepdims=True)
        kpe_rot = _rope128(kpe * kg[:, MLA_NOPE:], cos, sin)
        for hd in range(MLA_HEADS):
            lo = hd * MLA_QK_PAD
            qh = q[:, lo:lo + MLA_QK_PAD]
            rq = lax.rsqrt(jnp.sum(qh * qh, axis=-1, keepdims=True) * inv_d + NORM_EPS)
            q_ref[rows, lo:lo + MLA_NOPE] = (qh[:, :MLA_NOPE] * rq * qg[:, :MLA_NOPE]).astype(BF16)
            q_ref[rows, lo + MLA_NOPE:lo + MLA_QK_PAD] = _rope128(
                qh[:, MLA_NOPE:] * rq * qg[:, MLA_NOPE:], cos, sin).astype(BF16)
            kn = kv[:, hd * MLA_NOPE:(hd + 1) * MLA_NOPE]
            rk = lax.rsqrt((jnp.sum(kn * kn, axis=-1, keepdims=True) + kpe_ss) * inv_d + NORM_EPS)
            k_ref[rows, lo:lo + MLA_NOPE] = (kn * rk * kg[:, :MLA_NOPE]).astype(BF16)
            k_ref[rows, lo + MLA_NOPE:lo + MLA_QK_PAD] = (kpe_rot * rk).astype(BF16)
        _store_v_transposed(kv[:, MLA_HEADS * MLA_NOPE:], vt_ref, MLA_HEADS, rows)

    part = tm // PROJ_PARTS
    rows = [pl.ds(p * part, part) for p in range(PROJ_PARTS)]
    staged = up_project(low_rank(rows[0]))
    for p in range(PROJ_PARTS):
        upcoming = up_project(low_rank(rows[p + 1])) if p + 1 < PROJ_PARTS else None
        heads(rows[p], *staged)
        staged = upcoming


def _mla_proj(x2d, gain, w_a, qa_gain, kva_gain, w_qb, w_kvb, q_gain, k_gain, cos, sin, seq):
    t = x2d.shape[0]
    tm = ROW_TILE
    pos_tiles = seq // tm
    row = lambda i: (i, 0)
    const = lambda i: (0, 0)
    qk_out = jax.ShapeDtypeStruct((t, MLA_HEADS * MLA_QK_PAD), BF16)
    vt_out = jax.ShapeDtypeStruct((t // seq, pos_tiles, MLA_HEADS * V_ROWS, tm), BF16)
    return pl.pallas_call(
        _mla_proj_kernel,
        grid=(t // tm,),
        in_specs=[
            pl.BlockSpec((tm, D_MODEL), row),
            pl.BlockSpec((1, D_MODEL), const),
            pl.BlockSpec(w_a.shape, const),
            pl.BlockSpec((1, MLA_Q_LORA), const),
            pl.BlockSpec((1, MLA_KV_LORA), const),
            pl.BlockSpec(w_qb.shape, const),
            pl.BlockSpec(w_kvb.shape, const),
            pl.BlockSpec((1, MLA_QK_PAD), const),
            pl.BlockSpec((1, MLA_QK_PAD), const),
            pl.BlockSpec((tm, LANES), lambda i: (i % pos_tiles, 0)),
            pl.BlockSpec((tm, LANES), lambda i: (i % pos_tiles, 0)),
        ],
        out_specs=[pl.BlockSpec((tm, MLA_HEADS * MLA_QK_PAD), row),
                   pl.BlockSpec((tm, MLA_HEADS * MLA_QK_PAD), row),
                   pl.BlockSpec((1, 1, MLA_HEADS * V_ROWS, tm),
                                lambda i: (i // pos_tiles, i % pos_tiles, 0, 0))],
        out_shape=[qk_out, qk_out, vt_out],
        compiler_params=_cparams(("parallel",)),
        name="mla_proj",
    )(x2d, gain, w_a, qa_gain, kva_gain, w_qb, w_kvb, q_gain, k_gain, cos, sin)


def _flash_kernel(*refs, diff, lambda_init, tq, tk, n_chunks):
    if diff:
        (lq1_ref, lk1_ref, lq2_ref, lk2_ref, subln_ref, q_ref, k_ref, vt_ref, o_ref,
         qt_ref, m_ref, acc_ref, sa_ref, sb_ref) = refs
    else:
        q_ref, k_ref, vt_ref, o_ref, qt_ref, m_ref, acc_ref, sa_ref, sb_ref = refs
    qi = pl.program_id(2)
    ch = FLASH_CHUNK
    copies = 2 if diff else 1
    chunks_per_copy = n_chunks // copies

    q_t = q_ref[0].astype(F32).T
    if diff:
        feat = lax.broadcasted_iota(jnp.int32, q_t.shape, 0)
        first = ((feat // 32) % 2) == 0
        qt_ref[:, :tq] = jnp.where(first, q_t, 0.0).astype(BF16)
        qt_ref[:, tq:] = jnp.where(first, 0.0, q_t).astype(BF16)
    else:
        qt_ref[...] = q_t.astype(BF16)
    m_ref[...] = jnp.full(m_ref.shape, -jnp.inf, F32)
    acc_ref[...] = jnp.zeros(acc_ref.shape, F32)

    def key_tile(j):
        return k_ref[0, pl.ds(pl.multiple_of(j * tk, tk), tk), :]

    def scores(j, s_ref, chunks):
        k = key_tile(j)
        for c in chunks:
            s_ref[c] = jnp.dot(k, qt_ref[:, c * ch:(c + 1) * ch], preferred_element_type=F32)

    def accumulate(j, s_ref, chunks, masked=()):
        vt = vt_ref[0, j]
        for c in chunks:
            s = s_ref[c]
            if c in masked:
                key = lax.broadcasted_iota(jnp.int32, (tk, ch), 0)
                qry = lax.broadcasted_iota(jnp.int32, (tk, ch), 1)
                s = jnp.where(key <= qry, s, -jnp.inf)
            m_prev = m_ref[c]
            m_new = jnp.maximum(m_prev, jnp.max(s, axis=0, keepdims=True))
            alpha = jnp.exp2(m_prev - m_new)
            p = jnp.exp2(s - m_new).astype(BF16)
            acc_ref[c] = alpha * acc_ref[c] + jnp.dot(vt, p, preferred_element_type=F32)
            m_ref[c] = m_new

    every = tuple(range(n_chunks))
    early = tuple(c for c in every if c % chunks_per_copy == 0)
    late = tuple(c for c in every if c % chunks_per_copy == 1)
    scores(0, sa_ref, every)

    def pair(j):
        scores(j + 1, sb_ref, every)
        accumulate(j, sa_ref, every)
        scores(j + 2, sa_ref, every)
        accumulate(j + 1, sb_ref, every)

    def pairs(first_pair, count):
        for r in range(count):
            pair(2 * (first_pair + r))

    per_iter = PAIRS_PER_ITER[n_chunks]

    def main_body(i, carry):
        pairs(per_iter * i, per_iter)
        return carry

    lax.fori_loop(0, qi // per_iter, main_body, 0)
    block = per_iter // 2
    while block >= 1:
        @pl.when(qi % (2 * block) >= block)
        def _(block=block):
            pairs(qi - qi % (2 * block), block)
        block //= 2

    j = 2 * qi
    scores(j + 1, sb_ref, late)
    accumulate(j, sa_ref, every, masked=early)
    accumulate(j + 1, sb_ref, late, masked=late)

    def normalized(c):
        a = acc_ref[c]
        return a[:LANES] * (1.0 / a[LANES:LANES + 1])

    if diff:
        lam = (jnp.exp(jnp.sum(lq1_ref[...] * lk1_ref[...], axis=-1, keepdims=True))
               - jnp.exp(jnp.sum(lq2_ref[...] * lk2_ref[...], axis=-1, keepdims=True))
               + lambda_init)
        for c in range(chunks_per_copy):
            o_t = normalized(c) - lam * normalized(c + chunks_per_copy)
            o_t = o_t * lax.rsqrt(jnp.mean(o_t * o_t, axis=0, keepdims=True) + NORM_EPS)
            o = o_t.T * (subln_ref[...] * (1.0 - lambda_init))
            o_ref[0, c * ch:(c + 1) * ch, :] = o.astype(o_ref.dtype)
    else:
        for c in range(n_chunks):
            o_ref[0, c * ch:(c + 1) * ch, :] = normalized(c).T.astype(o_ref.dtype)


def _flash(q, k, vt, heads, dk, tq, diff_params=None, lambda_init=0.0):
    b, s, _ = q.shape
    tk = ROW_TILE
    diff = diff_params is not None
    n_chunks = (2 if diff else 1) * tq // FLASH_CHUNK
    q_spec = pl.BlockSpec((1, tq, dk), lambda bi, hi, qi: (bi, qi, hi))
    k_spec = pl.BlockSpec((1, s, dk), lambda bi, hi, qi: (bi, 0, hi))
    v_spec = pl.BlockSpec((1, s // tk, V_ROWS, tk), lambda bi, hi, qi: (bi, 0, hi, 0))
    o_spec = pl.BlockSpec((1, tq, LANES), lambda bi, hi, qi: (bi, qi, hi))
    small = lambda shape: pl.BlockSpec(shape, lambda bi, hi, qi: (0, 0))
    in_specs = [q_spec, k_spec, v_spec]
    args = [q, k, vt]
    if diff:
        in_specs = [small((1, DIFF_HEAD_DIM))] * 4 + [small((1, LANES))] + in_specs
        args = list(diff_params) + args
    assert tk == FLASH_CHUNK and tq == 2 * tk
    scratch = [pltpu.VMEM((dk, n_chunks * FLASH_CHUNK), BF16),
               pltpu.VMEM((n_chunks, 1, FLASH_CHUNK), F32),
               pltpu.VMEM((n_chunks, V_ROWS, FLASH_CHUNK), F32),
               pltpu.VMEM((n_chunks, tk, FLASH_CHUNK), F32),
               pltpu.VMEM((n_chunks, tk, FLASH_CHUNK), F32)]
    return pl.pallas_call(
        functools.partial(_flash_kernel, diff=diff, lambda_init=lambda_init, tq=tq, tk=tk,
                          n_chunks=n_chunks),
        grid=(b, heads, s // tq),
        in_specs=in_specs,
        out_specs=o_spec,
        out_shape=jax.ShapeDtypeStruct((b, s, heads * LANES), BF16),
        scratch_shapes=scratch,
        compiler_params=_cparams(("parallel", "parallel", "arbitrary")),
        name="diff_flash" if diff else "mla_flash",
    )(*args)


def _post_attn_kernel(o_ref, x_ref, wo_ref, fg_ref, wrh_ref, wrl_ref, br_ref,
                      x1_ref, hn_ref, meta_ref, cnt_ref, carry_ref):
    i = pl.program_id(0)

    @pl.when(i == 0)
    def _():
        carry_ref[...] = jnp.zeros(carry_ref.shape, F32)

    x1 = x_ref[...] + jnp.dot(o_ref[...], wo_ref[...], preferred_element_type=F32)
    x1_ref[...] = x1
    hn = _rms(x1, fg_ref[...])
    hn_ref[:, :PACKED] = _pack_bf16_pairs(hn)

    hi = hn.astype(BF16)
    lo = (hn - hi.astype(F32)).astype(BF16)
    wrh = wrh_ref[...]
    logits = (jnp.dot(hi, wrh, preferred_element_type=F32)
              + jnp.dot(lo, wrh, preferred_element_type=F32)
              + jnp.dot(hi, wrl_ref[...], preferred_element_type=F32)
              + br_ref[...])
    tm = logits.shape[0]
    lane = lax.broadcasted_iota(jnp.int32, logits.shape, 1)
    lane_f = lane.astype(F32)
    neg = jnp.float32(-jnp.inf)
    big = jnp.float32(1e9)

    is_group = (lane >= N_EXPERTS) & (lane < N_EXPERTS + N_GROUPS)
    gl = jnp.where(is_group, logits, neg)
    gmax = jnp.max(gl, axis=-1, keepdims=True)
    gidx = jnp.min(jnp.where(gl == gmax, lane_f, big), axis=-1, keepdims=True) - N_EXPERTS
    g_w = 1.0 / jnp.sum(jnp.exp(gl - gmax), axis=-1, keepdims=True)

    in_group = (lane < N_EXPERTS) & ((lane // EXPERTS_PER_GROUP).astype(F32) == gidx)
    el = jnp.where(in_group, logits, neg)
    m1 = jnp.max(el, axis=-1, keepdims=True)
    i1 = jnp.min(jnp.where(el == m1, lane_f, big), axis=-1, keepdims=True)
    el2 = jnp.where(lane_f == i1, neg, el)
    m2 = jnp.max(el2, axis=-1, keepdims=True)
    i2 = jnp.min(jnp.where(el2 == m2, lane_f, big), axis=-1, keepdims=True)
    d = jnp.exp(m2 - m1)
    w1 = g_w / (1.0 + d)
    w2 = g_w * d / (1.0 + d)

    first_expert = gidx * EXPERTS_PER_GROUP
    wv = (jnp.where(lane_f == i1 - first_expert, w1, 0.0)
          + jnp.where(lane_f == i2 - first_expert, w2, 0.0))
    hn_ref[:, PACKED:] = lax.bitcast_convert_type(wv, jnp.uint32)

    in_grp = lane_f == gidx + N_EXPERTS
    oh = jnp.where(in_grp, 1.0, 0.0)
    r = lax.broadcasted_iota(jnp.int32, (tm, tm), 0)
    c = lax.broadcasted_iota(jnp.int32, (tm, tm), 1)
    lower = jnp.where(c < r, 1.0, 0.0).astype(BF16)
    prefix = jnp.dot(lower, oh.astype(BF16), preferred_element_type=F32) + carry_ref[...]
    rank = jnp.sum(jnp.where(in_grp, prefix, 0.0), axis=-1, keepdims=True)
    carry = carry_ref[...] + jnp.sum(oh, axis=0, keepdims=True)
    carry_ref[...] = carry
    cnt_ref[...] = jnp.broadcast_to(carry, cnt_ref.shape)
    meta_ref[...] = jnp.where(lane == 0, gidx, jnp.where(lane == 1, rank, 0.0))


def _post_attn(o2d, x2d, w_out, ffn_gain, wr_hi, wr_lo, b_r):
    t = x2d.shape[0]
    tm = ROW_TILE
    row = lambda i: (i, 0)
    const = lambda i: (0, 0)
    return pl.pallas_call(
        _post_attn_kernel,
        grid=(t // tm,),
        in_specs=[
            pl.BlockSpec((tm, D_MODEL), row),
            pl.BlockSpec((tm, D_MODEL), row),
            pl.BlockSpec((D_MODEL, D_MODEL), const),
            pl.BlockSpec((1, D_MODEL), const),
            pl.BlockSpec((D_MODEL, LANES), const),
            pl.BlockSpec((D_MODEL, LANES), const),
            pl.BlockSpec((1, LANES), const),
        ],
        out_specs=[pl.BlockSpec((tm, D_MODEL), row),
                   pl.BlockSpec((tm, ROW_WORDS), row),
                   pl.BlockSpec((tm, LANES), row),
                   pl.BlockSpec((8, LANES), const)],
        out_shape=[jax.ShapeDtypeStruct((t, D_MODEL), F32),
                   jax.ShapeDtypeStruct((t, ROW_WORDS), jnp.uint32),
                   jax.ShapeDtypeStruct((t, LANES), F32),
                   jax.ShapeDtypeStruct((8, LANES), F32)],
        scratch_shapes=[pltpu.VMEM((1, LANES), F32)],
        compiler_params=_cparams(("arbitrary",)),
        name="post_attn_router",
    )(o2d, x2d, w_out, ffn_gain, wr_hi, wr_lo, b_r)


def _row_copy(src, dst, sem):
    return pltpu.make_async_copy(src, dst, sem)


def _dispatch_kernel(dest_ref, zero_tile_ref, hn_ref, xs_ref, zeros_ref, sem, pad_sem):
    tm = hn_ref.shape[0]
    tile = zeros_ref.shape[0]
    step = pl.program_id(0)
    base = step * tm

    @pl.when(step == 0)
    def _():
        zeros_ref[...] = jnp.zeros(zeros_ref.shape, zeros_ref.dtype)

        def tile_copy(tl):
            row0 = pl.multiple_of(tl * tile, tile)
            return _row_copy(zeros_ref, xs_ref.at[pl.ds(row0, tile)], pad_sem)

        def start_tile(tl, carry):
            @pl.when(zero_tile_ref[tl] == 1)
            def _():
                tile_copy(tl).start()
            return carry

        def wait_tile(tl, carry):
            @pl.when(zero_tile_ref[tl] == 1)
            def _():
                tile_copy(tl).wait()
            return carry

        n_tiles = xs_ref.shape[0] // tile
        lax.fori_loop(0, n_tiles, start_tile, 0)
        lax.fori_loop(0, n_tiles, wait_tile, 0)

    def issue(t, carry):
        d = dest_ref[base + t]
        _row_copy(hn_ref.at[pl.ds(t, 1)], xs_ref.at[pl.ds(d, 1)], sem).start()
        return carry

    lax.fori_loop(0, tm, issue, 0, unroll=ISSUE_UNROLL)
    _row_copy(hn_ref, xs_ref.at[pl.ds(0, tm)], sem).wait()


def _dispatch(dest, zero_tile, hn, n_slots):
    t = hn.shape[0]
    tm = COPY_TILE
    grid_spec = pltpu.PrefetchScalarGridSpec(
        num_scalar_prefetch=2,
        grid=(t // tm,),
        in_specs=[pl.BlockSpec((tm, ROW_WORDS), lambda i, *_: (i, 0))],
        out_specs=pl.BlockSpec(memory_space=pl.ANY),
        scratch_shapes=[pltpu.VMEM((EXPERT_TILE, ROW_WORDS), jnp.uint32),
                        pltpu.SemaphoreType.DMA(()),
                        pltpu.SemaphoreType.DMA(())],
    )
    return pl.pallas_call(
        _dispatch_kernel,
        grid_spec=grid_spec,
        out_shape=jax.ShapeDtypeStruct((n_slots, ROW_WORDS), jnp.uint32),
        compiler_params=_cparams(("arbitrary",)),
        name="moe_dispatch",
    )(dest, zero_tile, hn)


def _expert_kernel(tg_ref, tnext_ref, tv_ref, xs_ref, wg_hbm, wu_hbm, wd_hbm, ys_ref,
                   stage_g, stage_u, stage_d, wg_bf, wu_bf, wd_bf, sems, *, layer):
    i = pl.program_id(0)
    grp = tg_ref[i]
    first = i == 0
    changed = first | (grp != tg_ref[jnp.maximum(i - 1, 0)])
    ff = EXPERT_FF

    def weight_copies(g):
        e0 = g * EXPERTS_PER_GROUP
        return (_row_copy(wg_hbm.at[layer, pl.ds(e0, EXPERTS_PER_GROUP)], stage_g, sems.at[0]),
                _row_copy(wu_hbm.at[layer, pl.ds(e0, EXPERTS_PER_GROUP)], stage_u, sems.at[1]),
                _row_copy(wd_hbm.at[layer, pl.ds(e0, EXPERTS_PER_GROUP)], stage_d, sems.at[2]))

    @pl.when(first)
    def _():
        for cp in weight_copies(grp):
            cp.start()

    @pl.when(changed)
    def _():
        for cp in weight_copies(grp):
            cp.wait()
        for e in range(EXPERTS_PER_GROUP):
            wg_bf[:, e * ff:(e + 1) * ff] = stage_g[e].astype(BF16)
            wu_bf[:, e * ff:(e + 1) * ff] = stage_u[e].astype(BF16)
            wd_bf[e * ff:(e + 1) * ff, :] = stage_d[e].astype(BF16)
        nxt = tnext_ref[i]

        @pl.when(nxt != grp)
        def _():
            for cp in weight_copies(nxt):
                cp.start()

    @pl.when(tv_ref[i] == 1)
    def _():
        x_lo, x_hi = _unpack_bf16_pairs(xs_ref[:, :PACKED])
        x = jnp.concatenate([x_lo.astype(BF16), x_hi.astype(BF16)], axis=-1)
        wv = lax.bitcast_convert_type(xs_ref[:, PACKED:], F32)
        g = jnp.dot(x, wg_bf[...], preferred_element_type=F32)
        u = jnp.dot(x, wu_bf[...], preferred_element_type=F32)
        hid = g * jax.nn.sigmoid(g) * u
        hid = jnp.concatenate(
            [(hid[:, e * ff:(e + 1) * ff] * wv[:, e:e + 1]).astype(BF16)
             for e in range(EXPERTS_PER_GROUP)], axis=-1)
        y = jnp.dot(hid, wd_bf[...], preferred_element_type=F32)
        ys_ref[...] = _pack_bf16_pairs(y)

    @pl.when(tv_ref[i] == 0)
    def _():
        ys_ref[...] = jnp.zeros(ys_ref.shape, ys_ref.dtype)


def _experts(tile_group, tile_next, tile_valid, xs, w_gate, w_up, w_down, layer):
    p = xs.shape[0]
    tile = EXPERT_TILE
    group_ff = EXPERTS_PER_GROUP * EXPERT_FF
    grid_spec = pltpu.PrefetchScalarGridSpec(
        num_scalar_prefetch=3,
        grid=(p // tile,),
        in_specs=[
            pl.BlockSpec((tile, ROW_WORDS), lambda i, *_: (i, 0)),
            pl.BlockSpec(memory_space=pl.ANY),
            pl.BlockSpec(memory_space=pl.ANY),
            pl.BlockSpec(memory_space=pl.ANY),
        ],
        out_specs=pl.BlockSpec((tile, PACKED), lambda i, *_: (i, 0)),
        scratch_shapes=[pltpu.VMEM((EXPERTS_PER_GROUP, D_MODEL, EXPERT_FF), F32),
                        pltpu.VMEM((EXPERTS_PER_GROUP, D_MODEL, EXPERT_FF), F32),
                        pltpu.VMEM((EXPERTS_PER_GROUP, EXPERT_FF, D_MODEL), F32),
                        pltpu.VMEM((D_MODEL, group_ff), BF16),
                        pltpu.VMEM((D_MODEL, group_ff), BF16),
                        pltpu.VMEM((group_ff, D_MODEL), BF16),
                        pltpu.SemaphoreType.DMA((3,))],
    )
    return pl.pallas_call(
        functools.partial(_expert_kernel, layer=layer),
        grid_spec=grid_spec,
        out_shape=jax.ShapeDtypeStruct((p, PACKED), jnp.uint32),
        compiler_params=_cparams(("arbitrary",)),
        name="moe_experts",
    )(tile_group, tile_next, tile_valid, xs, w_gate, w_up, w_down)


def _combine_kernel(dest_ref, x1_ref, ys_ref, out_ref, buf_ref, sem):
    tm = x1_ref.shape[0]
    base = pl.program_id(0) * tm

    def issue(t, carry):
        d = dest_ref[base + t]
        _row_copy(ys_ref.at[pl.ds(d, 1)], buf_ref.at[pl.ds(t, 1)], sem).start()
        return carry

    lax.fori_loop(0, tm, issue, 0, unroll=ISSUE_UNROLL)
    _row_copy(ys_ref.at[pl.ds(0, tm)], buf_ref, sem).wait()

    lo, hi = _unpack_bf16_pairs(buf_ref[...])
    out_ref[:, :PACKED] = x1_ref[:, :PACKED] + lo
    out_ref[:, PACKED:] = x1_ref[:, PACKED:] + hi


def _combine(dest, x1, ys):
    t = x1.shape[0]
    tm = COPY_TILE
    grid_spec = pltpu.PrefetchScalarGridSpec(
        num_scalar_prefetch=1,
        grid=(t // tm,),
        in_specs=[pl.BlockSpec((tm, D_MODEL), lambda i, dest: (i, 0)),
                  pl.BlockSpec(memory_space=pl.ANY)],
        out_specs=pl.BlockSpec((tm, D_MODEL), lambda i, dest: (i, 0)),
        scratch_shapes=[pltpu.VMEM((tm, PACKED), jnp.uint32),
                        pltpu.SemaphoreType.DMA(())],
    )
    return pl.pallas_call(
        _combine_kernel,
        grid_spec=grid_spec,
        out_shape=jax.ShapeDtypeStruct((t, D_MODEL), F32),
        compiler_params=_cparams(("arbitrary",)),
        name="moe_combine",
    )(dest, x1, ys)


def _rope_tables(seq):
    inv_freq = 1.0 / (ROPE_THETA ** (jnp.arange(0, DIFF_HEAD_DIM, 2, dtype=F32) / DIFF_HEAD_DIM))
    ang = jnp.arange(seq, dtype=F32)[:, None] * jnp.tile(inv_freq, 4)[None, :]
    sign = jnp.where(jnp.arange(LANES) < LANES // 2, -1.0, 1.0).astype(F32)
    return jnp.cos(ang), jnp.sin(ang) * sign[None, :]


def _diff_head_layout(w):
    lead = w.shape[:-1]
    w = w.reshape(*lead, DIFF_HEADS, 2, 2, DIFF_HEAD_DIM // 2)
    w = jnp.swapaxes(w, -3, -2)
    return w.reshape(*lead, DIFF_QK_WIDTH)


def _diff_gain_layout(g, scale):
    g = (g.astype(F32) * scale).reshape(2, DIFF_HEAD_DIM // 2)
    g = jnp.broadcast_to(g[:, None, :], (2, 2, DIFF_HEAD_DIM // 2))
    return jnp.tile(g.reshape(1, LANES), (1, MXU_DIM // LANES))


def _pad_rope(w):
    half = MLA_ROPE // 2
    z = jnp.zeros(w.shape[:-1] + (half,), w.dtype)
    return jnp.concatenate([w[..., :half], z, w[..., half:], z], axis=-1)


def _mla_qk_layout(w):
    return jnp.concatenate([w[..., :MLA_NOPE], _pad_rope(w[..., MLA_NOPE:])], axis=-1)


def _segment_matrix():
    lane = jnp.arange(MXU_DIM)
    key = (lane // LANES) * 2 + (lane // 32) % 2
    return (key[:, None] == key[None, :]).astype(BF16)


def _moe(x1, hn, meta, cnt, w_gate, w_up, w_down, layer):
    t = x1.shape[0]
    tile = EXPERT_TILE
    n_slots = t + N_GROUPS * tile
    n_tiles = n_slots // tile
    groups = jnp.arange(N_GROUPS, dtype=jnp.int32)
    counts = cnt[0, N_EXPERTS:N_EXPERTS + N_GROUPS].astype(jnp.int32)
    padded = ((counts + tile - 1) // tile) * tile
    ends = jnp.cumsum(padded)
    offsets = ends - padded
    tile_start = jnp.arange(n_tiles, dtype=jnp.int32) * tile
    tile_valid = (tile_start < ends[-1]).astype(jnp.int32)
    tile_group = jnp.sum(tile_start[:, None] >= ends[None, :], axis=1).astype(jnp.int32)
    last_group = jnp.max(jnp.where(counts > 0, groups, 0))
    tile_group = jnp.where(tile_valid == 1, tile_group, last_group)
    later = (groups[None, :] > groups[:, None]) & (counts > 0)[None, :]
    next_of = jnp.min(jnp.where(later, groups[None, :], N_GROUPS), axis=1)
    next_of = jnp.where(next_of == N_GROUPS, groups, next_of)
    onehot_tile = tile_group[:, None] == groups[None, :]
    tile_next = jnp.sum(jnp.where(onehot_tile, next_of[None, :], 0), axis=1).astype(jnp.int32)
    partly_filled = jnp.any((tile_start[:, None] + tile == ends[None, :])
                            & (padded > counts)[None, :], axis=1)
    zero_tile = jnp.where(tile_valid == 1, partly_filled, True).astype(jnp.int32)
    grp = meta[:, 0].astype(jnp.int32)
    rank = meta[:, 1].astype(jnp.int32)
    dest = rank + jnp.sum(jnp.where(grp[:, None] == groups[None, :], offsets[None, :], 0), axis=-1)
    xs = _dispatch(dest, zero_tile, hn, n_slots)
    ys = _experts(tile_group, tile_next, tile_valid, xs, w_gate, w_up, w_down, layer)
    return _combine(dest, x1, ys)


def _router_weights(w_group, b_group, w_expert, b_expert):
    pad = LANES - N_EXPERTS - N_GROUPS
    w = jnp.concatenate([w_expert, w_group, jnp.zeros((D_MODEL, pad), F32)], axis=-1)
    b = jnp.concatenate([b_expert, b_group, jnp.zeros((pad,), F32)]).reshape(1, LANES)
    w_hi = w.astype(BF16)
    w_lo = (w - w_hi.astype(F32)).astype(BF16)
    return w_hi, w_lo, b


def kernel(x, attn_norm, ffn_norm, diff_w_in, diff_q_norm, diff_k_norm, diff_lambda_q1, diff_lambda_k1, diff_lambda_q2, diff_lambda_k2, diff_subln, diff_w_out, mla_w_a, mla_q_a_norm, mla_kv_a_norm, mla_w_qb, mla_w_kvb, mla_q_norm, mla_k_norm, mla_w_out, moe_w_group, moe_b_group, moe_w_expert, moe_b_expert, moe_w_gate, moe_w_up, moe_w_down):
    b, s, d = x.shape
    assert d == D_MODEL and s % Q_TILE == 0 and s % ROW_TILE == 0 and (b * s) % COPY_TILE == 0
    t = b * s
    rope_cos, rope_sin = _rope_tables(s)
    row = lambda v: v.astype(F32).reshape(1, -1)
    x2d = x.reshape(t, d)

    lambda_init = 0.8 - 0.6 * math.exp(-0.3 * 0)
    w_in = diff_w_in[0]
    w_in = jnp.concatenate([_diff_head_layout(w_in[:, :DIFF_QK_WIDTH]),
                            _diff_head_layout(w_in[:, DIFF_QK_WIDTH:2 * DIFF_QK_WIDTH]),
                            w_in[:, 2 * DIFF_QK_WIDTH:]], axis=-1).astype(BF16)
    q, k, vt = _diff_qkv(x2d, row(attn_norm[0]), w_in,
                         _diff_gain_layout(diff_q_norm[0], DIFF_HEAD_DIM ** -0.5 * LOG2E),
                         _diff_gain_layout(diff_k_norm[0], 1.0),
                         rope_cos, rope_sin, _segment_matrix(), s)
    shp = (b, s, DIFF_QK_WIDTH)
    o = _flash(q.reshape(shp), k.reshape(shp), vt, DIFF_HEADS, LANES, Q_TILE,
               diff_params=(row(diff_lambda_q1[0]), row(diff_lambda_k1[0]),
                            row(diff_lambda_q2[0]), row(diff_lambda_k2[0]),
                            row(diff_subln[0])),
               lambda_init=lambda_init)
    wr_hi, wr_lo, b_r = _router_weights(moe_w_group[0], moe_b_group[0],
                                        moe_w_expert[0], moe_b_expert[0])
    x1, hn, meta, cnt = _post_attn(o.reshape(t, d), x2d, diff_w_out[0].astype(BF16),
                                   row(ffn_norm[0]), wr_hi, wr_lo, b_r)
    x2d = _moe(x1, hn, meta, cnt, moe_w_gate, moe_w_up, moe_w_down, 0)

    w_a = mla_w_a[0]
    split = MLA_Q_LORA + MLA_KV_LORA
    w_a = jnp.concatenate([w_a[:, :split], _pad_rope(w_a[:, split:])], axis=-1).astype(BF16)
    w_qb = _mla_qk_layout(mla_w_qb[0].reshape(MLA_Q_LORA, MLA_HEADS, MLA_QK_DIM))
    w_qb = w_qb.reshape(MLA_Q_LORA, MLA_HEADS * MLA_QK_PAD).astype(BF16)
    w_kvb = mla_w_kvb[0].reshape(MLA_KV_LORA, MLA_HEADS, 2 * MLA_NOPE)
    w_kvb = jnp.concatenate([w_kvb[..., :MLA_NOPE].reshape(MLA_KV_LORA, -1),
                             w_kvb[..., MLA_NOPE:].reshape(MLA_KV_LORA, -1)], axis=-1).astype(BF16)
    q_gain = _mla_qk_layout(mla_q_norm[0].astype(F32) * (MLA_QK_DIM ** -0.5 * LOG2E)).reshape(1, -1)
    k_gain = _mla_qk_layout(mla_k_norm[0].astype(F32)).reshape(1, -1)
    q, k, vt = _mla_proj(x2d, row(attn_norm[1]), w_a, row(mla_q_a_norm[0]), row(mla_kv_a_norm[0]),
                        w_qb, w_kvb, q_gain, k_gain, rope_cos, rope_sin, s)
    qk_shp = (b, s, MLA_HEADS * MLA_QK_PAD)
    o = _flash(q.reshape(qk_shp), k.reshape(qk_shp), vt, MLA_HEADS, MLA_QK_PAD, Q_TILE)
    wr_hi, wr_lo, b_r = _router_weights(moe_w_group[1], moe_b_group[1],
                                        moe_w_expert[1], moe_b_expert[1])
    x1, hn, meta, cnt = _post_attn(o.reshape(t, d), x2d, mla_w_out[0].astype(BF16),
                                   row(ffn_norm[1]), wr_hi, wr_lo, b_r)
    x2d = _moe(x1, hn, meta, cnt, moe_w_gate, moe_w_up, moe_w_down, 1)
    return x2d.reshape(b, s, d)
```

```python
import functools
import math

import jax
import jax.numpy as jnp
from jax import lax
from jax.experimental import pallas as pl
from jax.experimental.pallas import tpu as pltpu

F32 = jnp.float32
BF16 = jnp.bfloat16

D_MODEL = 1024
PACKED = D_MODEL // 2
ROW_WORDS = PACKED + 128
ROPE_THETA = 10000.0
NORM_EPS = 1e-6
LANES = 128
MXU_DIM = 256

DIFF_HEAD_DIM = 64
DIFF_HEADS = 8
DIFF_QK_WIDTH = 1024

MLA_NOPE = 128
MLA_ROPE = 64
MLA_QK_DIM = 192
MLA_QK_PAD = 256
MLA_HEADS = 8
MLA_Q_LORA = 384
MLA_KV_LORA = 256

N_GROUPS = 8
EXPERTS_PER_GROUP = 8
N_EXPERTS = 64
TOP_K = 2
EXPERT_FF = 256

ROW_TILE = 512
Q_TILE = 1024
FLASH_CHUNK = 512
V_ROWS = 144
LOG2E = math.log2(math.e)
EXPERT_TILE = 256
PAIRS_PER_ITER = {2: 8, 4: 4}
PROJ_PARTS = 2
COPY_TILE = 2048
DISPATCH_TILE = 8192
ISSUE_UNROLL = 8
VMEM_LIMIT = 56 * 1024 * 1024


def _cparams(sem):
    return pltpu.CompilerParams(dimension_semantics=sem, vmem_limit_bytes=VMEM_LIMIT)


def _rms(x, gain):
    return x * lax.rsqrt(jnp.mean(x * x, axis=-1, keepdims=True) + NORM_EPS) * gain


def _pack_bf16_pairs(x):
    lo = lax.bitcast_convert_type(x[:, :PACKED].astype(BF16).astype(F32), jnp.uint32)
    hi = lax.bitcast_convert_type(x[:, PACKED:].astype(BF16).astype(F32), jnp.uint32)
    return (lo >> 16) | (hi & jnp.uint32(0xFFFF0000))


def _unpack_bf16_pairs(p):
    lo = lax.bitcast_convert_type(p << 16, F32)
    hi = lax.bitcast_convert_type(p & jnp.uint32(0xFFFF0000), F32)
    return lo, hi


def _store_v_transposed(v, vt_ref, heads, cols=slice(None)):
    rows = v.shape[0]
    ones = jnp.ones((V_ROWS - LANES, rows), BF16)
    for hd in range(heads):
        lo = hd * V_ROWS
        vt_ref[0, 0, lo:lo + LANES, cols] = v[:, hd * LANES:(hd + 1) * LANES].T.astype(BF16)
        vt_ref[0, 0, lo + LANES:lo + V_ROWS, cols] = ones


def _rope128(u, cos, sin):
    return u * cos + pltpu.roll(u, 64, 1) * sin


def _diff_qkv_kernel(x_ref, g_ref, w_ref, qg_ref, kg_ref, cos_ref, sin_ref, seg_ref,
                     q_ref, k_ref, vt_ref):
    tm = x_ref.shape[0]
    seg = seg_ref[...]

    def project(rows):
        h = _rms(x_ref[rows, :], g_ref[...])
        return jnp.dot(h.astype(BF16), w_ref[...], preferred_element_type=F32)

    def heads(rows, qkv):
        cos = cos_ref[rows, :]
        sin = sin_ref[rows, :]
        for off, gain_ref, out_ref in ((0, qg_ref, q_ref), (DIFF_QK_WIDTH, kg_ref, k_ref)):
            for c in range(DIFF_QK_WIDTH // MXU_DIM):
                t = qkv[:, off + c * MXU_DIM: off + (c + 1) * MXU_DIM]
                ss = jnp.dot((t * t).astype(BF16), seg, preferred_element_type=F32)
                tn = t * lax.rsqrt(ss * (1.0 / DIFF_HEAD_DIM) + NORM_EPS) * gain_ref[...]
                for half in range(MXU_DIM // LANES):
                    u = tn[:, half * LANES:(half + 1) * LANES]
                    lo = c * MXU_DIM + half * LANES
                    out_ref[rows, lo:lo + LANES] = _rope128(u, cos, sin).astype(BF16)
        _store_v_transposed(qkv[:, 2 * DIFF_QK_WIDTH:], vt_ref, DIFF_HEADS, rows)

    part = tm // PROJ_PARTS
    rows = [pl.ds(p * part, part) for p in range(PROJ_PARTS)]
    staged = project(rows[0])
    for p in range(PROJ_PARTS):
        upcoming = project(rows[p + 1]) if p + 1 < PROJ_PARTS else None
        heads(rows[p], staged)
        staged = upcoming


def _diff_qkv(x2d, gain, w_in, q_gain, k_gain, cos, sin, seg, seq):
    t = x2d.shape[0]
    tm = ROW_TILE
    pos_tiles = seq // tm
    row = lambda i: (i, 0)
    const = lambda i: (0, 0)
    out = jax.ShapeDtypeStruct((t, DIFF_QK_WIDTH), BF16)
    vt_out = jax.ShapeDtypeStruct((t // seq, pos_tiles, DIFF_HEADS * V_ROWS, tm), BF16)
    vt_spec = pl.BlockSpec((1, 1, DIFF_HEADS * V_ROWS, tm),
                           lambda i: (i // pos_tiles, i % pos_tiles, 0, 0))
    return pl.pallas_call(
        _diff_qkv_kernel,
        grid=(t // tm,),
        in_specs=[
            pl.BlockSpec((tm, D_MODEL), row),
            pl.BlockSpec((1, D_MODEL), const),
            pl.BlockSpec(w_in.shape, const),
            pl.BlockSpec((1, MXU_DIM), const),
            pl.BlockSpec((1, MXU_DIM), const),
            pl.BlockSpec((tm, LANES), lambda i: (i % pos_tiles, 0)),
            pl.BlockSpec((tm, LANES), lambda i: (i % pos_tiles, 0)),
            pl.BlockSpec((MXU_DIM, MXU_DIM), const),
        ],
        out_specs=[pl.BlockSpec((tm, DIFF_QK_WIDTH), row)] * 2 + [vt_spec],
        out_shape=[out, out, vt_out],
        compiler_params=_cparams(("parallel",)),
        name="diff_qkv",
    )(x2d, gain, w_in, q_gain, k_gain, cos, sin, seg)


def _mla_proj_kernel(x_ref, g_ref, wa_ref, qag_ref, kvag_ref, wqb_ref, wkvb_ref,
                     qg_ref, kg_ref, cos_ref, sin_ref, q_ref, k_ref, vt_ref):
    tm = x_ref.shape[0]
    qg = qg_ref[...]
    kg = kg_ref[...]
    inv_d = 1.0 / MLA_QK_DIM

    def low_rank(rows):
        h = _rms(x_ref[rows, :], g_ref[...])
        return jnp.dot(h.astype(BF16), wa_ref[...], preferred_element_type=F32)

    def up_project(a):
        cq = _rms(a[:, :MLA_Q_LORA], qag_ref[...])
        ckv = _rms(a[:, MLA_Q_LORA:MLA_Q_LORA + MLA_KV_LORA], kvag_ref[...])
        q = jnp.dot(cq.astype(BF16), wqb_ref[...], preferred_element_type=F32)
        kv = jnp.dot(ckv.astype(BF16), wkvb_ref[...], preferred_element_type=F32)
        return q, kv, a[:, MLA_Q_LORA + MLA_KV_LORA:]

    def heads(rows, q, kv, kpe):
        cos = cos_ref[rows, :]
        sin = sin_ref[rows, :]
        kpe_ss = jnp.sum(kpe * kpe, axis=-1, keepdims=True)
        kpe_rot = _rope128(kpe * kg[:, MLA_NOPE:], cos, sin)
        for hd in range(MLA_HEADS):
            lo = hd * MLA_QK_PAD
            qh = q[:, lo:lo + MLA_QK_PAD]
            rq = lax.rsqrt(jnp.sum(qh * qh, axis=-1, keepdims=True) * inv_d + NORM_EPS)
            q_ref[rows, lo:lo + MLA_NOPE] = (qh[:, :MLA_NOPE] * rq * qg[:, :MLA_NOPE]).astype(BF16)
            q_ref[rows, lo + MLA_NOPE:lo + MLA_QK_PAD] = _rope128(
                qh[:, MLA_NOPE:] * rq * qg[:, MLA_NOPE:], cos, sin).astype(BF16)
            kn = kv[:, hd * MLA_NOPE:(hd + 1) * MLA_NOPE]
            rk = lax.rsqrt((jnp.sum(kn * kn, axis=-1, keepdims=True) + kpe_ss) * inv_d + NORM_EPS)
            k_ref[rows, lo:lo + MLA_NOPE] = (kn * rk * kg[:, :MLA_NOPE]).astype(BF16)
            k_ref[rows, lo + MLA_NOPE:lo + MLA_QK_PAD] = (kpe_rot * rk).astype(BF16)
        _store_v_transposed(kv[:, MLA_HEADS * MLA_NOPE:], vt_ref, MLA_HEADS, rows)

    part = tm // PROJ_PARTS
    rows = [pl.ds(p * part, part) for p in range(PROJ_PARTS)]
    staged = up_project(low_rank(rows[0]))
    for p in range(PROJ_PARTS):
        upcoming = up_project(low_rank(rows[p + 1])) if p + 1 < PROJ_PARTS else None
        heads(rows[p], *staged)
        staged = upcoming


def _mla_proj(x2d, gain, w_a, qa_gain, kva_gain, w_qb, w_kvb, q_gain, k_gain, cos, sin, seq):
    t = x2d.shape[0]
    tm = ROW_TILE
    pos_tiles = seq // tm
    row = lambda i: (i, 0)
    const = lambda i: (0, 0)
    qk_out = jax.ShapeDtypeStruct((t, MLA_HEADS * MLA_QK_PAD), BF16)
    vt_out = jax.ShapeDtypeStruct((t // seq, pos_tiles, MLA_HEADS * V_ROWS, tm), BF16)
    return pl.pallas_call(
        _mla_proj_kernel,
        grid=(t // tm,),
        in_specs=[
            pl.BlockSpec((tm, D_MODEL), row),
            pl.BlockSpec((1, D_MODEL), const),
            pl.BlockSpec(w_a.shape, const),
            pl.BlockSpec((1, MLA_Q_LORA), const),
            pl.BlockSpec((1, MLA_KV_LORA), const),
            pl.BlockSpec(w_qb.shape, const),
            pl.BlockSpec(w_kvb.shape, const),
            pl.BlockSpec((1, MLA_QK_PAD), const),
            pl.BlockSpec((1, MLA_QK_PAD), const),
            pl.BlockSpec((tm, LANES), lambda i: (i % pos_tiles, 0)),
            pl.BlockSpec((tm, LANES), lambda i: (i % pos_tiles, 0)),
        ],
        out_specs=[pl.BlockSpec((tm, MLA_HEADS * MLA_QK_PAD), row),
                   pl.BlockSpec((tm, MLA_HEADS * MLA_QK_PAD), row),
                   pl.BlockSpec((1, 1, MLA_HEADS * V_ROWS, tm),
                                lambda i: (i // pos_tiles, i % pos_tiles, 0, 0))],
        out_shape=[qk_out, qk_out, vt_out],
        compiler_params=_cparams(("parallel",)),
        name="mla_proj",
    )(x2d, gain, w_a, qa_gain, kva_gain, w_qb, w_kvb, q_gain, k_gain, cos, sin)


def _flash_kernel(*refs, diff, lambda_init, tq, tk, n_chunks):
    if diff:
        (lq1_ref, lk1_ref, lq2_ref, lk2_ref, subln_ref, q_ref, k_ref, vt_ref, o_ref,
         qt_ref, m_ref, acc_ref, sa_ref, sb_ref) = refs
    else:
        q_ref, k_ref, vt_ref, o_ref, qt_ref, m_ref, acc_ref, sa_ref, sb_ref = refs
    qi = pl.program_id(2)
    ch = FLASH_CHUNK
    copies = 2 if diff else 1
    chunks_per_copy = n_chunks // copies

    q_t = q_ref[0].astype(F32).T
    if diff:
        feat = lax.broadcasted_iota(jnp.int32, q_t.shape, 0)
        first = ((feat // 32) % 2) == 0
        qt_ref[:, :tq] = jnp.where(first, q_t, 0.0).astype(BF16)
        qt_ref[:, tq:] = jnp.where(first, 0.0, q_t).astype(BF16)
    else:
        qt_ref[...] = q_t.astype(BF16)
    m_ref[...] = jnp.full(m_ref.shape, -jnp.inf, F32)
    acc_ref[...] = jnp.zeros(acc_ref.shape, F32)

    def key_tile(j):
        return k_ref[0, pl.ds(pl.multiple_of(j * tk, tk), tk), :]

    def scores(j, s_ref, chunks):
        k = key_tile(j)
        for c in chunks:
            s_ref[c] = jnp.dot(k, qt_ref[:, c * ch:(c + 1) * ch], preferred_element_type=F32)

    def accumulate(j, s_ref, chunks, masked=()):
        vt = vt_ref[0, j]
        for c in chunks:
            s = s_ref[c]
            if c in masked:
                key = lax.broadcasted_iota(jnp.int32, (tk, ch), 0)
                qry = lax.broadcasted_iota(jnp.int32, (tk, ch), 1)
                s = jnp.where(key <= qry, s, -jnp.inf)
            m_prev = m_ref[c]
            m_new = jnp.maximum(m_prev, jnp.max(s, axis=0, keepdims=True))
            alpha = jnp.exp2(m_prev - m_new)
            p = jnp.exp2(s - m_new).astype(BF16)
            acc_ref[c] = alpha * acc_ref[c] + jnp.dot(vt, p, preferred_element_type=F32)
            m_ref[c] = m_new

    every = tuple(range(n_chunks))
    early = tuple(c for c in every if c % chunks_per_copy == 0)
    late = tuple(c for c in every if c % chunks_per_copy == 1)
    scores(0, sa_ref, every)

    def pair(j):
        scores(j + 1, sb_ref, every)
        accumulate(j, sa_ref, every)
        scores(j + 2, sa_ref, every)
        accumulate(j + 1, sb_ref, every)

    def pairs(first_pair, count):
        for r in range(count):
            pair(2 * (first_pair + r))

    per_iter = PAIRS_PER_ITER[n_chunks]

    def main_body(i, carry):
        pairs(per_iter * i, per_iter)
        return carry

    lax.fori_loop(0, qi // per_iter, main_body, 0)
    block = per_iter // 2
    while block >= 1:
        @pl.when(qi % (2 * block) >= block)
        def _(block=block):
            pairs(qi - qi % (2 * block), block)
        block //= 2

    j = 2 * qi
    scores(j + 1, sb_ref, late)
    accumulate(j, sa_ref, every, masked=early)
    accumulate(j + 1, sb_ref, late, masked=late)

    def normalized(c):
        a = acc_ref[c]
        return a[:LANES] * (1.0 / a[LANES:LANES + 1])

    if diff:
        lam = (jnp.exp(jnp.sum(lq1_ref[...] * lk1_ref[...], axis=-1, keepdims=True))
               - jnp.exp(jnp.sum(lq2_ref[...] * lk2_ref[...], axis=-1, keepdims=True))
               + lambda_init)
        for c in range(chunks_per_copy):
            o_t = normalized(c) - lam * normalized(c + chunks_per_copy)
            o_t = o_t * lax.rsqrt(jnp.mean(o_t * o_t, axis=0, keepdims=True) + NORM_EPS)
            o = o_t.T * (subln_ref[...] * (1.0 - lambda_init))
            o_ref[0, c * ch:(c + 1) * ch, :] = o.astype(o_ref.dtype)
    else:
        for c in range(n_chunks):
            o_ref[0, c * ch:(c + 1) * ch, :] = normalized(c).T.astype(o_ref.dtype)


def _flash(q, k, vt, heads, dk, tq, diff_params=None, lambda_init=0.0):
    b, s, _ = q.shape
    tk = ROW_TILE
    diff = diff_params is not None
    n_chunks = (2 if diff else 1) * tq // FLASH_CHUNK
    q_spec = pl.BlockSpec((1, tq, dk), lambda bi, hi, qi: (bi, qi, hi))
    k_spec = pl.BlockSpec((1, s, dk), lambda bi, hi, qi: (bi, 0, hi))
    v_spec = pl.BlockSpec((1, s // tk, V_ROWS, tk), lambda bi, hi, qi: (bi, 0, hi, 0))
    o_spec = pl.BlockSpec((1, tq, LANES), lambda bi, hi, qi: (bi, qi, hi))
    small = lambda shape: pl.BlockSpec(shape, lambda bi, hi, qi: (0, 0))
    in_specs = [q_spec, k_spec, v_spec]
    args = [q, k, vt]
    if diff:
        in_specs = [small((1, DIFF_HEAD_DIM))] * 4 + [small((1, LANES))] + in_specs
        args = list(diff_params) + args
    assert tk == FLASH_CHUNK and tq == 2 * tk
    scratch = [pltpu.VMEM((dk, n_chunks * FLASH_CHUNK), BF16),
               pltpu.VMEM((n_chunks, 1, FLASH_CHUNK), F32),
               pltpu.VMEM((n_chunks, V_ROWS, FLASH_CHUNK), F32),
               pltpu.VMEM((n_chunks, tk, FLASH_CHUNK), F32),
               pltpu.VMEM((n_chunks, tk, FLASH_CHUNK), F32)]
    return pl.pallas_call(
        functools.partial(_flash_kernel, diff=diff, lambda_init=lambda_init, tq=tq, tk=tk,
                          n_chunks=n_chunks),
        grid=(b, heads, s // tq),
        in_specs=in_specs,
        out_specs=o_spec,
        out_shape=jax.ShapeDtypeStruct((b, s, heads * LANES), BF16),
        scratch_shapes=scratch,
        compiler_params=_cparams(("parallel", "parallel", "arbitrary")),
        name="diff_flash" if diff else "mla_flash",
    )(*args)


def _post_attn_kernel(o_ref, x_ref, wo_ref, fg_ref, wr_both_ref, wr_hi_ref, br_ref,
                      x1_ref, hn_ref, meta_ref, cnt_ref, carry_ref):
    i = pl.program_id(0)

    @pl.when(i == 0)
    def _():
        carry_ref[...] = jnp.zeros(carry_ref.shape, F32)

    x1 = x_ref[...] + jnp.dot(o_ref[...], wo_ref[...], preferred_element_type=F32)
    x1_ref[...] = x1
    hn = _rms(x1, fg_ref[...])
    hn_ref[:, :PACKED] = _pack_bf16_pairs(hn)

    hi = hn.astype(BF16)
    lo = (hn - hi.astype(F32)).astype(BF16)
    hi_terms = jnp.dot(hi, wr_both_ref[...], preferred_element_type=F32)
    logits = (hi_terms[:, :LANES] + hi_terms[:, LANES:]
              + jnp.dot(lo, wr_hi_ref[...], preferred_element_type=F32)
              + br_ref[...])
    tm = logits.shape[0]
    lane = lax.broadcasted_iota(jnp.int32, logits.shape, 1)
    lane_f = lane.astype(F32)
    neg = jnp.float32(-jnp.inf)
    big = jnp.float32(1e9)

    is_group = (lane >= N_EXPERTS) & (lane < N_EXPERTS + N_GROUPS)
    gl = jnp.where(is_group, logits, neg)
    gmax = jnp.max(gl, axis=-1, keepdims=True)
    gidx = jnp.min(jnp.where(gl == gmax, lane_f, big), axis=-1, keepdims=True) - N_EXPERTS
    g_w = 1.0 / jnp.sum(jnp.exp(gl - gmax), axis=-1, keepdims=True)

    in_group = (lane < N_EXPERTS) & ((lane // EXPERTS_PER_GROUP).astype(F32) == gidx)
    el = jnp.where(in_group, logits, neg)
    m1 = jnp.max(el, axis=-1, keepdims=True)
    i1 = jnp.min(jnp.where(el == m1, lane_f, big), axis=-1, keepdims=True)
    el2 = jnp.where(lane_f == i1, neg, el)
    m2 = jnp.max(el2, axis=-1, keepdims=True)
    i2 = jnp.min(jnp.where(el2 == m2, lane_f, big), axis=-1, keepdims=True)
    d = jnp.exp(m2 - m1)
    w1 = g_w / (1.0 + d)
    w2 = g_w * d / (1.0 + d)

    first_expert = gidx * EXPERTS_PER_GROUP
    wv = (jnp.where(lane_f == i1 - first_expert, w1, 0.0)
          + jnp.where(lane_f == i2 - first_expert, w2, 0.0))
    hn_ref[:, PACKED:] = lax.bitcast_convert_type(wv, jnp.uint32)

    in_grp = lane_f == gidx + N_EXPERTS
    oh = jnp.where(in_grp, 1.0, 0.0)
    r = lax.broadcasted_iota(jnp.int32, (tm, tm), 0)
    c = lax.broadcasted_iota(jnp.int32, (tm, tm), 1)
    lower = jnp.where(c < r, 1.0, 0.0).astype(BF16)
    prefix = jnp.dot(lower, oh.astype(BF16), preferred_element_type=F32) + carry_ref[...]
    rank = jnp.sum(jnp.where(in_grp, prefix, 0.0), axis=-1, keepdims=True)
    carry = carry_ref[...] + jnp.sum(oh, axis=0, keepdims=True)
    carry_ref[...] = carry
    cnt_ref[...] = jnp.broadcast_to(carry, cnt_ref.shape)
    meta_ref[...] = jnp.where(lane == 0, gidx, jnp.where(lane == 1, rank, 0.0))


def _post_attn(o2d, x2d, w_out, ffn_gain, wr_both, wr_hi, b_r):
    t = x2d.shape[0]
    tm = ROW_TILE
    row = lambda i: (i, 0)
    const = lambda i: (0, 0)
    return pl.pallas_call(
        _post_attn_kernel,
        grid=(t // tm,),
        in_specs=[
            pl.BlockSpec((tm, D_MODEL), row),
            pl.BlockSpec((tm, D_MODEL), row),
            pl.BlockSpec((D_MODEL, D_MODEL), const),
            pl.BlockSpec((1, D_MODEL), const),
            pl.BlockSpec((D_MODEL, 2 * LANES), const),
            pl.BlockSpec((D_MODEL, LANES), const),
            pl.BlockSpec((1, LANES), const),
        ],
        out_specs=[pl.BlockSpec((tm, D_MODEL), row),
                   pl.BlockSpec((tm, ROW_WORDS), row),
                   pl.BlockSpec((tm, LANES), row),
                   pl.BlockSpec((8, LANES), const)],
        out_shape=[jax.ShapeDtypeStruct((t, D_MODEL), F32),
                   jax.ShapeDtypeStruct((t, ROW_WORDS), jnp.uint32),
                   jax.ShapeDtypeStruct((t, LANES), F32),
                   jax.ShapeDtypeStruct((8, LANES), F32)],
        scratch_shapes=[pltpu.VMEM((1, LANES), F32)],
        compiler_params=_cparams(("arbitrary",)),
        name="post_attn_router",
    )(o2d, x2d, w_out, ffn_gain, wr_both, wr_hi, b_r)


def _row_copy(src, dst, sem):
    return pltpu.make_async_copy(src, dst, sem)


def _dispatch_kernel(dest_ref, zero_tile_ref, hn_ref, xs_ref, zeros_ref, sem, pad_sem, *, tm):
    tile = zeros_ref.shape[0]
    step = pl.program_id(0)
    base = step * tm

    @pl.when(step == 0)
    def _():
        zeros_ref[...] = jnp.zeros(zeros_ref.shape, zeros_ref.dtype)

        def tile_copy(tl):
            row0 = pl.multiple_of(tl * tile, tile)
            return _row_copy(zeros_ref, xs_ref.at[pl.ds(row0, tile)], pad_sem)

        def start_tile(tl, carry):
            @pl.when(zero_tile_ref[tl] == 1)
            def _():
                tile_copy(tl).start()
            return carry

        def wait_tile(tl, carry):
            @pl.when(zero_tile_ref[tl] == 1)
            def _():
                tile_copy(tl).wait()
            return carry

        n_tiles = xs_ref.shape[0] // tile
        lax.fori_loop(0, n_tiles, start_tile, 0)
        lax.fori_loop(0, n_tiles, wait_tile, 0)

    def issue(t, carry):
        d = dest_ref[base + t]
        _row_copy(hn_ref.at[pl.ds(base + t, 1)], xs_ref.at[pl.ds(d, 1)], sem).start()
        return carry

    lax.fori_loop(0, tm, issue, 0, unroll=ISSUE_UNROLL)
    _row_copy(hn_ref.at[pl.ds(0, tm)], xs_ref.at[pl.ds(0, tm)], sem).wait()


def _dispatch(dest, zero_tile, hn, n_slots):
    t = hn.shape[0]
    tm = DISPATCH_TILE
    grid_spec = pltpu.PrefetchScalarGridSpec(
        num_scalar_prefetch=2,
        grid=(t // tm,),
        in_specs=[pl.BlockSpec(memory_space=pl.ANY)],
        out_specs=pl.BlockSpec(memory_space=pl.ANY),
        scratch_shapes=[pltpu.VMEM((EXPERT_TILE, ROW_WORDS), jnp.uint32),
                        pltpu.SemaphoreType.DMA(()),
                        pltpu.SemaphoreType.DMA(())],
    )
    return pl.pallas_call(
        functools.partial(_dispatch_kernel, tm=tm),
        grid_spec=grid_spec,
        out_shape=jax.ShapeDtypeStruct((n_slots, ROW_WORDS), jnp.uint32),
        compiler_params=_cparams(("arbitrary",)),
        name="moe_dispatch",
    )(dest, zero_tile, hn)


def _expert_kernel(tg_ref, tnext_ref, tv_ref, xs_ref, wg_hbm, wu_hbm, wd_hbm, ys_ref,
                   stage_g, stage_u, stage_d, wg_bf, wu_bf, wd_bf, sems, *, layer):
    i = pl.program_id(0)
    grp = tg_ref[i]
    first = i == 0
    changed = first | (grp != tg_ref[jnp.maximum(i - 1, 0)])
    ff = EXPERT_FF

    def weight_copies(g):
        e0 = g * EXPERTS_PER_GROUP
        return (_row_copy(wg_hbm.at[layer, pl.ds(e0, EXPERTS_PER_GROUP)], stage_g, sems.at[0]),
                _row_copy(wu_hbm.at[layer, pl.ds(e0, EXPERTS_PER_GROUP)], stage_u, sems.at[1]),
                _row_copy(wd_hbm.at[layer, pl.ds(e0, EXPERTS_PER_GROUP)], stage_d, sems.at[2]))

    @pl.when(first)
    def _():
        for cp in weight_copies(grp):
            cp.start()

    @pl.when(changed)
    def _():
        for cp in weight_copies(grp):
            cp.wait()
        for e in range(EXPERTS_PER_GROUP):
            wg_bf[:, e * ff:(e + 1) * ff] = stage_g[e].astype(BF16)
            wu_bf[:, e * ff:(e + 1) * ff] = stage_u[e].astype(BF16)
            wd_bf[e * ff:(e + 1) * ff, :] = stage_d[e].astype(BF16)
        nxt = tnext_ref[i]

        @pl.when(nxt != grp)
        def _():
            for cp in weight_copies(nxt):
                cp.start()

    @pl.when(tv_ref[i] == 1)
    def _():
        x_lo, x_hi = _unpack_bf16_pairs(xs_ref[:, :PACKED])
        x = jnp.concatenate([x_lo.astype(BF16), x_hi.astype(BF16)], axis=-1)
        wv = lax.bitcast_convert_type(xs_ref[:, PACKED:], F32)
        g = jnp.dot(x, wg_bf[...], preferred_element_type=F32)
        u = jnp.dot(x, wu_bf[...], preferred_element_type=F32)
        hid = g * jax.nn.sigmoid(g) * u
        hid = jnp.concatenate(
            [(hid[:, e * ff:(e + 1) * ff] * wv[:, e:e + 1]).astype(BF16)
             for e in range(EXPERTS_PER_GROUP)], axis=-1)
        y = jnp.dot(hid, wd_bf[...], preferred_element_type=F32)
        ys_ref[...] = _pack_bf16_pairs(y)

    @pl.when(tv_ref[i] == 0)
    def _():
        ys_ref[...] = jnp.zeros(ys_ref.shape, ys_ref.dtype)


def _experts(tile_group, tile_next, tile_valid, xs, w_gate, w_up, w_down, layer):
    p = xs.shape[0]
    tile = EXPERT_TILE
    group_ff = EXPERTS_PER_GROUP * EXPERT_FF
    grid_spec = pltpu.PrefetchScalarGridSpec(
        num_scalar_prefetch=3,
        grid=(p // tile,),
        in_specs=[
            pl.BlockSpec((tile, ROW_WORDS), lambda i, *_: (i, 0)),
            pl.BlockSpec(memory_space=pl.ANY),
            pl.BlockSpec(memory_space=pl.ANY),
            pl.BlockSpec(memory_space=pl.ANY),
        ],
        out_specs=pl.BlockSpec((tile, PACKED), lambda i, *_: (i, 0)),
        scratch_shapes=[pltpu.VMEM((EXPERTS_PER_GROUP, D_MODEL, EXPERT_FF), F32),
                        pltpu.VMEM((EXPERTS_PER_GROUP, D_MODEL, EXPERT_FF), F32),
                        pltpu.VMEM((EXPERTS_PER_GROUP, EXPERT_FF, D_MODEL), F32),
                        pltpu.VMEM((D_MODEL, group_ff), BF16),
                        pltpu.VMEM((D_MODEL, group_ff), BF16),
                        pltpu.VMEM((group_ff, D_MODEL), BF16),
                        pltpu.SemaphoreType.DMA((3,))],
    )
    return pl.pallas_call(
        functools.partial(_expert_kernel, layer=layer),
        grid_spec=grid_spec,
        out_shape=jax.ShapeDtypeStruct((p, PACKED), jnp.uint32),
        compiler_params=_cparams(("arbitrary",)),
        name="moe_experts",
    )(tile_group, tile_next, tile_valid, xs, w_gate, w_up, w_down)


def _combine_kernel(dest_ref, x1_ref, ys_ref, out_ref, buf_ref, sems):
    tm = x1_ref.shape[0]
    step = pl.program_id(0)
    slot = step % 2

    def gather(tile_index, into):
        base = tile_index * tm

        def issue(t, carry):
            d = dest_ref[base + t]
            _row_copy(ys_ref.at[pl.ds(d, 1)], buf_ref.at[into, pl.ds(t, 1)], sems.at[into]).start()
            return carry

        lax.fori_loop(0, tm, issue, 0, unroll=ISSUE_UNROLL)

    @pl.when(step == 0)
    def _():
        gather(0, 0)

    @pl.when(step + 1 < pl.num_programs(0))
    def _():
        gather(step + 1, 1 - slot)

    _row_copy(ys_ref.at[pl.ds(0, tm)], buf_ref.at[slot], sems.at[slot]).wait()

    lo, hi = _unpack_bf16_pairs(buf_ref[slot])
    out_ref[:, :PACKED] = x1_ref[:, :PACKED] + lo
    out_ref[:, PACKED:] = x1_ref[:, PACKED:] + hi


def _combine(dest, x1, ys):
    t = x1.shape[0]
    tm = COPY_TILE
    grid_spec = pltpu.PrefetchScalarGridSpec(
        num_scalar_prefetch=1,
        grid=(t // tm,),
        in_specs=[pl.BlockSpec((tm, D_MODEL), lambda i, dest: (i, 0)),
                  pl.BlockSpec(memory_space=pl.ANY)],
        out_specs=pl.BlockSpec((tm, D_MODEL), lambda i, dest: (i, 0)),
        scratch_shapes=[pltpu.VMEM((2, tm, PACKED), jnp.uint32),
                        pltpu.SemaphoreType.DMA((2,))],
    )
    return pl.pallas_call(
        _combine_kernel,
        grid_spec=grid_spec,
        out_shape=jax.ShapeDtypeStruct((t, D_MODEL), F32),
        compiler_params=_cparams(("arbitrary",)),
        name="moe_combine",
    )(dest, x1, ys)


def _rope_tables(seq):
    inv_freq = 1.0 / (ROPE_THETA ** (jnp.arange(0, DIFF_HEAD_DIM, 2, dtype=F32) / DIFF_HEAD_DIM))
    ang = jnp.arange(seq, dtype=F32)[:, None] * jnp.tile(inv_freq, 4)[None, :]
    sign = jnp.where(jnp.arange(LANES) < LANES // 2, -1.0, 1.0).astype(F32)
    return jnp.cos(ang), jnp.sin(ang) * sign[None, :]


def _diff_head_layout(w):
    lead = w.shape[:-1]
    w = w.reshape(*lead, DIFF_HEADS, 2, 2, DIFF_HEAD_DIM // 2)
    w = jnp.swapaxes(w, -3, -2)
    return w.reshape(*lead, DIFF_QK_WIDTH)


def _diff_gain_layout(g, scale):
    g = (g.astype(F32) * scale).reshape(2, DIFF_HEAD_DIM // 2)
    g = jnp.broadcast_to(g[:, None, :], (2, 2, DIFF_HEAD_DIM // 2))
    return jnp.tile(g.reshape(1, LANES), (1, MXU_DIM // LANES))


def _pad_rope(w):
    half = MLA_ROPE // 2
    z = jnp.zeros(w.shape[:-1] + (half,), w.dtype)
    return jnp.concatenate([w[..., :half], z, w[..., half:], z], axis=-1)


def _mla_qk_layout(w):
    return jnp.concatenate([w[..., :MLA_NOPE], _pad_rope(w[..., MLA_NOPE:])], axis=-1)


def _segment_matrix():
    lane = jnp.arange(MXU_DIM)
    key = (lane // LANES) * 2 + (lane // 32) % 2
    return (key[:, None] == key[None, :]).astype(BF16)


def _moe(x1, hn, meta, cnt, w_gate, w_up, w_down, layer):
    t = x1.shape[0]
    tile = EXPERT_TILE
    n_slots = t + N_GROUPS * tile
    n_tiles = n_slots // tile
    groups = jnp.arange(N_GROUPS, dtype=jnp.int32)
    counts = cnt[0, N_EXPERTS:N_EXPERTS + N_GROUPS].astype(jnp.int32)
    padded = ((counts + tile - 1) // tile) * tile
    ends = jnp.cumsum(padded)
    offsets = ends - padded
    tile_start = jnp.arange(n_tiles, dtype=jnp.int32) * tile
    tile_valid = (tile_start < ends[-1]).astype(jnp.int32)
    tile_group = jnp.sum(tile_start[:, None] >= ends[None, :], axis=1).astype(jnp.int32)
    last_group = jnp.max(jnp.where(counts > 0, groups, 0))
    tile_group = jnp.where(tile_valid == 1, tile_group, last_group)
    later = (groups[None, :] > groups[:, None]) & (counts > 0)[None, :]
    next_of = jnp.min(jnp.where(later, groups[None, :], N_GROUPS), axis=1)
    next_of = jnp.where(next_of == N_GROUPS, groups, next_of)
    onehot_tile = tile_group[:, None] == groups[None, :]
    tile_next = jnp.sum(jnp.where(onehot_tile, next_of[None, :], 0), axis=1).astype(jnp.int32)
    partly_filled = jnp.any((tile_start[:, None] + tile == ends[None, :])
                            & (padded > counts)[None, :], axis=1)
    zero_tile = jnp.where(tile_valid == 1, partly_filled, True).astype(jnp.int32)
    grp = meta[:, 0].astype(jnp.int32)
    rank = meta[:, 1].astype(jnp.int32)
    dest = rank + jnp.sum(jnp.where(grp[:, None] == groups[None, :], offsets[None, :], 0), axis=-1)
    xs = _dispatch(dest, zero_tile, hn, n_slots)
    ys = _experts(tile_group, tile_next, tile_valid, xs, w_gate, w_up, w_down, layer)
    return _combine(dest, x1, ys)


def _router_weights(w_group, b_group, w_expert, b_expert):
    pad = LANES - N_EXPERTS - N_GROUPS
    w = jnp.concatenate([w_expert, w_group, jnp.zeros((D_MODEL, pad), F32)], axis=-1)
    b = jnp.concatenate([b_expert, b_group, jnp.zeros((pad,), F32)]).reshape(1, LANES)
    w_hi = w.astype(BF16)
    w_lo = (w - w_hi.astype(F32)).astype(BF16)
    return jnp.concatenate([w_hi, w_lo], axis=-1), w_hi, b


def kernel(x, attn_norm, ffn_norm, diff_w_in, diff_q_norm, diff_k_norm, diff_lambda_q1, diff_lambda_k1, diff_lambda_q2, diff_lambda_k2, diff_subln, diff_w_out, mla_w_a, mla_q_a_norm, mla_kv_a_norm, mla_w_qb, mla_w_kvb, mla_q_norm, mla_k_norm, mla_w_out, moe_w_group, moe_b_group, moe_w_expert, moe_b_expert, moe_w_gate, moe_w_up, moe_w_down):
    b, s, d = x.shape
    assert d == D_MODEL and s % Q_TILE == 0 and s % ROW_TILE == 0 and (b * s) % DISPATCH_TILE == 0
    t = b * s
    rope_cos, rope_sin = _rope_tables(s)
    row = lambda v: v.astype(F32).reshape(1, -1)
    x2d = x.reshape(t, d)

    lambda_init = 0.8 - 0.6 * math.exp(-0.3 * 0)
    w_in = diff_w_in[0]
    w_in = jnp.concatenate([_diff_head_layout(w_in[:, :DIFF_QK_WIDTH]),
                            _diff_head_layout(w_in[:, DIFF_QK_WIDTH:2 * DIFF_QK_WIDTH]),
                            w_in[:, 2 * DIFF_QK_WIDTH:]], axis=-1).astype(BF16)
    q, k, vt = _diff_qkv(x2d, row(attn_norm[0]), w_in,
                         _diff_gain_layout(diff_q_norm[0], DIFF_HEAD_DIM ** -0.5 * LOG2E),
                         _diff_gain_layout(diff_k_norm[0], 1.0),
                         rope_cos, rope_sin, _segment_matrix(), s)
    shp = (b, s, DIFF_QK_WIDTH)
    o = _flash(q.reshape(shp), k.reshape(shp), vt, DIFF_HEADS, LANES, Q_TILE,
               diff_params=(row(diff_lambda_q1[0]), row(diff_lambda_k1[0]),
                            row(diff_lambda_q2[0]), row(diff_lambda_k2[0]),
                            row(diff_subln[0])),
               lambda_init=lambda_init)
    wr_both, wr_hi, b_r = _router_weights(moe_w_group[0], moe_b_group[0],
                                          moe_w_expert[0], moe_b_expert[0])
    x1, hn, meta, cnt = _post_attn(o.reshape(t, d), x2d, diff_w_out[0].astype(BF16),
                                   row(ffn_norm[0]), wr_both, wr_hi, b_r)
    x2d = _moe(x1, hn, meta, cnt, moe_w_gate, moe_w_up, moe_w_down, 0)

    w_a = mla_w_a[0]
    split = MLA_Q_LORA + MLA_KV_LORA
    w_a = jnp.concatenate([w_a[:, :split], _pad_rope(w_a[:, split:])], axis=-1).astype(BF16)
    w_qb = _mla_qk_layout(mla_w_qb[0].reshape(MLA_Q_LORA, MLA_HEADS, MLA_QK_DIM))
    w_qb = w_qb.reshape(MLA_Q_LORA, MLA_HEADS * MLA_QK_PAD).astype(BF16)
    w_kvb = mla_w_kvb[0].reshape(MLA_KV_LORA, MLA_HEADS, 2 * MLA_NOPE)
    w_kvb = jnp.concatenate([w_kvb[..., :MLA_NOPE].reshape(MLA_KV_LORA, -1),
                             w_kvb[..., MLA_NOPE:].reshape(MLA_KV_LORA, -1)], axis=-1).astype(BF16)
    q_gain = _mla_qk_layout(mla_q_norm[0].astype(F32) * (MLA_QK_DIM ** -0.5 * LOG2E)).reshape(1, -1)
    k_gain = _mla_qk_layout(mla_k_norm[0].astype(F32)).reshape(1, -1)
    q, k, vt = _mla_proj(x2d, row(attn_norm[1]), w_a, row(mla_q_a_norm[0]), row(mla_kv_a_norm[0]),
                        w_qb, w_kvb, q_gain, k_gain, rope_cos, rope_sin, s)
    qk_shp = (b, s, MLA_HEADS * MLA_QK_PAD)
    o = _flash(q.reshape(qk_shp), k.reshape(qk_shp), vt, MLA_HEADS, MLA_QK_PAD, Q_TILE)
    wr_both, wr_hi, b_r = _router_weights(moe_w_group[1], moe_b_group[1],
                                          moe_w_expert[1], moe_b_expert[1])
    x1, hn, meta, cnt = _post_attn(o.reshape(t, d), x2d, mla_w_out[0].astype(BF16),
                                   row(ffn_norm[1]), wr_both, wr_hi, b_r)
    x2d = _moe(x1, hn, meta, cnt, moe_w_gate, moe_w_up, moe_w_down, 1)
    return x2d.reshape(b, s, d)
```

```python
import functools
import math

import jax
import jax.numpy as jnp
from jax import lax
from jax.experimental import pallas as pl
from jax.experimental.pallas import tpu as pltpu

F32 = jnp.float32
BF16 = jnp.bfloat16

D_MODEL = 1024
PACKED = D_MODEL // 2
ROW_WORDS = PACKED + 128
ROPE_THETA = 10000.0
NORM_EPS = 1e-6
LANES = 128
MXU_DIM = 256

DIFF_HEAD_DIM = 64
DIFF_HEADS = 8
DIFF_QK_WIDTH = 1024

MLA_NOPE = 128
MLA_ROPE = 64
MLA_QK_DIM = 192
MLA_QK_PAD = 256
MLA_HEADS = 8
MLA_Q_LORA = 384
MLA_KV_LORA = 256

N_GROUPS = 8
EXPERTS_PER_GROUP = 8
N_EXPERTS = 64
TOP_K = 2
EXPERT_FF = 256

ROW_TILE = 512
Q_TILE = 1024
FLASH_CHUNK = 512
V_ROWS = 144
LOG2E = math.log2(math.e)
EXPERT_TILE = 256
PAIRS_PER_ITER = {2: 8, 4: 4}
PROJ_PARTS = 2
COPY_TILE = 2048
ISSUE_UNROLL = 8
VMEM_LIMIT = 56 * 1024 * 1024


def _cparams(sem):
    return pltpu.CompilerParams(dimension_semantics=sem, vmem_limit_bytes=VMEM_LIMIT)


def _rms(x, gain):
    return x * lax.rsqrt(jnp.mean(x * x, axis=-1, keepdims=True) + NORM_EPS) * gain


def _pack_bf16_pairs(x):
    lo = lax.bitcast_convert_type(x[:, :PACKED].astype(BF16).astype(F32), jnp.uint32)
    hi = lax.bitcast_convert_type(x[:, PACKED:].astype(BF16).astype(F32), jnp.uint32)
    return (lo >> 16) | (hi & jnp.uint32(0xFFFF0000))


def _unpack_bf16_pairs(p):
    lo = lax.bitcast_convert_type(p << 16, F32)
    hi = lax.bitcast_convert_type(p & jnp.uint32(0xFFFF0000), F32)
    return lo, hi


def _store_v_transposed(v, vt_ref, heads, cols=slice(None)):
    rows = v.shape[0]
    ones = jnp.ones((V_ROWS - LANES, rows), BF16)
    for hd in range(heads):
        lo = hd * V_ROWS
        vt_ref[0, 0, lo:lo + LANES, cols] = v[:, hd * LANES:(hd + 1) * LANES].T.astype(BF16)
        vt_ref[0, 0, lo + LANES:lo + V_ROWS, cols] = ones


def _rope128(u, cos, sin):
    return u * cos + pltpu.roll(u, 64, 1) * sin


def _diff_qkv_kernel(x_ref, g_ref, w_ref, qg_ref, kg_ref, cos_ref, sin_ref, seg_ref,
                     q_ref, k_ref, vt_ref):
    tm = x_ref.shape[0]
    seg = seg_ref[...]

    def project(rows):
        h = _rms(x_ref[rows, :], g_ref[...])
        return jnp.dot(h.astype(BF16), w_ref[...], preferred_element_type=F32)

    def heads(rows, qkv):
        cos = cos_ref[rows, :]
        sin = sin_ref[rows, :]
        for off, gain_ref, out_ref in ((0, qg_ref, q_ref), (DIFF_QK_WIDTH, kg_ref, k_ref)):
            for c in range(DIFF_QK_WIDTH // MXU_DIM):
                t = qkv[:, off + c * MXU_DIM: off + (c + 1) * MXU_DIM]
                ss = jnp.dot((t * t).astype(BF16), seg, preferred_element_type=F32)
                tn = t * lax.rsqrt(ss * (1.0 / DIFF_HEAD_DIM) + NORM_EPS) * gain_ref[...]
                for half in range(MXU_DIM // LANES):
                    u = tn[:, half * LANES:(half + 1) * LANES]
                    lo = c * MXU_DIM + half * LANES
                    out_ref[rows, lo:lo + LANES] = _rope128(u, cos, sin).astype(BF16)
        _store_v_transposed(qkv[:, 2 * DIFF_QK_WIDTH:], vt_ref, DIFF_HEADS, rows)

    part = tm // PROJ_PARTS
    rows = [pl.ds(p * part, part) for p in range(PROJ_PARTS)]
    staged = project(rows[0])
    for p in range(PROJ_PARTS):
        upcoming = project(rows[p + 1]) if p + 1 < PROJ_PARTS else None
        heads(rows[p], staged)
        staged = upcoming


def _diff_qkv(x2d, gain, w_in, q_gain, k_gain, cos, sin, seg, seq):
    t = x2d.shape[0]
    tm = ROW_TILE
    pos_tiles = seq // tm
    row = lambda i: (i, 0)
    const = lambda i: (0, 0)
    out = jax.ShapeDtypeStruct((t, DIFF_QK_WIDTH), BF16)
    vt_out = jax.ShapeDtypeStruct((t // seq, pos_tiles, DIFF_HEADS * V_ROWS, tm), BF16)
    vt_spec = pl.BlockSpec((1, 1, DIFF_HEADS * V_ROWS, tm),
                           lambda i: (i // pos_tiles, i % pos_tiles, 0, 0))
    return pl.pallas_call(
        _diff_qkv_kernel,
        grid=(t // tm,),
        in_specs=[
            pl.BlockSpec((tm, D_MODEL), row),
            pl.BlockSpec((1, D_MODEL), const),
            pl.BlockSpec(w_in.shape, const),
            pl.BlockSpec((1, MXU_DIM), const),
            pl.BlockSpec((1, MXU_DIM), const),
            pl.BlockSpec((tm, LANES), lambda i: (i % pos_tiles, 0)),
            pl.BlockSpec((tm, LANES), lambda i: (i % pos_tiles, 0)),
            pl.BlockSpec((MXU_DIM, MXU_DIM), const),
        ],
        out_specs=[pl.BlockSpec((tm, DIFF_QK_WIDTH), row)] * 2 + [vt_spec],
        out_shape=[out, out, vt_out],
        compiler_params=_cparams(("parallel",)),
        name="diff_qkv",
    )(x2d, gain, w_in, q_gain, k_gain, cos, sin, seg)


def _mla_proj_kernel(x_ref, g_ref, wa_ref, qag_ref, kvag_ref, wqb_ref, wkvb_ref,
                     qg_ref, kg_ref, cos_ref, sin_ref, q_ref, k_ref, vt_ref):
    tm = x_ref.shape[0]
    qg = qg_ref[...]
    kg = kg_ref[...]
    inv_d = 1.0 / MLA_QK_DIM

    def low_rank(rows):
        h = _rms(x_ref[rows, :], g_ref[...])
        return jnp.dot(h.astype(BF16), wa_ref[...], preferred_element_type=F32)

    def up_project(a):
        cq = _rms(a[:, :MLA_Q_LORA], qag_ref[...])
        ckv = _rms(a[:, MLA_Q_LORA:MLA_Q_LORA + MLA_KV_LORA], kvag_ref[...])
        q = jnp.dot(cq.astype(BF16), wqb_ref[...], preferred_element_type=F32)
        kv = jnp.dot(ckv.astype(BF16), wkvb_ref[...], preferred_element_type=F32)
        return q, kv, a[:, MLA_Q_LORA + MLA_KV_LORA:]

    def heads(rows, q, kv, kpe):
        cos = cos_ref[rows, :]
        sin = sin_ref[rows, :]
        kpe_ss = jnp.sum(kpe * kpe, axis=-1, keepdims=True)
        kpe_rot = _rope128(kpe * kg[:, MLA_NOPE:], cos, sin)
        for hd in range(MLA_HEADS):
            lo = hd * MLA_QK_PAD
            qh = q[:, lo:lo + MLA_QK_PAD]
            rq = lax.rsqrt(jnp.sum(qh * qh, axis=-1, keepdims=True) * inv_d + NORM_EPS)
            q_ref[rows, lo:lo + MLA_NOPE] = (qh[:, :MLA_NOPE] * rq * qg[:, :MLA_NOPE]).astype(BF16)
            q_ref[rows, lo + MLA_NOPE:lo + MLA_QK_PAD] = _rope128(
                qh[:, MLA_NOPE:] * rq * qg[:, MLA_NOPE:], cos, sin).astype(BF16)
            kn = kv[:, hd * MLA_NOPE:(hd + 1) * MLA_NOPE]
            rk = lax.rsqrt((jnp.sum(kn * kn, axis=-1, keepdims=True) + kpe_ss) * inv_d + NORM_EPS)
            k_ref[rows, lo:lo + MLA_NOPE] = (kn * rk * kg[:, :MLA_NOPE]).astype(BF16)
            k_ref[rows, lo + MLA_NOPE:lo + MLA_QK_PAD] = (kpe_rot * rk).astype(BF16)
        _store_v_transposed(kv[:, MLA_HEADS * MLA_NOPE:], vt_ref, MLA_HEADS, rows)

    part = tm // PROJ_PARTS
    rows = [pl.ds(p * part, part) for p in range(PROJ_PARTS)]
    staged = up_project(low_rank(rows[0]))
    for p in range(PROJ_PARTS):
        upcoming = up_project(low_rank(rows[p + 1])) if p + 1 < PROJ_PARTS else None
        heads(rows[p], *staged)
        staged = upcoming


def _mla_proj(x2d, gain, w_a, qa_gain, kva_gain, w_qb, w_kvb, q_gain, k_gain, cos, sin, seq):
    t = x2d.shape[0]
    tm = ROW_TILE
    pos_tiles = seq // tm
    row = lambda i: (i, 0)
    const = lambda i: (0, 0)
    qk_out = jax.ShapeDtypeStruct((t, MLA_HEADS * MLA_QK_PAD), BF16)
    vt_out = jax.ShapeDtypeStruct((t // seq, pos_tiles, MLA_HEADS * V_ROWS, tm), BF16)
    return pl.pallas_call(
        _mla_proj_kernel,
        grid=(t // tm,),
        in_specs=[
            pl.BlockSpec((tm, D_MODEL), row),
            pl.BlockSpec((1, D_MODEL), const),
            pl.BlockSpec(w_a.shape, const),
            pl.BlockSpec((1, MLA_Q_LORA), const),
            pl.BlockSpec((1, MLA_KV_LORA), const),
            pl.BlockSpec(w_qb.shape, const),
            pl.BlockSpec(w_kvb.shape, const),
            pl.BlockSpec((1, MLA_QK_PAD), const),
            pl.BlockSpec((1, MLA_QK_PAD), const),
            pl.BlockSpec((tm, LANES), lambda i: (i % pos_tiles, 0)),
            pl.BlockSpec((tm, LANES), lambda i: (i % pos_tiles, 0)),
        ],
        out_specs=[pl.BlockSpec((tm, MLA_HEADS * MLA_QK_PAD), row),
                   pl.BlockSpec((tm, MLA_HEADS * MLA_QK_PAD), row),
                   pl.BlockSpec((1, 1, MLA_HEADS * V_ROWS, tm),
                                lambda i: (i // pos_tiles, i % pos_tiles, 0, 0))],
        out_shape=[qk_out, qk_out, vt_out],
        compiler_params=_cparams(("parallel",)),
        name="mla_proj",
    )(x2d, gain, w_a, qa_gain, kva_gain, w_qb, w_kvb, q_gain, k_gain, cos, sin)


def _flash_kernel(*refs, diff, lambda_init, tq, tk, n_chunks):
    if diff:
        (lq1_ref, lk1_ref, lq2_ref, lk2_ref, subln_ref, q_ref, k_ref, vt_ref, o_ref,
         qt_ref, m_ref, acc_ref, sa_ref, sb_ref) = refs
    else:
        q_ref, k_ref, vt_ref, o_ref, qt_ref, m_ref, acc_ref, sa_ref, sb_ref = refs
    qi = pl.program_id(2)
    ch = FLASH_CHUNK
    copies = 2 if diff else 1
    chunks_per_copy = n_chunks // copies

    q_t = q_ref[0].astype(F32).T
    if diff:
        feat = lax.broadcasted_iota(jnp.int32, q_t.shape, 0)
        first = ((feat // 32) % 2) == 0
        qt_ref[:, :tq] = jnp.where(first, q_t, 0.0).astype(BF16)
        qt_ref[:, tq:] = jnp.where(first, 0.0, q_t).astype(BF16)
    else:
        qt_ref[...] = q_t.astype(BF16)
    m_ref[...] = jnp.full(m_ref.shape, -jnp.inf, F32)
    acc_ref[...] = jnp.zeros(acc_ref.shape, F32)

    def key_tile(j):
        return k_ref[0, pl.ds(pl.multiple_of(j * tk, tk), tk), :]

    def scores(j, s_ref, chunks):
        k = key_tile(j)
        for c in chunks:
            s_ref[c] = jnp.dot(k, qt_ref[:, c * ch:(c + 1) * ch], preferred_element_type=F32)

    def accumulate(j, s_ref, chunks, masked=()):
        vt = vt_ref[0, j]
        for c in chunks:
            s = s_ref[c]
            if c in masked:
                key = lax.broadcasted_iota(jnp.int32, (tk, ch), 0)
                qry = lax.broadcasted_iota(jnp.int32, (tk, ch), 1)
                s = jnp.where(key <= qry, s, -jnp.inf)
            m_prev = m_ref[c]
            m_new = jnp.maximum(m_prev, jnp.max(s, axis=0, keepdims=True))
            alpha = jnp.exp2(m_prev - m_new)
            p = jnp.exp2(s - m_new).astype(BF16)
            acc_ref[c] = alpha * acc_ref[c] + jnp.dot(vt, p, preferred_element_type=F32)
            m_ref[c] = m_new

    every = tuple(range(n_chunks))
    early = tuple(c for c in every if c % chunks_per_copy == 0)
    late = tuple(c for c in every if c % chunks_per_copy == 1)
    scores(0, sa_ref, every)

    def pair(j):
        scores(j + 1, sb_ref, every)
        accumulate(j, sa_ref, every)
        scores(j + 2, sa_ref, every)
        accumulate(j + 1, sb_ref, every)

    def pairs(first_pair, count):
        for r in range(count):
            pair(2 * (first_pair + r))

    per_iter = PAIRS_PER_ITER[n_chunks]

    def main_body(i, carry):
        pairs(per_iter * i, per_iter)
        return carry

    lax.fori_loop(0, qi // per_iter, main_body, 0)
    block = per_iter // 2
    while block >= 1:
        @pl.when(qi % (2 * block) >= block)
        def _(block=block):
            pairs(qi - qi % (2 * block), block)
        block //= 2

    j = 2 * qi
    scores(j + 1, sb_ref, late)
    accumulate(j, sa_ref, every, masked=early)
    accumulate(j + 1, sb_ref, late, masked=late)

    def normalized(c):
        a = acc_ref[c]
        return a[:LANES] * (1.0 / a[LANES:LANES + 1])

    if diff:
        lam = (jnp.exp(jnp.sum(lq1_ref[...] * lk1_ref[...], axis=-1, keepdims=True))
               - jnp.exp(jnp.sum(lq2_ref[...] * lk2_ref[...], axis=-1, keepdims=True))
               + lambda_init)
        for c in range(chunks_per_copy):
            o_t = normalized(c) - lam * normalized(c + chunks_per_copy)
            o_t = o_t * lax.rsqrt(jnp.mean(o_t * o_t, axis=0, keepdims=True) + NORM_EPS)
            o = o_t.T * (subln_ref[...] * (1.0 - lambda_init))
            o_ref[0, c * ch:(c + 1) * ch, :] = o.astype(o_ref.dtype)
    else:
        for c in range(n_chunks):
            o_ref[0, c * ch:(c + 1) * ch, :] = normalized(c).T.astype(o_ref.dtype)


def _flash(q, k, vt, heads, dk, tq, diff_params=None, lambda_init=0.0):
    b, s, _ = q.shape
    tk = ROW_TILE
    diff = diff_params is not None
    n_chunks = (2 if diff else 1) * tq // FLASH_CHUNK
    q_spec = pl.BlockSpec((1, tq, dk), lambda bi, hi, qi: (bi, qi, hi))
    k_spec = pl.BlockSpec((1, s, dk), lambda bi, hi, qi: (bi, 0, hi))
    v_spec = pl.BlockSpec((1, s // tk, V_ROWS, tk), lambda bi, hi, qi: (bi, 0, hi, 0))
    o_spec = pl.BlockSpec((1, tq, LANES), lambda bi, hi, qi: (bi, qi, hi))
    small = lambda shape: pl.BlockSpec(shape, lambda bi, hi, qi: (0, 0))
    in_specs = [q_spec, k_spec, v_spec]
    args = [q, k, vt]
    if diff:
        in_specs = [small((1, DIFF_HEAD_DIM))] * 4 + [small((1, LANES))] + in_specs
        args = list(diff_params) + args
    assert tk == FLASH_CHUNK and tq == 2 * tk
    scratch = [pltpu.VMEM((dk, n_chunks * FLASH_CHUNK), BF16),
               pltpu.VMEM((n_chunks, 1, FLASH_CHUNK), F32),
               pltpu.VMEM((n_chunks, V_ROWS, FLASH_CHUNK), F32),
               pltpu.VMEM((n_chunks, tk, FLASH_CHUNK), F32),
               pltpu.VMEM((n_chunks, tk, FLASH_CHUNK), F32)]
    return pl.pallas_call(
        functools.partial(_flash_kernel, diff=diff, lambda_init=lambda_init, tq=tq, tk=tk,
                          n_chunks=n_chunks),
        grid=(b, heads, s // tq),
        in_specs=in_specs,
        out_specs=o_spec,
        out_shape=jax.ShapeDtypeStruct((b, s, heads * LANES), BF16),
        scratch_shapes=scratch,
        compiler_params=_cparams(("parallel", "parallel", "arbitrary")),
        name="diff_flash" if diff else "mla_flash",
    )(*args)


def _post_attn_kernel(o_ref, x_ref, wo_ref, fg_ref, wr_both_ref, wr_hi_ref, br_ref,
                      x1_ref, hn_ref, meta_ref, cnt_ref, carry_ref):
    i = pl.program_id(0)

    @pl.when(i == 0)
    def _():
        carry_ref[...] = jnp.zeros(carry_ref.shape, F32)

    x1 = x_ref[...] + jnp.dot(o_ref[...], wo_ref[...], preferred_element_type=F32)
    x1_ref[...] = x1
    hn = _rms(x1, fg_ref[...])
    hn_ref[:, :PACKED] = _pack_bf16_pairs(hn)

    hi = hn.astype(BF16)
    lo = (hn - hi.astype(F32)).astype(BF16)
    hi_terms = jnp.dot(hi, wr_both_ref[...], preferred_element_type=F32)
    logits = (hi_terms[:, :LANES] + hi_terms[:, LANES:]
              + jnp.dot(lo, wr_hi_ref[...], preferred_element_type=F32)
              + br_ref[...])
    tm = logits.shape[0]
    lane = lax.broadcasted_iota(jnp.int32, logits.shape, 1)
    lane_f = lane.astype(F32)
    neg = jnp.float32(-jnp.inf)
    big = jnp.float32(1e9)

    is_group = (lane >= N_EXPERTS) & (lane < N_EXPERTS + N_GROUPS)
    gl = jnp.where(is_group, logits, neg)
    gmax = jnp.max(gl, axis=-1, keepdims=True)
    gidx = jnp.min(jnp.where(gl == gmax, lane_f, big), axis=-1, keepdims=True) - N_EXPERTS
    g_w = 1.0 / jnp.sum(jnp.exp(gl - gmax), axis=-1, keepdims=True)

    in_group = (lane < N_EXPERTS) & ((lane // EXPERTS_PER_GROUP).astype(F32) == gidx)
    el = jnp.where(in_group, logits, neg)
    m1 = jnp.max(el, axis=-1, keepdims=True)
    i1 = jnp.min(jnp.where(el == m1, lane_f, big), axis=-1, keepdims=True)
    el2 = jnp.where(lane_f == i1, neg, el)
    m2 = jnp.max(el2, axis=-1, keepdims=True)
    i2 = jnp.min(jnp.where(el2 == m2, lane_f, big), axis=-1, keepdims=True)
    d = jnp.exp(m2 - m1)
    w1 = g_w / (1.0 + d)
    w2 = g_w * d / (1.0 + d)

    first_expert = gidx * EXPERTS_PER_GROUP
    wv = (jnp.where(lane_f == i1 - first_expert, w1, 0.0)
          + jnp.where(lane_f == i2 - first_expert, w2, 0.0))
    hn_ref[:, PACKED:] = lax.bitcast_convert_type(wv, jnp.uint32)

    in_grp = lane_f == gidx + N_EXPERTS
    oh = jnp.where(in_grp, 1.0, 0.0)
    r = lax.broadcasted_iota(jnp.int32, (tm, tm), 0)
    c = lax.broadcasted_iota(jnp.int32, (tm, tm), 1)
    lower = jnp.where(c < r, 1.0, 0.0).astype(BF16)
    prefix = jnp.dot(lower, oh.astype(BF16), preferred_element_type=F32) + carry_ref[...]
    rank = jnp.sum(jnp.where(in_grp, prefix, 0.0), axis=-1, keepdims=True)
    carry = carry_ref[...] + jnp.sum(oh, axis=0, keepdims=True)
    carry_ref[...] = carry
    cnt_ref[...] = jnp.broadcast_to(carry, cnt_ref.shape)
    meta_ref[...] = jnp.where(lane == 0, gidx, jnp.where(lane == 1, rank, 0.0))


def _post_attn(o2d, x2d, w_out, ffn_gain, wr_both, wr_hi, b_r):
    t = x2d.shape[0]
    tm = ROW_TILE
    row = lambda i: (i, 0)
    const = lambda i: (0, 0)
    return pl.pallas_call(
        _post_attn_kernel,
        grid=(t // tm,),
        in_specs=[
            pl.BlockSpec((tm, D_MODEL), row),
            pl.BlockSpec((tm, D_MODEL), row),
            pl.BlockSpec((D_MODEL, D_MODEL), const),
            pl.BlockSpec((1, D_MODEL), const),
            pl.BlockSpec((D_MODEL, 2 * LANES), const),
            pl.BlockSpec((D_MODEL, LANES), const),
            pl.BlockSpec((1, LANES), const),
        ],
        out_specs=[pl.BlockSpec((tm, D_MODEL), row),
                   pl.BlockSpec((tm, ROW_WORDS), row),
                   pl.BlockSpec((tm, LANES), row),
                   pl.BlockSpec((8, LANES), const)],
        out_shape=[jax.ShapeDtypeStruct((t, D_MODEL), F32),
                   jax.ShapeDtypeStruct((t, ROW_WORDS), jnp.uint32),
                   jax.ShapeDtypeStruct((t, LANES), F32),
                   jax.ShapeDtypeStruct((8, LANES), F32)],
        scratch_shapes=[pltpu.VMEM((1, LANES), F32)],
        compiler_params=_cparams(("arbitrary",)),
        name="post_attn_router",
    )(o2d, x2d, w_out, ffn_gain, wr_both, wr_hi, b_r)


def _row_copy(src, dst, sem):
    return pltpu.make_async_copy(src, dst, sem)


def _dispatch_kernel(dest_ref, zero_tile_ref, hn_ref, xs_ref, zeros_ref, sem, pad_sem):
    tm = hn_ref.shape[0]
    tile = zeros_ref.shape[0]
    step = pl.program_id(0)
    base = step * tm

    @pl.when(step == 0)
    def _():
        zeros_ref[...] = jnp.zeros(zeros_ref.shape, zeros_ref.dtype)

        def tile_copy(tl):
            row0 = pl.multiple_of(tl * tile, tile)
            return _row_copy(zeros_ref, xs_ref.at[pl.ds(row0, tile)], pad_sem)

        def start_tile(tl, carry):
            @pl.when(zero_tile_ref[tl] == 1)
            def _():
                tile_copy(tl).start()
            return carry

        def wait_tile(tl, carry):
            @pl.when(zero_tile_ref[tl] == 1)
            def _():
                tile_copy(tl).wait()
            return carry

        n_tiles = xs_ref.shape[0] // tile
        lax.fori_loop(0, n_tiles, start_tile, 0)
        lax.fori_loop(0, n_tiles, wait_tile, 0)

    def issue(t, carry):
        d = dest_ref[base + t]
        _row_copy(hn_ref.at[pl.ds(t, 1)], xs_ref.at[pl.ds(d, 1)], sem).start()
        return carry

    lax.fori_loop(0, tm, issue, 0, unroll=ISSUE_UNROLL)
    _row_copy(hn_ref, xs_ref.at[pl.ds(0, tm)], sem).wait()


def _dispatch(dest, zero_tile, hn, n_slots):
    t = hn.shape[0]
    tm = COPY_TILE
    grid_spec = pltpu.PrefetchScalarGridSpec(
        num_scalar_prefetch=2,
        grid=(t // tm,),
        in_specs=[pl.BlockSpec((tm, ROW_WORDS), lambda i, *_: (i, 0))],
        out_specs=pl.BlockSpec(memory_space=pl.ANY),
        scratch_shapes=[pltpu.VMEM((EXPERT_TILE, ROW_WORDS), jnp.uint32),
                        pltpu.SemaphoreType.DMA(()),
                        pltpu.SemaphoreType.DMA(())],
    )
    return pl.pallas_call(
        _dispatch_kernel,
        grid_spec=grid_spec,
        out_shape=jax.ShapeDtypeStruct((n_slots, ROW_WORDS), jnp.uint32),
        compiler_params=_cparams(("arbitrary",)),
        name="moe_dispatch",
    )(dest, zero_tile, hn)


def _expert_kernel(tg_ref, tnext_ref, tv_ref, xs_ref, wg_hbm, wu_hbm, wd_hbm, ys_ref,
                   stage_g, stage_u, stage_d, wg_bf, wu_bf, wd_bf, sems, *, layer):
    i = pl.program_id(0)
    grp = tg_ref[i]
    first = i == 0
    changed = first | (grp != tg_ref[jnp.maximum(i - 1, 0)])
    ff = EXPERT_FF

    def weight_copies(g):
        e0 = g * EXPERTS_PER_GROUP
        return (_row_copy(wg_hbm.at[layer, pl.ds(e0, EXPERTS_PER_GROUP)], stage_g, sems.at[0]),
                _row_copy(wu_hbm.at[layer, pl.ds(e0, EXPERTS_PER_GROUP)], stage_u, sems.at[1]),
                _row_copy(wd_hbm.at[layer, pl.ds(e0, EXPERTS_PER_GROUP)], stage_d, sems.at[2]))

    @pl.when(first)
    def _():
        for cp in weight_copies(grp):
            cp.start()

    @pl.when(changed)
    def _():
        for cp in weight_copies(grp):
            cp.wait()
        for e in range(EXPERTS_PER_GROUP):
            wg_bf[:, e * ff:(e + 1) * ff] = stage_g[e].astype(BF16)
            wu_bf[:, e * ff:(e + 1) * ff] = stage_u[e].astype(BF16)
            wd_bf[e * ff:(e + 1) * ff, :] = stage_d[e].astype(BF16)
        nxt = tnext_ref[i]

        @pl.when(nxt != grp)
        def _():
            for cp in weight_copies(nxt):
                cp.start()

    @pl.when(tv_ref[i] == 1)
    def _():
        x_lo, x_hi = _unpack_bf16_pairs(xs_ref[:, :PACKED])
        x = jnp.concatenate([x_lo.astype(BF16), x_hi.astype(BF16)], axis=-1)
        wv = lax.bitcast_convert_type(xs_ref[:, PACKED:], F32)
        g = jnp.dot(x, wg_bf[...], preferred_element_type=F32)
        u = jnp.dot(x, wu_bf[...], preferred_element_type=F32)
        hid = g * jax.nn.sigmoid(g) * u
        hid = jnp.concatenate(
            [(hid[:, e * ff:(e + 1) * ff] * wv[:, e:e + 1]).astype(BF16)
             for e in range(EXPERTS_PER_GROUP)], axis=-1)
        y = jnp.dot(hid, wd_bf[...], preferred_element_type=F32)
        ys_ref[...] = _pack_bf16_pairs(y)

    @pl.when(tv_ref[i] == 0)
    def _():
        ys_ref[...] = jnp.zeros(ys_ref.shape, ys_ref.dtype)


def _experts(tile_group, tile_next, tile_valid, xs, w_gate, w_up, w_down, layer):
    p = xs.shape[0]
    tile = EXPERT_TILE
    group_ff = EXPERTS_PER_GROUP * EXPERT_FF
    grid_spec = pltpu.PrefetchScalarGridSpec(
        num_scalar_prefetch=3,
        grid=(p // tile,),
        in_specs=[
            pl.BlockSpec((tile, ROW_WORDS), lambda i, *_: (i, 0)),
            pl.BlockSpec(memory_space=pl.ANY),
            pl.BlockSpec(memory_space=pl.ANY),
            pl.BlockSpec(memory_space=pl.ANY),
        ],
        out_specs=pl.BlockSpec((tile, PACKED), lambda i, *_: (i, 0)),
        scratch_shapes=[pltpu.VMEM((EXPERTS_PER_GROUP, D_MODEL, EXPERT_FF), F32),
                        pltpu.VMEM((EXPERTS_PER_GROUP, D_MODEL, EXPERT_FF), F32),
                        pltpu.VMEM((EXPERTS_PER_GROUP, EXPERT_FF, D_MODEL), F32),
                        pltpu.VMEM((D_MODEL, group_ff), BF16),
                        pltpu.VMEM((D_MODEL, group_ff), BF16),
                        pltpu.VMEM((group_ff, D_MODEL), BF16),
                        pltpu.SemaphoreType.DMA((3,))],
    )
    return pl.pallas_call(
        functools.partial(_expert_kernel, layer=layer),
        grid_spec=grid_spec,
        out_shape=jax.ShapeDtypeStruct((p, PACKED), jnp.uint32),
        compiler_params=_cparams(("arbitrary",)),
        name="moe_experts",
    )(tile_group, tile_next, tile_valid, xs, w_gate, w_up, w_down)


def _combine_kernel(dest_ref, x1_ref, ys_ref, out_ref, buf_ref, sems):
    tm = x1_ref.shape[0]
    step = pl.program_id(0)
    slot = step % 2

    def gather(tile_index, into):
        base = tile_index * tm

        def issue(t, carry):
            d = dest_ref[base + t]
            _row_copy(ys_ref.at[pl.ds(d, 1)], buf_ref.at[into, pl.ds(t, 1)], sems.at[into]).start()
            return carry

        lax.fori_loop(0, tm, issue, 0, unroll=ISSUE_UNROLL)

    @pl.when(step == 0)
    def _():
        gather(0, 0)

    @pl.when(step + 1 < pl.num_programs(0))
    def _():
        gather(step + 1, 1 - slot)

    _row_copy(ys_ref.at[pl.ds(0, tm)], buf_ref.at[slot], sems.at[slot]).wait()

    lo, hi = _unpack_bf16_pairs(buf_ref[slot])
    out_ref[:, :PACKED] = x1_ref[:, :PACKED] + lo
    out_ref[:, PACKED:] = x1_ref[:, PACKED:] + hi


def _combine(dest, x1, ys):
    t = x1.shape[0]
    tm = COPY_TILE
    grid_spec = pltpu.PrefetchScalarGridSpec(
        num_scalar_prefetch=1,
        grid=(t // tm,),
        in_specs=[pl.BlockSpec((tm, D_MODEL), lambda i, dest: (i, 0)),
                  pl.BlockSpec(memory_space=pl.ANY)],
        out_specs=pl.BlockSpec((tm, D_MODEL), lambda i, dest: (i, 0)),
        scratch_shapes=[pltpu.VMEM((2, tm, PACKED), jnp.uint32),
                        pltpu.SemaphoreType.DMA((2,))],
    )
    return pl.pallas_call(
        _combine_kernel,
        grid_spec=grid_spec,
        out_shape=jax.ShapeDtypeStruct((t, D_MODEL), F32),
        compiler_params=_cparams(("arbitrary",)),
        name="moe_combine",
    )(dest, x1, ys)


def _rope_tables(seq):
    inv_freq = 1.0 / (ROPE_THETA ** (jnp.arange(0, DIFF_HEAD_DIM, 2, dtype=F32) / DIFF_HEAD_DIM))
    ang = jnp.arange(seq, dtype=F32)[:, None] * jnp.tile(inv_freq, 4)[None, :]
    sign = jnp.where(jnp.arange(LANES) < LANES // 2, -1.0, 1.0).astype(F32)
    return jnp.cos(ang), jnp.sin(ang) * sign[None, :]


def _diff_head_layout(w):
    lead = w.shape[:-1]
    w = w.reshape(*lead, DIFF_HEADS, 2, 2, DIFF_HEAD_DIM // 2)
    w = jnp.swapaxes(w, -3, -2)
    return w.reshape(*lead, DIFF_QK_WIDTH)


def _diff_gain_layout(g, scale):
    g = (g.astype(F32) * scale).reshape(2, DIFF_HEAD_DIM // 2)
    g = jnp.broadcast_to(g[:, None, :], (2, 2, DIFF_HEAD_DIM // 2))
    return jnp.tile(g.reshape(1, LANES), (1, MXU_DIM // LANES))


def _pad_rope(w):
    half = MLA_ROPE // 2
    z = jnp.zeros(w.shape[:-1] + (half,), w.dtype)
    return jnp.concatenate([w[..., :half], z, w[..., half:], z], axis=-1)


def _mla_qk_layout(w):
    return jnp.concatenate([w[..., :MLA_NOPE], _pad_rope(w[..., MLA_NOPE:])], axis=-1)


def _segment_matrix():
    lane = jnp.arange(MXU_DIM)
    key = (lane // LANES) * 2 + (lane // 32) % 2
    return (key[:, None] == key[None, :]).astype(BF16)


def _moe(x1, hn, meta, cnt, w_gate, w_up, w_down, layer):
    t = x1.shape[0]
    tile = EXPERT_TILE
    n_slots = t + N_GROUPS * tile
    n_tiles = n_slots // tile
    groups = jnp.arange(N_GROUPS, dtype=jnp.int32)
    counts = cnt[0, N_EXPERTS:N_EXPERTS + N_GROUPS].astype(jnp.int32)
    padded = ((counts + tile - 1) // tile) * tile
    ends = jnp.cumsum(padded)
    offsets = ends - padded
    tile_start = jnp.arange(n_tiles, dtype=jnp.int32) * tile
    tile_valid = (tile_start < ends[-1]).astype(jnp.int32)
    tile_group = jnp.sum(tile_start[:, None] >= ends[None, :], axis=1).astype(jnp.int32)
    last_group = jnp.max(jnp.where(counts > 0, groups, 0))
    tile_group = jnp.where(tile_valid == 1, tile_group, last_group)
    later = (groups[None, :] > groups[:, None]) & (counts > 0)[None, :]
    next_of = jnp.min(jnp.where(later, groups[None, :], N_GROUPS), axis=1)
    next_of = jnp.where(next_of == N_GROUPS, groups, next_of)
    onehot_tile = tile_group[:, None] == groups[None, :]
    tile_next = jnp.sum(jnp.where(onehot_tile, next_of[None, :], 0), axis=1).astype(jnp.int32)
    partly_filled = jnp.any((tile_start[:, None] + tile == ends[None, :])
                            & (padded > counts)[None, :], axis=1)
    zero_tile = jnp.where(tile_valid == 1, partly_filled, True).astype(jnp.int32)
    grp = meta[:, 0].astype(jnp.int32)
    rank = meta[:, 1].astype(jnp.int32)
    dest = rank + jnp.sum(jnp.where(grp[:, None] == groups[None, :], offsets[None, :], 0), axis=-1)
    xs = _dispatch(dest, zero_tile, hn, n_slots)
    ys = _experts(tile_group, tile_next, tile_valid, xs, w_gate, w_up, w_down, layer)
    return _combine(dest, x1, ys)


def _router_weights(w_group, b_group, w_expert, b_expert):
    pad = LANES - N_EXPERTS - N_GROUPS
    w = jnp.concatenate([w_expert, w_group, jnp.zeros((D_MODEL, pad), F32)], axis=-1)
    b = jnp.concatenate([b_expert, b_group, jnp.zeros((pad,), F32)]).reshape(1, LANES)
    w_hi = w.astype(BF16)
    w_lo = (w - w_hi.astype(F32)).astype(BF16)
    return jnp.concatenate([w_hi, w_lo], axis=-1), w_hi, b


def kernel(x, attn_norm, ffn_norm, diff_w_in, diff_q_norm, diff_k_norm, diff_lambda_q1, diff_lambda_k1, diff_lambda_q2, diff_lambda_k2, diff_subln, diff_w_out, mla_w_a, mla_q_a_norm, mla_kv_a_norm, mla_w_qb, mla_w_kvb, mla_q_norm, mla_k_norm, mla_w_out, moe_w_group, moe_b_group, moe_w_expert, moe_b_expert, moe_w_gate, moe_w_up, moe_w_down):
    b, s, d = x.shape
    assert d == D_MODEL and s % Q_TILE == 0 and s % ROW_TILE == 0 and (b * s) % COPY_TILE == 0
    t = b * s
    rope_cos, rope_sin = _rope_tables(s)
    row = lambda v: v.astype(F32).reshape(1, -1)
    x2d = x.reshape(t, d)

    lambda_init = 0.8 - 0.6 * math.exp(-0.3 * 0)
    w_in = diff_w_in[0]
    w_in = jnp.concatenate([_diff_head_layout(w_in[:, :DIFF_QK_WIDTH]),
                            _diff_head_layout(w_in[:, DIFF_QK_WIDTH:2 * DIFF_QK_WIDTH]),
                            w_in[:, 2 * DIFF_QK_WIDTH:]], axis=-1).astype(BF16)
    q, k, vt = _diff_qkv(x2d, row(attn_norm[0]), w_in,
                         _diff_gain_layout(diff_q_norm[0], DIFF_HEAD_DIM ** -0.5 * LOG2E),
                         _diff_gain_layout(diff_k_norm[0], 1.0),
                         rope_cos, rope_sin, _segment_matrix(), s)
    shp = (b, s, DIFF_QK_WIDTH)
    o = _flash(q.reshape(shp), k.reshape(shp), vt, DIFF_HEADS, LANES, Q_TILE,
               diff_params=(row(diff_lambda_q1[0]), row(diff_lambda_k1[0]),
                            row(diff_lambda_q2[0]), row(diff_lambda_k2[0]),
                            row(diff_subln[0])),
               lambda_init=lambda_init)
    wr_both, wr_hi, b_r = _router_weights(moe_w_group[0], moe_b_group[0],
                                          moe_w_expert[0], moe_b_expert[0])
    x1, hn, meta, cnt = _post_attn(o.reshape(t, d), x2d, diff_w_out[0].astype(BF16),
                                   row(ffn_norm[0]), wr_both, wr_hi, b_r)
    x2d = _moe(x1, hn, meta, cnt, moe_w_gate, moe_w_up, moe_w_down, 0)

    w_a = mla_w_a[0]
    split = MLA_Q_LORA + MLA_KV_LORA
    w_a = jnp.concatenate([w_a[:, :split], _pad_rope(w_a[:, split:])], axis=-1).astype(BF16)
    w_qb = _mla_qk_layout(mla_w_qb[0].reshape(MLA_Q_LORA, MLA_HEADS, MLA_QK_DIM))
    w_qb = w_qb.reshape(MLA_Q_LORA, MLA_HEADS * MLA_QK_PAD).astype(BF16)
    w_kvb = mla_w_kvb[0].reshape(MLA_KV_LORA, MLA_HEADS, 2 * MLA_NOPE)
    w_kvb = jnp.concatenate([w_kvb[..., :MLA_NOPE].reshape(MLA_KV_LORA, -1),
                             w_kvb[..., MLA_NOPE:].reshape(MLA_KV_LORA, -1)], axis=-1).astype(BF16)
    q_gain = _mla_qk_layout(mla_q_norm[0].astype(F32) * (MLA_QK_DIM ** -0.5 * LOG2E)).reshape(1, -1)
    k_gain = _mla_qk_layout(mla_k_norm[0].astype(F32)).reshape(1, -1)
    q, k, vt = _mla_proj(x2d, row(attn_norm[1]), w_a, row(mla_q_a_norm[0]), row(mla_kv_a_norm[0]),
                        w_qb, w_kvb, q_gain, k_gain, rope_cos, rope_sin, s)
    qk_shp = (b, s, MLA_HEADS * MLA_QK_PAD)
    o = _flash(q.reshape(qk_shp), k.reshape(qk_shp), vt, MLA_HEADS, MLA_QK_PAD, Q_TILE)
    wr_both, wr_hi, b_r = _router_weights(moe_w_group[1], moe_b_group[1],
                                          moe_w_expert[1], moe_b_expert[1])
    x1, hn, meta, cnt = _post_attn(o.reshape(t, d), x2d, mla_w_out[0].astype(BF16),
                                   row(ffn_norm[1]), wr_both, wr_hi, b_r)
    x2d = _moe(x1, hn, meta, cnt, moe_w_gate, moe_w_up, moe_w_down, 1)
    return x2d.reshape(b, s, d)
```

```python
import functools
import math

import jax
import jax.numpy as jnp
from jax import lax
from jax.experimental import pallas as pl
from jax.experimental.pallas import tpu as pltpu

F32 = jnp.float32
BF16 = jnp.bfloat16

D_MODEL = 1024
ROW_WORDS = D_MODEL + 128
ROPE_THETA = 10000.0
NORM_EPS = 1e-6
LANES = 128
MXU_DIM = 256

DIFF_HEAD_DIM = 64
DIFF_HEADS = 8
DIFF_QK_WIDTH = 1024

MLA_NOPE = 128
MLA_ROPE = 64
MLA_QK_DIM = 192
MLA_QK_PAD = 256
MLA_HEADS = 8
MLA_Q_LORA = 384
MLA_KV_LORA = 256

N_GROUPS = 8
EXPERTS_PER_GROUP = 8
N_EXPERTS = 64
TOP_K = 2
EXPERT_FF = 256

ROW_TILE = 512
Q_TILE = 1024
FLASH_CHUNK = 512
V_ROWS = 144
LOG2E = math.log2(math.e)
EXPERT_TILE = 256
PAIRS_PER_ITER = {2: 8, 4: 4}
PROJ_PARTS = 2
COPY_TILE = 1024
ISSUE_UNROLL = 8
VMEM_LIMIT = 56 * 1024 * 1024


def _cparams(sem):
    return pltpu.CompilerParams(dimension_semantics=sem, vmem_limit_bytes=VMEM_LIMIT)


def _rms(x, gain):
    return x * lax.rsqrt(jnp.mean(x * x, axis=-1, keepdims=True) + NORM_EPS) * gain


def _store_v_transposed(v, vt_ref, heads, cols=slice(None)):
    rows = v.shape[0]
    ones = jnp.ones((V_ROWS - LANES, rows), BF16)
    for hd in range(heads):
        lo = hd * V_ROWS
        vt_ref[0, 0, lo:lo + LANES, cols] = v[:, hd * LANES:(hd + 1) * LANES].T.astype(BF16)
        vt_ref[0, 0, lo + LANES:lo + V_ROWS, cols] = ones


def _rope128(u, cos, sin):
    return u * cos + pltpu.roll(u, 64, 1) * sin


def _diff_qkv_kernel(x_ref, g_ref, w_ref, qg_ref, kg_ref, cos_ref, sin_ref, seg_ref,
                     q_ref, k_ref, vt_ref):
    tm = x_ref.shape[0]
    seg = seg_ref[...]

    def project(rows):
        h = _rms(x_ref[rows, :], g_ref[...])
        return jnp.dot(h.astype(BF16), w_ref[...], preferred_element_type=F32)

    def heads(rows, qkv):
        cos = cos_ref[rows, :]
        sin = sin_ref[rows, :]
        for off, gain_ref, out_ref in ((0, qg_ref, q_ref), (DIFF_QK_WIDTH, kg_ref, k_ref)):
            for c in range(DIFF_QK_WIDTH // MXU_DIM):
                t = qkv[:, off + c * MXU_DIM: off + (c + 1) * MXU_DIM]
                ss = jnp.dot((t * t).astype(BF16), seg, preferred_element_type=F32)
                tn = t * lax.rsqrt(ss * (1.0 / DIFF_HEAD_DIM) + NORM_EPS) * gain_ref[...]
                for half in range(MXU_DIM // LANES):
                    u = tn[:, half * LANES:(half + 1) * LANES]
                    lo = c * MXU_DIM + half * LANES
                    out_ref[rows, lo:lo + LANES] = _rope128(u, cos, sin).astype(BF16)
        _store_v_transposed(qkv[:, 2 * DIFF_QK_WIDTH:], vt_ref, DIFF_HEADS, rows)

    part = tm // PROJ_PARTS
    rows = [pl.ds(p * part, part) for p in range(PROJ_PARTS)]
    staged = project(rows[0])
    for p in range(PROJ_PARTS):
        upcoming = project(rows[p + 1]) if p + 1 < PROJ_PARTS else None
        heads(rows[p], staged)
        staged = upcoming


def _diff_qkv(x2d, gain, w_in, q_gain, k_gain, cos, sin, seg, seq):
    t = x2d.shape[0]
    tm = ROW_TILE
    pos_tiles = seq // tm
    row = lambda i: (i, 0)
    const = lambda i: (0, 0)
    out = jax.ShapeDtypeStruct((t, DIFF_QK_WIDTH), BF16)
    vt_out = jax.ShapeDtypeStruct((t // seq, pos_tiles, DIFF_HEADS * V_ROWS, tm), BF16)
    vt_spec = pl.BlockSpec((1, 1, DIFF_HEADS * V_ROWS, tm),
                           lambda i: (i // pos_tiles, i % pos_tiles, 0, 0))
    return pl.pallas_call(
        _diff_qkv_kernel,
        grid=(t // tm,),
        in_specs=[
            pl.BlockSpec((tm, D_MODEL), row),
            pl.BlockSpec((1, D_MODEL), const),
            pl.BlockSpec(w_in.shape, const),
            pl.BlockSpec((1, MXU_DIM), const),
            pl.BlockSpec((1, MXU_DIM), const),
            pl.BlockSpec((tm, LANES), lambda i: (i % pos_tiles, 0)),
            pl.BlockSpec((tm, LANES), lambda i: (i % pos_tiles, 0)),
            pl.BlockSpec((MXU_DIM, MXU_DIM), const),
        ],
        out_specs=[pl.BlockSpec((tm, DIFF_QK_WIDTH), row)] * 2 + [vt_spec],
        out_shape=[out, out, vt_out],
        compiler_params=_cparams(("parallel",)),
        name="diff_qkv",
    )(x2d, gain, w_in, q_gain, k_gain, cos, sin, seg)


def _mla_proj_kernel(x_ref, g_ref, wa_ref, qag_ref, kvag_ref, wqb_ref, wkvb_ref,
                     qg_ref, kg_ref, cos_ref, sin_ref, q_ref, k_ref, vt_ref):
    tm = x_ref.shape[0]
    qg = qg_ref[...]
    kg = kg_ref[...]
    inv_d = 1.0 / MLA_QK_DIM

    def low_rank(rows):
        h = _rms(x_ref[rows, :], g_ref[...])
        return jnp.dot(h.astype(BF16), wa_ref[...], preferred_element_type=F32)

    def up_project(a):
        cq = _rms(a[:, :MLA_Q_LORA], qag_ref[...])
        ckv = _rms(a[:, MLA_Q_LORA:MLA_Q_LORA + MLA_KV_LORA], kvag_ref[...])
        q = jnp.dot(cq.astype(BF16), wqb_ref[...], preferred_element_type=F32)
        kv = jnp.dot(ckv.astype(BF16), wkvb_ref[...], preferred_element_type=F32)
        return q, kv, a[:, MLA_Q_LORA + MLA_KV_LORA:]

    def heads(rows, q, kv, kpe):
        cos = cos_ref[rows, :]
        sin = sin_ref[rows, :]
        kpe_ss = jnp.sum(kpe * kpe, axis=-1, keepdims=True)
        kpe_rot = _rope128(kpe * kg[:, MLA_NOPE:], cos, sin)
        for hd in range(MLA_HEADS):
            lo = hd * MLA_QK_PAD
            qh = q[:, lo:lo + MLA_QK_PAD]
            rq = lax.rsqrt(jnp.sum(qh * qh, axis=-1, keepdims=True) * inv_d + NORM_EPS)
            q_ref[rows, lo:lo + MLA_NOPE] = (qh[:, :MLA_NOPE] * rq * qg[:, :MLA_NOPE]).astype(BF16)
            q_ref[rows, lo + MLA_NOPE:lo + MLA_QK_PAD] = _rope128(
                qh[:, MLA_NOPE:] * rq * qg[:, MLA_NOPE:], cos, sin).astype(BF16)
            kn = kv[:, hd * MLA_NOPE:(hd + 1) * MLA_NOPE]
            rk = lax.rsqrt((jnp.sum(kn * kn, axis=-1, keepdims=True) + kpe_ss) * inv_d + NORM_EPS)
            k_ref[rows, lo:lo + MLA_NOPE] = (kn * rk * kg[:, :MLA_NOPE]).astype(BF16)
            k_ref[rows, lo + MLA_NOPE:lo + MLA_QK_PAD] = (kpe_rot * rk).astype(BF16)
        _store_v_transposed(kv[:, MLA_HEADS * MLA_NOPE:], vt_ref, MLA_HEADS, rows)

    part = tm // PROJ_PARTS
    rows = [pl.ds(p * part, part) for p in range(PROJ_PARTS)]
    staged = up_project(low_rank(rows[0]))
    for p in range(PROJ_PARTS):
        upcoming = up_project(low_rank(rows[p + 1])) if p + 1 < PROJ_PARTS else None
        heads(rows[p], *staged)
        staged = upcoming


def _mla_proj(x2d, gain, w_a, qa_gain, kva_gain, w_qb, w_kvb, q_gain, k_gain, cos, sin, seq):
    t = x2d.shape[0]
    tm = ROW_TILE
    pos_tiles = seq // tm
    row = lambda i: (i, 0)
    const = lambda i: (0, 0)
    qk_out = jax.ShapeDtypeStruct((t, MLA_HEADS * MLA_QK_PAD), BF16)
    vt_out = jax.ShapeDtypeStruct((t // seq, pos_tiles, MLA_HEADS * V_ROWS, tm), BF16)
    return pl.pallas_call(
        _mla_proj_kernel,
        grid=(t // tm,),
        in_specs=[
            pl.BlockSpec((tm, D_MODEL), row),
            pl.BlockSpec((1, D_MODEL), const),
            pl.BlockSpec(w_a.shape, const),
            pl.BlockSpec((1, MLA_Q_LORA), const),
            pl.BlockSpec((1, MLA_KV_LORA), const),
            pl.BlockSpec(w_qb.shape, const),
            pl.BlockSpec(w_kvb.shape, const),
            pl.BlockSpec((1, MLA_QK_PAD), const),
            pl.BlockSpec((1, MLA_QK_PAD), const),
            pl.BlockSpec((tm, LANES), lambda i: (i % pos_tiles, 0)),
            pl.BlockSpec((tm, LANES), lambda i: (i % pos_tiles, 0)),
        ],
        out_specs=[pl.BlockSpec((tm, MLA_HEADS * MLA_QK_PAD), row),
                   pl.BlockSpec((tm, MLA_HEADS * MLA_QK_PAD), row),
                   pl.BlockSpec((1, 1, MLA_HEADS * V_ROWS, tm),
                                lambda i: (i // pos_tiles, i % pos_tiles, 0, 0))],
        out_shape=[qk_out, qk_out, vt_out],
        compiler_params=_cparams(("parallel",)),
        name="mla_proj",
    )(x2d, gain, w_a, qa_gain, kva_gain, w_qb, w_kvb, q_gain, k_gain, cos, sin)


def _flash_kernel(*refs, diff, lambda_init, tq, tk, n_chunks):
    if diff:
        (lq1_ref, lk1_ref, lq2_ref, lk2_ref, subln_ref, q_ref, k_ref, vt_ref, o_ref,
         qt_ref, m_ref, acc_ref, sa_ref, sb_ref) = refs
    else:
        q_ref, k_ref, vt_ref, o_ref, qt_ref, m_ref, acc_ref, sa_ref, sb_ref = refs
    qi = pl.program_id(2)
    ch = FLASH_CHUNK
    copies = 2 if diff else 1
    chunks_per_copy = n_chunks // copies

    q_t = q_ref[0].astype(F32).T
    if diff:
        feat = lax.broadcasted_iota(jnp.int32, q_t.shape, 0)
        first = ((feat // 32) % 2) == 0
        qt_ref[:, :tq] = jnp.where(first, q_t, 0.0).astype(BF16)
        qt_ref[:, tq:] = jnp.where(first, 0.0, q_t).astype(BF16)
    else:
        qt_ref[...] = q_t.astype(BF16)
    m_ref[...] = jnp.full(m_ref.shape, -jnp.inf, F32)
    acc_ref[...] = jnp.zeros(acc_ref.shape, F32)

    def key_tile(j):
        return k_ref[0, pl.ds(pl.multiple_of(j * tk, tk), tk), :]

    def scores(j, s_ref, chunks):
        k = key_tile(j)
        for c in chunks:
            s_ref[c] = jnp.dot(k, qt_ref[:, c * ch:(c + 1) * ch], preferred_element_type=F32)

    def accumulate(j, s_ref, chunks, masked=()):
        vt = vt_ref[0, j]
        for c in chunks:
            s = s_ref[c]
            if c in masked:
                key = lax.broadcasted_iota(jnp.int32, (tk, ch), 0)
                qry = lax.broadcasted_iota(jnp.int32, (tk, ch), 1)
                s = jnp.where(key <= qry, s, -jnp.inf)
            m_prev = m_ref[c]
            m_new = jnp.maximum(m_prev, jnp.max(s, axis=0, keepdims=True))
            alpha = jnp.exp2(m_prev - m_new)
            p = jnp.exp2(s - m_new).astype(BF16)
            acc_ref[c] = alpha * acc_ref[c] + jnp.dot(vt, p, preferred_element_type=F32)
            m_ref[c] = m_new

    every = tuple(range(n_chunks))
    early = tuple(c for c in every if c % chunks_per_copy == 0)
    late = tuple(c for c in every if c % chunks_per_copy == 1)
    scores(0, sa_ref, every)

    def pair(j):
        scores(j + 1, sb_ref, every)
        accumulate(j, sa_ref, every)
        scores(j + 2, sa_ref, every)
        accumulate(j + 1, sb_ref, every)

    def pairs(first_pair, count):
        for r in range(count):
            pair(2 * (first_pair + r))

    per_iter = PAIRS_PER_ITER[n_chunks]

    def main_body(i, carry):
        pairs(per_iter * i, per_iter)
        return carry

    lax.fori_loop(0, qi // per_iter, main_body, 0)
    block = per_iter // 2
    while block >= 1:
        @pl.when(qi % (2 * block) >= block)
        def _(block=block):
            pairs(qi - qi % (2 * block), block)
        block //= 2

    j = 2 * qi
    scores(j + 1, sb_ref, late)
    accumulate(j, sa_ref, every, masked=early)
    accumulate(j + 1, sb_ref, late, masked=late)

    def normalized(c):
        a = acc_ref[c]
        return a[:LANES] * (1.0 / a[LANES:LANES + 1])

    if diff:
        lam = (jnp.exp(jnp.sum(lq1_ref[...] * lk1_ref[...], axis=-1, keepdims=True))
               - jnp.exp(jnp.sum(lq2_ref[...] * lk2_ref[...], axis=-1, keepdims=True))
               + lambda_init)
        for c in range(chunks_per_copy):
            o_t = normalized(c) - lam * normalized(c + chunks_per_copy)
            o_t = o_t * lax.rsqrt(jnp.mean(o_t * o_t, axis=0, keepdims=True) + NORM_EPS)
            o = o_t.T * (subln_ref[...] * (1.0 - lambda_init))
            o_ref[0, c * ch:(c + 1) * ch, :] = o.astype(o_ref.dtype)
    else:
        for c in range(n_chunks):
            o_ref[0, c * ch:(c + 1) * ch, :] = normalized(c).T.astype(o_ref.dtype)


def _flash(q, k, vt, heads, dk, tq, diff_params=None, lambda_init=0.0):
    b, s, _ = q.shape
    tk = ROW_TILE
    diff = diff_params is not None
    n_chunks = (2 if diff else 1) * tq // FLASH_CHUNK
    q_spec = pl.BlockSpec((1, tq, dk), lambda bi, hi, qi: (bi, qi, hi))
    k_spec = pl.BlockSpec((1, s, dk), lambda bi, hi, qi: (bi, 0, hi))
    v_spec = pl.BlockSpec((1, s // tk, V_ROWS, tk), lambda bi, hi, qi: (bi, 0, hi, 0))
    o_spec = pl.BlockSpec((1, tq, LANES), lambda bi, hi, qi: (bi, qi, hi))
    small = lambda shape: pl.BlockSpec(shape, lambda bi, hi, qi: (0, 0))
    in_specs = [q_spec, k_spec, v_spec]
    args = [q, k, vt]
    if diff:
        in_specs = [small((1, DIFF_HEAD_DIM))] * 4 + [small((1, LANES))] + in_specs
        args = list(diff_params) + args
    assert tk == FLASH_CHUNK and tq == 2 * tk
    scratch = [pltpu.VMEM((dk, n_chunks * FLASH_CHUNK), BF16),
               pltpu.VMEM((n_chunks, 1, FLASH_CHUNK), F32),
               pltpu.VMEM((n_chunks, V_ROWS, FLASH_CHUNK), F32),
               pltpu.VMEM((n_chunks, tk, FLASH_CHUNK), F32),
               pltpu.VMEM((n_chunks, tk, FLASH_CHUNK), F32)]
    return pl.pallas_call(
        functools.partial(_flash_kernel, diff=diff, lambda_init=lambda_init, tq=tq, tk=tk,
                          n_chunks=n_chunks),
        grid=(b, heads, s // tq),
        in_specs=in_specs,
        out_specs=o_spec,
        out_shape=jax.ShapeDtypeStruct((b, s, heads * LANES), BF16),
        scratch_shapes=scratch,
        compiler_params=_cparams(("parallel", "parallel", "arbitrary")),
        name="diff_flash" if diff else "mla_flash",
    )(*args)


def _post_attn_kernel(o_ref, x_ref, wo_ref, fg_ref, wr_both_ref, wr_hi_ref, br_ref,
                      x1_ref, hn_ref, meta_ref, cnt_ref, carry_ref):
    i = pl.program_id(0)

    @pl.when(i == 0)
    def _():
        carry_ref[...] = jnp.zeros(carry_ref.shape, F32)

    x1 = x_ref[...] + jnp.dot(o_ref[...], wo_ref[...], preferred_element_type=F32)
    x1_ref[...] = x1
    hn = _rms(x1, fg_ref[...])
    hn_ref[:, :D_MODEL] = hn

    hi = hn.astype(BF16)
    lo = (hn - hi.astype(F32)).astype(BF16)
    hi_terms = jnp.dot(hi, wr_both_ref[...], preferred_element_type=F32)
    logits = (hi_terms[:, :LANES] + hi_terms[:, LANES:]
              + jnp.dot(lo, wr_hi_ref[...], preferred_element_type=F32)
              + br_ref[...])
    tm = logits.shape[0]
    lane = lax.broadcasted_iota(jnp.int32, logits.shape, 1)
    lane_f = lane.astype(F32)
    neg = jnp.float32(-jnp.inf)
    big = jnp.float32(1e9)

    is_group = (lane >= N_EXPERTS) & (lane < N_EXPERTS + N_GROUPS)
    gl = jnp.where(is_group, logits, neg)
    gmax = jnp.max(gl, axis=-1, keepdims=True)
    gidx = jnp.min(jnp.where(gl == gmax, lane_f, big), axis=-1, keepdims=True) - N_EXPERTS
    g_w = 1.0 / jnp.sum(jnp.exp(gl - gmax), axis=-1, keepdims=True)

    in_group = (lane < N_EXPERTS) & ((lane // EXPERTS_PER_GROUP).astype(F32) == gidx)
    el = jnp.where(in_group, logits, neg)
    m1 = jnp.max(el, axis=-1, keepdims=True)
    i1 = jnp.min(jnp.where(el == m1, lane_f, big), axis=-1, keepdims=True)
    el2 = jnp.where(lane_f == i1, neg, el)
    m2 = jnp.max(el2, axis=-1, keepdims=True)
    i2 = jnp.min(jnp.where(el2 == m2, lane_f, big), axis=-1, keepdims=True)
    d = jnp.exp(m2 - m1)
    w1 = g_w / (1.0 + d)
    w2 = g_w * d / (1.0 + d)

    first_expert = gidx * EXPERTS_PER_GROUP
    wv = (jnp.where(lane_f == i1 - first_expert, w1, 0.0)
          + jnp.where(lane_f == i2 - first_expert, w2, 0.0))
    hn_ref[:, D_MODEL:] = wv

    in_grp = lane_f == gidx + N_EXPERTS
    oh = jnp.where(in_grp, 1.0, 0.0)
    r = lax.broadcasted_iota(jnp.int32, (tm, tm), 0)
    c = lax.broadcasted_iota(jnp.int32, (tm, tm), 1)
    lower = jnp.where(c < r, 1.0, 0.0).astype(BF16)
    prefix = jnp.dot(lower, oh.astype(BF16), preferred_element_type=F32) + carry_ref[...]
    rank = jnp.sum(jnp.where(in_grp, prefix, 0.0), axis=-1, keepdims=True)
    carry = carry_ref[...] + jnp.sum(oh, axis=0, keepdims=True)
    carry_ref[...] = carry
    cnt_ref[...] = jnp.broadcast_to(carry, cnt_ref.shape)
    meta_ref[...] = jnp.where(lane == 0, gidx, jnp.where(lane == 1, rank, 0.0))


def _post_attn(o2d, x2d, w_out, ffn_gain, wr_both, wr_hi, b_r):
    t = x2d.shape[0]
    tm = ROW_TILE
    row = lambda i: (i, 0)
    const = lambda i: (0, 0)
    return pl.pallas_call(
        _post_attn_kernel,
        grid=(t // tm,),
        in_specs=[
            pl.BlockSpec((tm, D_MODEL), row),
            pl.BlockSpec((tm, D_MODEL), row),
            pl.BlockSpec((D_MODEL, D_MODEL), const),
            pl.BlockSpec((1, D_MODEL), const),
            pl.BlockSpec((D_MODEL, 2 * LANES), const),
            pl.BlockSpec((D_MODEL, LANES), const),
            pl.BlockSpec((1, LANES), const),
        ],
        out_specs=[pl.BlockSpec((tm, D_MODEL), row),
                   pl.BlockSpec((tm, ROW_WORDS), row),
                   pl.BlockSpec((tm, LANES), row),
                   pl.BlockSpec((8, LANES), const)],
        out_shape=[jax.ShapeDtypeStruct((t, D_MODEL), F32),
                   jax.ShapeDtypeStruct((t, ROW_WORDS), F32),
                   jax.ShapeDtypeStruct((t, LANES), F32),
                   jax.ShapeDtypeStruct((8, LANES), F32)],
        scratch_shapes=[pltpu.VMEM((1, LANES), F32)],
        compiler_params=_cparams(("arbitrary",)),
        name="post_attn_router",
    )(o2d, x2d, w_out, ffn_gain, wr_both, wr_hi, b_r)


def _row_copy(src, dst, sem):
    return pltpu.make_async_copy(src, dst, sem)


def _dispatch_kernel(dest_ref, zero_tile_ref, hn_ref, xs_ref, zeros_ref, sem, pad_sem):
    tm = hn_ref.shape[0]
    tile = zeros_ref.shape[0]
    step = pl.program_id(0)
    base = step * tm

    @pl.when(step == 0)
    def _():
        zeros_ref[...] = jnp.zeros(zeros_ref.shape, zeros_ref.dtype)

        def tile_copy(tl):
            row0 = pl.multiple_of(tl * tile, tile)
            return _row_copy(zeros_ref, xs_ref.at[pl.ds(row0, tile)], pad_sem)

        def start_tile(tl, carry):
            @pl.when(zero_tile_ref[tl] == 1)
            def _():
                tile_copy(tl).start()
            return carry

        def wait_tile(tl, carry):
            @pl.when(zero_tile_ref[tl] == 1)
            def _():
                tile_copy(tl).wait()
            return carry

        n_tiles = xs_ref.shape[0] // tile
        lax.fori_loop(0, n_tiles, start_tile, 0)
        lax.fori_loop(0, n_tiles, wait_tile, 0)

    def issue(t, carry):
        d = dest_ref[base + t]
        _row_copy(hn_ref.at[pl.ds(t, 1)], xs_ref.at[pl.ds(d, 1)], sem).start()
        return carry

    lax.fori_loop(0, tm, issue, 0, unroll=ISSUE_UNROLL)
    _row_copy(hn_ref, xs_ref.at[pl.ds(0, tm)], sem).wait()


def _dispatch(dest, zero_tile, hn, n_slots):
    t = hn.shape[0]
    tm = COPY_TILE
    grid_spec = pltpu.PrefetchScalarGridSpec(
        num_scalar_prefetch=2,
        grid=(t // tm,),
        in_specs=[pl.BlockSpec((tm, ROW_WORDS), lambda i, *_: (i, 0))],
        out_specs=pl.BlockSpec(memory_space=pl.ANY),
        scratch_shapes=[pltpu.VMEM((EXPERT_TILE, ROW_WORDS), F32),
                        pltpu.SemaphoreType.DMA(()),
                        pltpu.SemaphoreType.DMA(())],
    )
    return pl.pallas_call(
        _dispatch_kernel,
        grid_spec=grid_spec,
        out_shape=jax.ShapeDtypeStruct((n_slots, ROW_WORDS), F32),
        compiler_params=_cparams(("arbitrary",)),
        name="moe_dispatch",
    )(dest, zero_tile, hn)


def _expert_kernel(tg_ref, tnext_ref, tv_ref, xs_ref, wg_hbm, wu_hbm, wd_hbm, ys_ref,
                   stage_g, stage_u, stage_d, wg_bf, wu_bf, wd_bf, sems, *, layer):
    i = pl.program_id(0)
    grp = tg_ref[i]
    first = i == 0
    changed = first | (grp != tg_ref[jnp.maximum(i - 1, 0)])
    ff = EXPERT_FF

    def weight_copies(g):
        e0 = g * EXPERTS_PER_GROUP
        return (_row_copy(wg_hbm.at[layer, pl.ds(e0, EXPERTS_PER_GROUP)], stage_g, sems.at[0]),
                _row_copy(wu_hbm.at[layer, pl.ds(e0, EXPERTS_PER_GROUP)], stage_u, sems.at[1]),
                _row_copy(wd_hbm.at[layer, pl.ds(e0, EXPERTS_PER_GROUP)], stage_d, sems.at[2]))

    @pl.when(first)
    def _():
        for cp in weight_copies(grp):
            cp.start()

    @pl.when(changed)
    def _():
        for cp in weight_copies(grp):
            cp.wait()
        for e in range(EXPERTS_PER_GROUP):
            wg_bf[:, e * ff:(e + 1) * ff] = stage_g[e].astype(BF16)
            wu_bf[:, e * ff:(e + 1) * ff] = stage_u[e].astype(BF16)
            wd_bf[e * ff:(e + 1) * ff, :] = stage_d[e].astype(BF16)
        nxt = tnext_ref[i]

        @pl.when(nxt != grp)
        def _():
            for cp in weight_copies(nxt):
                cp.start()

    @pl.when(tv_ref[i] == 1)
    def _():
        x = xs_ref[:, :D_MODEL].astype(BF16)
        wv = xs_ref[:, D_MODEL:]
        g = jnp.dot(x, wg_bf[...], preferred_element_type=F32)
        u = jnp.dot(x, wu_bf[...], preferred_element_type=F32)
        hid = g * jax.nn.sigmoid(g) * u
        hid = jnp.concatenate(
            [(hid[:, e * ff:(e + 1) * ff] * wv[:, e:e + 1]).astype(BF16)
             for e in range(EXPERTS_PER_GROUP)], axis=-1)
        y = jnp.dot(hid, wd_bf[...], preferred_element_type=F32)
        ys_ref[...] = y

    @pl.when(tv_ref[i] == 0)
    def _():
        ys_ref[...] = jnp.zeros(ys_ref.shape, ys_ref.dtype)


def _experts(tile_group, tile_next, tile_valid, xs, w_gate, w_up, w_down, layer):
    p = xs.shape[0]
    tile = EXPERT_TILE
    group_ff = EXPERTS_PER_GROUP * EXPERT_FF
    grid_spec = pltpu.PrefetchScalarGridSpec(
        num_scalar_prefetch=3,
        grid=(p // tile,),
        in_specs=[
            pl.BlockSpec((tile, ROW_WORDS), lambda i, *_: (i, 0)),
            pl.BlockSpec(memory_space=pl.ANY),
            pl.BlockSpec(memory_space=pl.ANY),
            pl.BlockSpec(memory_space=pl.ANY),
        ],
        out_specs=pl.BlockSpec((tile, D_MODEL), lambda i, *_: (i, 0)),
        scratch_shapes=[pltpu.VMEM((EXPERTS_PER_GROUP, D_MODEL, EXPERT_FF), F32),
                        pltpu.VMEM((EXPERTS_PER_GROUP, D_MODEL, EXPERT_FF), F32),
                        pltpu.VMEM((EXPERTS_PER_GROUP, EXPERT_FF, D_MODEL), F32),
                        pltpu.VMEM((D_MODEL, group_ff), BF16),
                        pltpu.VMEM((D_MODEL, group_ff), BF16),
                        pltpu.VMEM((group_ff, D_MODEL), BF16),
                        pltpu.SemaphoreType.DMA((3,))],
    )
    return pl.pallas_call(
        functools.partial(_expert_kernel, layer=layer),
        grid_spec=grid_spec,
        out_shape=jax.ShapeDtypeStruct((p, D_MODEL), F32),
        compiler_params=_cparams(("arbitrary",)),
        name="moe_experts",
    )(tile_group, tile_next, tile_valid, xs, w_gate, w_up, w_down)


def _combine_kernel(dest_ref, x1_ref, ys_ref, out_ref, buf_ref, sems):
    tm = x1_ref.shape[0]
    step = pl.program_id(0)
    slot = step % 2

    def gather(tile_index, into):
        base = tile_index * tm

        def issue(t, carry):
            d = dest_ref[base + t]
            _row_copy(ys_ref.at[pl.ds(d, 1)], buf_ref.at[into, pl.ds(t, 1)], sems.at[into]).start()
            return carry

        lax.fori_loop(0, tm, issue, 0, unroll=ISSUE_UNROLL)

    @pl.when(step == 0)
    def _():
        gather(0, 0)

    @pl.when(step + 1 < pl.num_programs(0))
    def _():
        gather(step + 1, 1 - slot)

    _row_copy(ys_ref.at[pl.ds(0, tm)], buf_ref.at[slot], sems.at[slot]).wait()

    out_ref[...] = x1_ref[...] + buf_ref[slot]


def _combine(dest, x1, ys):
    t = x1.shape[0]
    tm = COPY_TILE
    grid_spec = pltpu.PrefetchScalarGridSpec(
        num_scalar_prefetch=1,
        grid=(t // tm,),
        in_specs=[pl.BlockSpec((tm, D_MODEL), lambda i, dest: (i, 0)),
                  pl.BlockSpec(memory_space=pl.ANY)],
        out_specs=pl.BlockSpec((tm, D_MODEL), lambda i, dest: (i, 0)),
        scratch_shapes=[pltpu.VMEM((2, tm, D_MODEL), F32),
                        pltpu.SemaphoreType.DMA((2,))],
    )
    return pl.pallas_call(
        _combine_kernel,
        grid_spec=grid_spec,
        out_shape=jax.ShapeDtypeStruct((t, D_MODEL), F32),
        compiler_params=_cparams(("arbitrary",)),
        name="moe_combine",
    )(dest, x1, ys)


def _rope_tables(seq):
    inv_freq = 1.0 / (ROPE_THETA ** (jnp.arange(0, DIFF_HEAD_DIM, 2, dtype=F32) / DIFF_HEAD_DIM))
    ang = jnp.arange(seq, dtype=F32)[:, None] * jnp.tile(inv_freq, 4)[None, :]
    sign = jnp.where(jnp.arange(LANES) < LANES // 2, -1.0, 1.0).astype(F32)
    return jnp.cos(ang), jnp.sin(ang) * sign[None, :]


def _diff_head_layout(w):
    lead = w.shape[:-1]
    w = w.reshape(*lead, DIFF_HEADS, 2, 2, DIFF_HEAD_DIM // 2)
    w = jnp.swapaxes(w, -3, -2)
    return w.reshape(*lead, DIFF_QK_WIDTH)


def _diff_gain_layout(g, scale):
    g = (g.astype(F32) * scale).reshape(2, DIFF_HEAD_DIM // 2)
    g = jnp.broadcast_to(g[:, None, :], (2, 2, DIFF_HEAD_DIM // 2))
    return jnp.tile(g.reshape(1, LANES), (1, MXU_DIM // LANES))


def _pad_rope(w):
    half = MLA_ROPE // 2
    z = jnp.zeros(w.shape[:-1] + (half,), w.dtype)
    return jnp.concatenate([w[..., :half], z, w[..., half:], z], axis=-1)


def _mla_qk_layout(w):
    return jnp.concatenate([w[..., :MLA_NOPE], _pad_rope(w[..., MLA_NOPE:])], axis=-1)


def _segment_matrix():
    lane = jnp.arange(MXU_DIM)
    key = (lane // LANES) * 2 + (lane // 32) % 2
    return (key[:, None] == key[None, :]).astype(BF16)


def _moe(x1, hn, meta, cnt, w_gate, w_up, w_down, layer):
    t = x1.shape[0]
    tile = EXPERT_TILE
    n_slots = t + N_GROUPS * tile
    n_tiles = n_slots // tile
    groups = jnp.arange(N_GROUPS, dtype=jnp.int32)
    counts = cnt[0, N_EXPERTS:N_EXPERTS + N_GROUPS].astype(jnp.int32)
    padded = ((counts + tile - 1) // tile) * tile
    ends = jnp.cumsum(padded)
    offsets = ends - padded
    tile_start = jnp.arange(n_tiles, dtype=jnp.int32) * tile
    tile_valid = (tile_start < ends[-1]).astype(jnp.int32)
    tile_group = jnp.sum(tile_start[:, None] >= ends[None, :], axis=1).astype(jnp.int32)
    last_group = jnp.max(jnp.where(counts > 0, groups, 0))
    tile_group = jnp.where(tile_valid == 1, tile_group, last_group)
    later = (groups[None, :] > groups[:, None]) & (counts > 0)[None, :]
    next_of = jnp.min(jnp.where(later, groups[None, :], N_GROUPS), axis=1)
    next_of = jnp.where(next_of == N_GROUPS, groups, next_of)
    onehot_tile = tile_group[:, None] == groups[None, :]
    tile_next = jnp.sum(jnp.where(onehot_tile, next_of[None, :], 0), axis=1).astype(jnp.int32)
    partly_filled = jnp.any((tile_start[:, None] + tile == ends[None, :])
                            & (padded > counts)[None, :], axis=1)
    zero_tile = jnp.where(tile_valid == 1, partly_filled, True).astype(jnp.int32)
    grp = meta[:, 0].astype(jnp.int32)
    rank = meta[:, 1].astype(jnp.int32)
    dest = rank + jnp.sum(jnp.where(grp[:, None] == groups[None, :], offsets[None, :], 0), axis=-1)
    xs = _dispatch(dest, zero_tile, hn, n_slots)
    ys = _experts(tile_group, tile_next, tile_valid, xs, w_gate, w_up, w_down, layer)
    return _combine(dest, x1, ys)


def _router_weights(w_group, b_group, w_expert, b_expert):
    pad = LANES - N_EXPERTS - N_GROUPS
    w = jnp.concatenate([w_expert, w_group, jnp.zeros((D_MODEL, pad), F32)], axis=-1)
    b = jnp.concatenate([b_expert, b_group, jnp.zeros((pad,), F32)]).reshape(1, LANES)
    w_hi = w.astype(BF16)
    w_lo = (w - w_hi.astype(F32)).astype(BF16)
    return jnp.concatenate([w_hi, w_lo], axis=-1), w_hi, b


def kernel(x, attn_norm, ffn_norm, diff_w_in, diff_q_norm, diff_k_norm, diff_lambda_q1, diff_lambda_k1, diff_lambda_q2, diff_lambda_k2, diff_subln, diff_w_out, mla_w_a, mla_q_a_norm, mla_kv_a_norm, mla_w_qb, mla_w_kvb, mla_q_norm, mla_k_norm, mla_w_out, moe_w_group, moe_b_group, moe_w_expert, moe_b_expert, moe_w_gate, moe_w_up, moe_w_down):
    b, s, d = x.shape
    assert d == D_MODEL and s % Q_TILE == 0 and s % ROW_TILE == 0 and (b * s) % COPY_TILE == 0
    t = b * s
    rope_cos, rope_sin = _rope_tables(s)
    row = lambda v: v.astype(F32).reshape(1, -1)
    x2d = x.reshape(t, d)

    lambda_init = 0.8 - 0.6 * math.exp(-0.3 * 0)
    w_in = diff_w_in[0]
    w_in = jnp.concatenate([_diff_head_layout(w_in[:, :DIFF_QK_WIDTH]),
                            _diff_head_layout(w_in[:, DIFF_QK_WIDTH:2 * DIFF_QK_WIDTH]),
                            w_in[:, 2 * DIFF_QK_WIDTH:]], axis=-1).astype(BF16)
    q, k, vt = _diff_qkv(x2d, row(attn_norm[0]), w_in,
                         _diff_gain_layout(diff_q_norm[0], DIFF_HEAD_DIM ** -0.5 * LOG2E),
                         _diff_gain_layout(diff_k_norm[0], 1.0),
                         rope_cos, rope_sin, _segment_matrix(), s)
    shp = (b, s, DIFF_QK_WIDTH)
    o = _flash(q.reshape(shp), k.reshape(shp), vt, DIFF_HEADS, LANES, Q_TILE,
               diff_params=(row(diff_lambda_q1[0]), row(diff_lambda_k1[0]),
                            row(diff_lambda_q2[0]), row(diff_lambda_k2[0]),
                            row(diff_subln[0])),
               lambda_init=lambda_init)
    wr_both, wr_hi, b_r = _router_weights(moe_w_group[0], moe_b_group[0],
                                          moe_w_expert[0], moe_b_expert[0])
    x1, hn, meta, cnt = _post_attn(o.reshape(t, d), x2d, diff_w_out[0].astype(BF16),
                                   row(ffn_norm[0]), wr_both, wr_hi, b_r)
    x2d = _moe(x1, hn, meta, cnt, moe_w_gate, moe_w_up, moe_w_down, 0)

    w_a = mla_w_a[0]
    split = MLA_Q_LORA + MLA_KV_LORA
    w_a = jnp.concatenate([w_a[:, :split], _pad_rope(w_a[:, split:])], axis=-1).astype(BF16)
    w_qb = _mla_qk_layout(mla_w_qb[0].reshape(MLA_Q_LORA, MLA_HEADS, MLA_QK_DIM))
    w_qb = w_qb.reshape(MLA_Q_LORA, MLA_HEADS * MLA_QK_PAD).astype(BF16)
    w_kvb = mla_w_kvb[0].reshape(MLA_KV_LORA, MLA_HEADS, 2 * MLA_NOPE)
    w_kvb = jnp.concatenate([w_kvb[..., :MLA_NOPE].reshape(MLA_KV_LORA, -1),
                             w_kvb[..., MLA_NOPE:].reshape(MLA_KV_LORA, -1)], axis=-1).astype(BF16)
    q_gain = _mla_qk_layout(mla_q_norm[0].astype(F32) * (MLA_QK_DIM ** -0.5 * LOG2E)).reshape(1, -1)
    k_gain = _mla_qk_layout(mla_k_norm[0].astype(F32)).reshape(1, -1)
    q, k, vt = _mla_proj(x2d, row(attn_norm[1]), w_a, row(mla_q_a_norm[0]), row(mla_kv_a_norm[0]),
                        w_qb, w_kvb, q_gain, k_gain, rope_cos, rope_sin, s)
    qk_shp = (b, s, MLA_HEADS * MLA_QK_PAD)
    o = _flash(q.reshape(qk_shp), k.reshape(qk_shp), vt, MLA_HEADS, MLA_QK_PAD, Q_TILE)
    wr_both, wr_hi, b_r = _router_weights(moe_w_group[1], moe_b_group[1],
                                          moe_w_expert[1], moe_b_expert[1])
    x1, hn, meta, cnt = _post_attn(o.reshape(t, d), x2d, mla_w_out[0].astype(BF16),
                                   row(ffn_norm[1]), wr_both, wr_hi, b_r)
    x2d = _moe(x1, hn, meta, cnt, moe_w_gate, moe_w_up, moe_w_down, 1)
    return x2d.reshape(b, s, d)
```

```python
import functools
import math

import jax
import jax.numpy as jnp
from jax import lax
from jax.experimental import pallas as pl
from jax.experimental.pallas import tpu as pltpu

F32 = jnp.float32
BF16 = jnp.bfloat16

D_MODEL = 1024
ROW_WORDS = D_MODEL + 128
ROPE_THETA = 10000.0
NORM_EPS = 1e-6
LANES = 128
MXU_DIM = 256

DIFF_HEAD_DIM = 64
DIFF_HEADS = 8
DIFF_QK_WIDTH = 1024

MLA_NOPE = 128
MLA_ROPE = 64
MLA_QK_DIM = 192
MLA_QK_PAD = 256
MLA_HEADS = 8
MLA_Q_LORA = 384
MLA_KV_LORA = 256

N_GROUPS = 8
EXPERTS_PER_GROUP = 8
N_EXPERTS = 64
TOP_K = 2
EXPERT_FF = 256

ROW_TILE = 512
Q_TILE = 1024
FLASH_CHUNK = 512
V_ROWS = 144
LOG2E = math.log2(math.e)
EXPERT_TILE = 256
PAIRS_PER_ITER = {2: 8, 4: 4}
PROJ_PARTS = 2
COPY_TILE = 1024
DISPATCH_TILE = 2048
ISSUE_UNROLL = 8
VMEM_LIMIT = 56 * 1024 * 1024


def _cparams(sem):
    return pltpu.CompilerParams(dimension_semantics=sem, vmem_limit_bytes=VMEM_LIMIT)


def _rms(x, gain):
    return x * lax.rsqrt(jnp.mean(x * x, axis=-1, keepdims=True) + NORM_EPS) * gain


def _store_v_transposed(v, vt_ref, heads, cols=slice(None)):
    rows = v.shape[0]
    ones = jnp.ones((V_ROWS - LANES, rows), BF16)
    for hd in range(heads):
        lo = hd * V_ROWS
        vt_ref[0, 0, lo:lo + LANES, cols] = v[:, hd * LANES:(hd + 1) * LANES].T.astype(BF16)
        vt_ref[0, 0, lo + LANES:lo + V_ROWS, cols] = ones


def _rope128(u, cos, sin):
    return u * cos + pltpu.roll(u, 64, 1) * sin


def _diff_qkv_kernel(x_ref, g_ref, w_ref, qg_ref, kg_ref, cos_ref, sin_ref, seg_ref,
                     q_ref, k_ref, vt_ref):
    tm = x_ref.shape[0]
    seg = seg_ref[...]

    def project(rows):
        h = _rms(x_ref[rows, :], g_ref[...])
        return jnp.dot(h.astype(BF16), w_ref[...], preferred_element_type=F32)

    def heads(rows, qkv):
        cos = cos_ref[rows, :]
        sin = sin_ref[rows, :]
        for off, gain_ref, out_ref in ((0, qg_ref, q_ref), (DIFF_QK_WIDTH, kg_ref, k_ref)):
            for c in range(DIFF_QK_WIDTH // MXU_DIM):
                t = qkv[:, off + c * MXU_DIM: off + (c + 1) * MXU_DIM]
                ss = jnp.dot((t * t).astype(BF16), seg, preferred_element_type=F32)
                tn = t * lax.rsqrt(ss * (1.0 / DIFF_HEAD_DIM) + NORM_EPS) * gain_ref[...]
                for half in range(MXU_DIM // LANES):
                    u = tn[:, half * LANES:(half + 1) * LANES]
                    lo = c * MXU_DIM + half * LANES
                    out_ref[rows, lo:lo + LANES] = _rope128(u, cos, sin).astype(BF16)
        _store_v_transposed(qkv[:, 2 * DIFF_QK_WIDTH:], vt_ref, DIFF_HEADS, rows)

    part = tm // PROJ_PARTS
    rows = [pl.ds(p * part, part) for p in range(PROJ_PARTS)]
    staged = project(rows[0])
    for p in range(PROJ_PARTS):
        upcoming = project(rows[p + 1]) if p + 1 < PROJ_PARTS else None
        heads(rows[p], staged)
        staged = upcoming


def _diff_qkv(x2d, gain, w_in, q_gain, k_gain, cos, sin, seg, seq):
    t = x2d.shape[0]
    tm = ROW_TILE
    pos_tiles = seq // tm
    row = lambda i: (i, 0)
    const = lambda i: (0, 0)
    out = jax.ShapeDtypeStruct((t, DIFF_QK_WIDTH), BF16)
    vt_out = jax.ShapeDtypeStruct((t // seq, pos_tiles, DIFF_HEADS * V_ROWS, tm), BF16)
    vt_spec = pl.BlockSpec((1, 1, DIFF_HEADS * V_ROWS, tm),
                           lambda i: (i // pos_tiles, i % pos_tiles, 0, 0))
    return pl.pallas_call(
        _diff_qkv_kernel,
        grid=(t // tm,),
        in_specs=[
            pl.BlockSpec((tm, D_MODEL), row),
            pl.BlockSpec((1, D_MODEL), const),
            pl.BlockSpec(w_in.shape, const),
            pl.BlockSpec((1, MXU_DIM), const),
            pl.BlockSpec((1, MXU_DIM), const),
            pl.BlockSpec((tm, LANES), lambda i: (i % pos_tiles, 0)),
            pl.BlockSpec((tm, LANES), lambda i: (i % pos_tiles, 0)),
            pl.BlockSpec((MXU_DIM, MXU_DIM), const),
        ],
        out_specs=[pl.BlockSpec((tm, DIFF_QK_WIDTH), row)] * 2 + [vt_spec],
        out_shape=[out, out, vt_out],
        compiler_params=_cparams(("parallel",)),
        name="diff_qkv",
    )(x2d, gain, w_in, q_gain, k_gain, cos, sin, seg)


def _mla_proj_kernel(x_ref, g_ref, wa_ref, qag_ref, kvag_ref, wqb_ref, wkvb_ref,
                     qg_ref, kg_ref, cos_ref, sin_ref, q_ref, k_ref, vt_ref):
    tm = x_ref.shape[0]
    qg = qg_ref[...]
    kg = kg_ref[...]
    inv_d = 1.0 / MLA_QK_DIM

    def low_rank(rows):
        h = _rms(x_ref[rows, :], g_ref[...])
        return jnp.dot(h.astype(BF16), wa_ref[...], preferred_element_type=F32)

    def up_project(a):
        cq = _rms(a[:, :MLA_Q_LORA], qag_ref[...])
        ckv = _rms(a[:, MLA_Q_LORA:MLA_Q_LORA + MLA_KV_LORA], kvag_ref[...])
        q = jnp.dot(cq.astype(BF16), wqb_ref[...], preferred_element_type=F32)
        kv = jnp.dot(ckv.astype(BF16), wkvb_ref[...], preferred_element_type=F32)
        return q, kv, a[:, MLA_Q_LORA + MLA_KV_LORA:]

    def heads(rows, q, kv, kpe):
        cos = cos_ref[rows, :]
        sin = sin_ref[rows, :]
        kpe_ss = jnp.sum(kpe * kpe, axis=-1, keepdims=True)
        kpe_rot = _rope128(kpe * kg[:, MLA_NOPE:], cos, sin)
        for hd in range(MLA_HEADS):
            lo = hd * MLA_QK_PAD
            qh = q[:, lo:lo + MLA_QK_PAD]
            rq = lax.rsqrt(jnp.sum(qh * qh, axis=-1, keepdims=True) * inv_d + NORM_EPS)
            q_ref[rows, lo:lo + MLA_NOPE] = (qh[:, :MLA_NOPE] * rq * qg[:, :MLA_NOPE]).astype(BF16)
            q_ref[rows, lo + MLA_NOPE:lo + MLA_QK_PAD] = _rope128(
                qh[:, MLA_NOPE:] * rq * qg[:, MLA_NOPE:], cos, sin).astype(BF16)
            kn = kv[:, hd * MLA_NOPE:(hd + 1) * MLA_NOPE]
            rk = lax.rsqrt((jnp.sum(kn * kn, axis=-1, keepdims=True) + kpe_ss) * inv_d + NORM_EPS)
            k_ref[rows, lo:lo + MLA_NOPE] = (kn * rk * kg[:, :MLA_NOPE]).astype(BF16)
            k_ref[rows, lo + MLA_NOPE:lo + MLA_QK_PAD] = (kpe_rot * rk).astype(BF16)
        _store_v_transposed(kv[:, MLA_HEADS * MLA_NOPE:], vt_ref, MLA_HEADS, rows)

    part = tm // PROJ_PARTS
    rows = [pl.ds(p * part, part) for p in range(PROJ_PARTS)]
    staged = up_project(low_rank(rows[0]))
    for p in range(PROJ_PARTS):
        upcoming = up_project(low_rank(rows[p + 1])) if p + 1 < PROJ_PARTS else None
        heads(rows[p], *staged)
        staged = upcoming


def _mla_proj(x2d, gain, w_a, qa_gain, kva_gain, w_qb, w_kvb, q_gain, k_gain, cos, sin, seq):
    t = x2d.shape[0]
    tm = ROW_TILE
    pos_tiles = seq // tm
    row = lambda i: (i, 0)
    const = lambda i: (0, 0)
    qk_out = jax.ShapeDtypeStruct((t, MLA_HEADS * MLA_QK_PAD), BF16)
    vt_out = jax.ShapeDtypeStruct((t // seq, pos_tiles, MLA_HEADS * V_ROWS, tm), BF16)
    return pl.pallas_call(
        _mla_proj_kernel,
        grid=(t // tm,),
        in_specs=[
            pl.BlockSpec((tm, D_MODEL), row),
            pl.BlockSpec((1, D_MODEL), const),
            pl.BlockSpec(w_a.shape, const),
            pl.BlockSpec((1, MLA_Q_LORA), const),
            pl.BlockSpec((1, MLA_KV_LORA), const),
            pl.BlockSpec(w_qb.shape, const),
            pl.BlockSpec(w_kvb.shape, const),
            pl.BlockSpec((1, MLA_QK_PAD), const),
            pl.BlockSpec((1, MLA_QK_PAD), const),
            pl.BlockSpec((tm, LANES), lambda i: (i % pos_tiles, 0)),
            pl.BlockSpec((tm, LANES), lambda i: (i % pos_tiles, 0)),
        ],
        out_specs=[pl.BlockSpec((tm, MLA_HEADS * MLA_QK_PAD), row),
                   pl.BlockSpec((tm, MLA_HEADS * MLA_QK_PAD), row),
                   pl.BlockSpec((1, 1, MLA_HEADS * V_ROWS, tm),
                                lambda i: (i // pos_tiles, i % pos_tiles, 0, 0))],
        out_shape=[qk_out, qk_out, vt_out],
        compiler_params=_cparams(("parallel",)),
        name="mla_proj",
    )(x2d, gain, w_a, qa_gain, kva_gain, w_qb, w_kvb, q_gain, k_gain, cos, sin)


def _flash_kernel(*refs, diff, lambda_init, tq, tk, n_chunks):
    if diff:
        (lq1_ref, lk1_ref, lq2_ref, lk2_ref, subln_ref, q_ref, k_ref, vt_ref, o_ref,
         qt_ref, m_ref, acc_ref, sa_ref, sb_ref) = refs
    else:
        q_ref, k_ref, vt_ref, o_ref, qt_ref, m_ref, acc_ref, sa_ref, sb_ref = refs
    qi = pl.program_id(2)
    ch = FLASH_CHUNK
    copies = 2 if diff else 1
    chunks_per_copy = n_chunks // copies

    q_t = q_ref[0].astype(F32).T
    if diff:
        feat = lax.broadcasted_iota(jnp.int32, q_t.shape, 0)
        first = ((feat // 32) % 2) == 0
        qt_ref[:, :tq] = jnp.where(first, q_t, 0.0).astype(BF16)
        qt_ref[:, tq:] = jnp.where(first, 0.0, q_t).astype(BF16)
    else:
        qt_ref[...] = q_t.astype(BF16)
    m_ref[...] = jnp.full(m_ref.shape, -jnp.inf, F32)
    acc_ref[...] = jnp.zeros(acc_ref.shape, F32)

    def key_tile(j):
        return k_ref[0, pl.ds(pl.multiple_of(j * tk, tk), tk), :]

    def scores(j, s_ref, chunks):
        k = key_tile(j)
        for c in chunks:
            s_ref[c] = jnp.dot(k, qt_ref[:, c * ch:(c + 1) * ch], preferred_element_type=F32)

    def accumulate(j, s_ref, chunks, masked=()):
        vt = vt_ref[0, j]
        for c in chunks:
            s = s_ref[c]
            if c in masked:
                key = lax.broadcasted_iota(jnp.int32, (tk, ch), 0)
                qry = lax.broadcasted_iota(jnp.int32, (tk, ch), 1)
                s = jnp.where(key <= qry, s, -jnp.inf)
            m_prev = m_ref[c]
            m_new = jnp.maximum(m_prev, jnp.max(s, axis=0, keepdims=True))
            alpha = jnp.exp2(m_prev - m_new)
            p = jnp.exp2(s - m_new).astype(BF16)
            acc_ref[c] = alpha * acc_ref[c] + jnp.dot(vt, p, preferred_element_type=F32)
            m_ref[c] = m_new

    every = tuple(range(n_chunks))
    early = tuple(c for c in every if c % chunks_per_copy == 0)
    late = tuple(c for c in every if c % chunks_per_copy == 1)
    scores(0, sa_ref, every)

    def pair(j):
        scores(j + 1, sb_ref, every)
        accumulate(j, sa_ref, every)
        scores(j + 2, sa_ref, every)
        accumulate(j + 1, sb_ref, every)

    def pairs(first_pair, count):
        for r in range(count):
            pair(2 * (first_pair + r))

    per_iter = PAIRS_PER_ITER[n_chunks]

    def main_body(i, carry):
        pairs(per_iter * i, per_iter)
        return carry

    lax.fori_loop(0, qi // per_iter, main_body, 0)
    block = per_iter // 2
    while block >= 1:
        @pl.when(qi % (2 * block) >= block)
        def _(block=block):
            pairs(qi - qi % (2 * block), block)
        block //= 2

    j = 2 * qi
    scores(j + 1, sb_ref, late)
    accumulate(j, sa_ref, every, masked=early)
    accumulate(j + 1, sb_ref, late, masked=late)

    def normalized(c):
        a = acc_ref[c]
        return a[:LANES] * (1.0 / a[LANES:LANES + 1])

    if diff:
        lam = (jnp.exp(jnp.sum(lq1_ref[...] * lk1_ref[...], axis=-1, keepdims=True))
               - jnp.exp(jnp.sum(lq2_ref[...] * lk2_ref[...], axis=-1, keepdims=True))
               + lambda_init)
        for c in range(chunks_per_copy):
            o_t = normalized(c) - lam * normalized(c + chunks_per_copy)
            o_t = o_t * lax.rsqrt(jnp.mean(o_t * o_t, axis=0, keepdims=True) + NORM_EPS)
            o = o_t.T * (subln_ref[...] * (1.0 - lambda_init))
            o_ref[0, c * ch:(c + 1) * ch, :] = o.astype(o_ref.dtype)
    else:
        for c in range(n_chunks):
            o_ref[0, c * ch:(c + 1) * ch, :] = normalized(c).T.astype(o_ref.dtype)


def _flash(q, k, vt, heads, dk, tq, diff_params=None, lambda_init=0.0):
    b, s, _ = q.shape
    tk = ROW_TILE
    diff = diff_params is not None
    n_chunks = (2 if diff else 1) * tq // FLASH_CHUNK
    q_spec = pl.BlockSpec((1, tq, dk), lambda bi, hi, qi: (bi, qi, hi))
    k_spec = pl.BlockSpec((1, s, dk), lambda bi, hi, qi: (bi, 0, hi))
    v_spec = pl.BlockSpec((1, s // tk, V_ROWS, tk), lambda bi, hi, qi: (bi, 0, hi, 0))
    o_spec = pl.BlockSpec((1, tq, LANES), lambda bi, hi, qi: (bi, qi, hi))
    small = lambda shape: pl.BlockSpec(shape, lambda bi, hi, qi: (0, 0))
    in_specs = [q_spec, k_spec, v_spec]
    args = [q, k, vt]
    if diff:
        in_specs = [small((1, DIFF_HEAD_DIM))] * 4 + [small((1, LANES))] + in_specs
        args = list(diff_params) + args
    assert tk == FLASH_CHUNK and tq == 2 * tk
    scratch = [pltpu.VMEM((dk, n_chunks * FLASH_CHUNK), BF16),
               pltpu.VMEM((n_chunks, 1, FLASH_CHUNK), F32),
               pltpu.VMEM((n_chunks, V_ROWS, FLASH_CHUNK), F32),
               pltpu.VMEM((n_chunks, tk, FLASH_CHUNK), F32),
               pltpu.VMEM((n_chunks, tk, FLASH_CHUNK), F32)]
    return pl.pallas_call(
        functools.partial(_flash_kernel, diff=diff, lambda_init=lambda_init, tq=tq, tk=tk,
                          n_chunks=n_chunks),
        grid=(b, heads, s // tq),
        in_specs=in_specs,
        out_specs=o_spec,
        out_shape=jax.ShapeDtypeStruct((b, s, heads * LANES), BF16),
        scratch_shapes=scratch,
        compiler_params=_cparams(("parallel", "parallel", "arbitrary")),
        name="diff_flash" if diff else "mla_flash",
    )(*args)


def _post_attn_kernel(o_ref, x_ref, wo_ref, fg_ref, wr_both_ref, wr_hi_ref, br_ref,
                      x1_ref, hn_ref, meta_ref, cnt_ref, carry_ref):
    i = pl.program_id(0)

    @pl.when(i == 0)
    def _():
        carry_ref[...] = jnp.zeros(carry_ref.shape, F32)

    x1 = x_ref[...] + jnp.dot(o_ref[...], wo_ref[...], preferred_element_type=F32)
    x1_ref[...] = x1
    hn = _rms(x1, fg_ref[...])
    hn_ref[:, :D_MODEL] = hn

    hi = hn.astype(BF16)
    lo = (hn - hi.astype(F32)).astype(BF16)
    hi_terms = jnp.dot(hi, wr_both_ref[...], preferred_element_type=F32)
    logits = (hi_terms[:, :LANES] + hi_terms[:, LANES:]
              + jnp.dot(lo, wr_hi_ref[...], preferred_element_type=F32)
              + br_ref[...])
    tm = logits.shape[0]
    lane = lax.broadcasted_iota(jnp.int32, logits.shape, 1)
    lane_f = lane.astype(F32)
    neg = jnp.float32(-jnp.inf)
    big = jnp.float32(1e9)

    is_group = (lane >= N_EXPERTS) & (lane < N_EXPERTS + N_GROUPS)
    gl = jnp.where(is_group, logits, neg)
    gmax = jnp.max(gl, axis=-1, keepdims=True)
    gidx = jnp.min(jnp.where(gl == gmax, lane_f, big), axis=-1, keepdims=True) - N_EXPERTS
    g_w = 1.0 / jnp.sum(jnp.exp(gl - gmax), axis=-1, keepdims=True)

    in_group = (lane < N_EXPERTS) & ((lane // EXPERTS_PER_GROUP).astype(F32) == gidx)
    el = jnp.where(in_group, logits, neg)
    m1 = jnp.max(el, axis=-1, keepdims=True)
    i1 = jnp.min(jnp.where(el == m1, lane_f, big), axis=-1, keepdims=True)
    el2 = jnp.where(lane_f == i1, neg, el)
    m2 = jnp.max(el2, axis=-1, keepdims=True)
    i2 = jnp.min(jnp.where(el2 == m2, lane_f, big), axis=-1, keepdims=True)
    d = jnp.exp(m2 - m1)
    w1 = g_w / (1.0 + d)
    w2 = g_w * d / (1.0 + d)

    first_expert = gidx * EXPERTS_PER_GROUP
    wv = (jnp.where(lane_f == i1 - first_expert, w1, 0.0)
          + jnp.where(lane_f == i2 - first_expert, w2, 0.0))
    hn_ref[:, D_MODEL:] = wv

    in_grp = lane_f == gidx + N_EXPERTS
    oh = jnp.where(in_grp, 1.0, 0.0)
    r = lax.broadcasted_iota(jnp.int32, (tm, tm), 0)
    c = lax.broadcasted_iota(jnp.int32, (tm, tm), 1)
    lower = jnp.where(c < r, 1.0, 0.0).astype(BF16)
    prefix = jnp.dot(lower, oh.astype(BF16), preferred_element_type=F32) + carry_ref[...]
    rank = jnp.sum(jnp.where(in_grp, prefix, 0.0), axis=-1, keepdims=True)
    carry = carry_ref[...] + jnp.sum(oh, axis=0, keepdims=True)
    carry_ref[...] = carry
    cnt_ref[...] = jnp.broadcast_to(carry, cnt_ref.shape)
    meta_ref[...] = jnp.where(lane == 0, gidx, jnp.where(lane == 1, rank, 0.0))


def _post_attn(o2d, x2d, w_out, ffn_gain, wr_both, wr_hi, b_r):
    t = x2d.shape[0]
    tm = ROW_TILE
    row = lambda i: (i, 0)
    const = lambda i: (0, 0)
    return pl.pallas_call(
        _post_attn_kernel,
        grid=(t // tm,),
        in_specs=[
            pl.BlockSpec((tm, D_MODEL), row),
            pl.BlockSpec((tm, D_MODEL), row),
            pl.BlockSpec((D_MODEL, D_MODEL), const),
            pl.BlockSpec((1, D_MODEL), const),
            pl.BlockSpec((D_MODEL, 2 * LANES), const),
            pl.BlockSpec((D_MODEL, LANES), const),
            pl.BlockSpec((1, LANES), const),
        ],
        out_specs=[pl.BlockSpec((tm, D_MODEL), row),
                   pl.BlockSpec((tm, ROW_WORDS), row),
                   pl.BlockSpec((tm, LANES), row),
                   pl.BlockSpec((8, LANES), const)],
        out_shape=[jax.ShapeDtypeStruct((t, D_MODEL), F32),
                   jax.ShapeDtypeStruct((t, ROW_WORDS), F32),
                   jax.ShapeDtypeStruct((t, LANES), F32),
                   jax.ShapeDtypeStruct((8, LANES), F32)],
        scratch_shapes=[pltpu.VMEM((1, LANES), F32)],
        compiler_params=_cparams(("arbitrary",)),
        name="post_attn_router",
    )(o2d, x2d, w_out, ffn_gain, wr_both, wr_hi, b_r)


def _row_copy(src, dst, sem):
    return pltpu.make_async_copy(src, dst, sem)


def _dispatch_kernel(dest_ref, zero_tile_ref, hn_ref, xs_ref, zeros_ref, sem, pad_sem):
    tm = hn_ref.shape[0]
    tile = zeros_ref.shape[0]
    step = pl.program_id(0)
    base = step * tm

    @pl.when(step == 0)
    def _():
        zeros_ref[...] = jnp.zeros(zeros_ref.shape, zeros_ref.dtype)

        def tile_copy(tl):
            row0 = pl.multiple_of(tl * tile, tile)
            return _row_copy(zeros_ref, xs_ref.at[pl.ds(row0, tile)], pad_sem)

        def start_tile(tl, carry):
            @pl.when(zero_tile_ref[tl] == 1)
            def _():
                tile_copy(tl).start()
            return carry

        def wait_tile(tl, carry):
            @pl.when(zero_tile_ref[tl] == 1)
            def _():
                tile_copy(tl).wait()
            return carry

        n_tiles = xs_ref.shape[0] // tile
        lax.fori_loop(0, n_tiles, start_tile, 0)
        lax.fori_loop(0, n_tiles, wait_tile, 0)

    def issue(t, carry):
        d = dest_ref[base + t]
        _row_copy(hn_ref.at[pl.ds(t, 1)], xs_ref.at[pl.ds(d, 1)], sem).start()
        return carry

    lax.fori_loop(0, tm, issue, 0, unroll=ISSUE_UNROLL)
    _row_copy(hn_ref, xs_ref.at[pl.ds(0, tm)], sem).wait()


def _dispatch(dest, zero_tile, hn, n_slots):
    t = hn.shape[0]
    tm = DISPATCH_TILE
    grid_spec = pltpu.PrefetchScalarGridSpec(
        num_scalar_prefetch=2,
        grid=(t // tm,),
        in_specs=[pl.BlockSpec((tm, ROW_WORDS), lambda i, *_: (i, 0))],
        out_specs=pl.BlockSpec(memory_space=pl.ANY),
        scratch_shapes=[pltpu.VMEM((EXPERT_TILE, ROW_WORDS), F32),
                        pltpu.SemaphoreType.DMA(()),
                        pltpu.SemaphoreType.DMA(())],
    )
    return pl.pallas_call(
        _dispatch_kernel,
        grid_spec=grid_spec,
        out_shape=jax.ShapeDtypeStruct((n_slots, ROW_WORDS), F32),
        compiler_params=_cparams(("arbitrary",)),
        name="moe_dispatch",
    )(dest, zero_tile, hn)


def _expert_kernel(tg_ref, tnext_ref, tv_ref, xs_ref, wg_hbm, wu_hbm, wd_hbm, ys_ref,
                   stage_g, stage_u, stage_d, wg_bf, wu_bf, wd_bf, sems, *, layer):
    i = pl.program_id(0)
    grp = tg_ref[i]
    first = i == 0
    changed = first | (grp != tg_ref[jnp.maximum(i - 1, 0)])
    ff = EXPERT_FF

    def weight_copies(g):
        e0 = g * EXPERTS_PER_GROUP
        return (_row_copy(wg_hbm.at[layer, pl.ds(e0, EXPERTS_PER_GROUP)], stage_g, sems.at[0]),
                _row_copy(wu_hbm.at[layer, pl.ds(e0, EXPERTS_PER_GROUP)], stage_u, sems.at[1]),
                _row_copy(wd_hbm.at[layer, pl.ds(e0, EXPERTS_PER_GROUP)], stage_d, sems.at[2]))

    @pl.when(first)
    def _():
        for cp in weight_copies(grp):
            cp.start()

    @pl.when(changed)
    def _():
        for cp in weight_copies(grp):
            cp.wait()
        for e in range(EXPERTS_PER_GROUP):
            wg_bf[:, e * ff:(e + 1) * ff] = stage_g[e].astype(BF16)
            wu_bf[:, e * ff:(e + 1) * ff] = stage_u[e].astype(BF16)
            wd_bf[e * ff:(e + 1) * ff, :] = stage_d[e].astype(BF16)
        nxt = tnext_ref[i]

        @pl.when(nxt != grp)
        def _():
            for cp in weight_copies(nxt):
                cp.start()

    @pl.when(tv_ref[i] == 1)
    def _():
        x = xs_ref[:, :D_MODEL].astype(BF16)
        wv = xs_ref[:, D_MODEL:]
        g = jnp.dot(x, wg_bf[...], preferred_element_type=F32)
        u = jnp.dot(x, wu_bf[...], preferred_element_type=F32)
        hid = g * jax.nn.sigmoid(g) * u
        hid = jnp.concatenate(
            [(hid[:, e * ff:(e + 1) * ff] * wv[:, e:e + 1]).astype(BF16)
             for e in range(EXPERTS_PER_GROUP)], axis=-1)
        y = jnp.dot(hid, wd_bf[...], preferred_element_type=F32)
        ys_ref[...] = y

    @pl.when(tv_ref[i] == 0)
    def _():
        ys_ref[...] = jnp.zeros(ys_ref.shape, ys_ref.dtype)


def _experts(tile_group, tile_next, tile_valid, xs, w_gate, w_up, w_down, layer):
    p = xs.shape[0]
    tile = EXPERT_TILE
    group_ff = EXPERTS_PER_GROUP * EXPERT_FF
    grid_spec = pltpu.PrefetchScalarGridSpec(
        num_scalar_prefetch=3,
        grid=(p // tile,),
        in_specs=[
            pl.BlockSpec((tile, ROW_WORDS), lambda i, *_: (i, 0)),
            pl.BlockSpec(memory_space=pl.ANY),
            pl.BlockSpec(memory_space=pl.ANY),
            pl.BlockSpec(memory_space=pl.ANY),
        ],
        out_specs=pl.BlockSpec((tile, D_MODEL), lambda i, *_: (i, 0)),
        scratch_shapes=[pltpu.VMEM((EXPERTS_PER_GROUP, D_MODEL, EXPERT_FF), F32),
                        pltpu.VMEM((EXPERTS_PER_GROUP, D_MODEL, EXPERT_FF), F32),
                        pltpu.VMEM((EXPERTS_PER_GROUP, EXPERT_FF, D_MODEL), F32),
                        pltpu.VMEM((D_MODEL, group_ff), BF16),
                        pltpu.VMEM((D_MODEL, group_ff), BF16),
                        pltpu.VMEM((group_ff, D_MODEL), BF16),
                        pltpu.SemaphoreType.DMA((3,))],
    )
    return pl.pallas_call(
        functools.partial(_expert_kernel, layer=layer),
        grid_spec=grid_spec,
        out_shape=jax.ShapeDtypeStruct((p, D_MODEL), F32),
        compiler_params=_cparams(("arbitrary",)),
        name="moe_experts",
    )(tile_group, tile_next, tile_valid, xs, w_gate, w_up, w_down)


def _combine_kernel(dest_ref, x1_ref, ys_ref, out_ref, buf_ref, sems):
    tm = x1_ref.shape[0]
    step = pl.program_id(0)
    slot = step % 2

    def gather(tile_index, into):
        base = tile_index * tm

        def issue(t, carry):
            d = dest_ref[base + t]
            _row_copy(ys_ref.at[pl.ds(d, 1)], buf_ref.at[into, pl.ds(t, 1)], sems.at[into]).start()
            return carry

        lax.fori_loop(0, tm, issue, 0, unroll=ISSUE_UNROLL)

    @pl.when(step == 0)
    def _():
        gather(0, 0)

    @pl.when(step + 1 < pl.num_programs(0))
    def _():
        gather(step + 1, 1 - slot)

    _row_copy(ys_ref.at[pl.ds(0, tm)], buf_ref.at[slot], sems.at[slot]).wait()

    out_ref[...] = x1_ref[...] + buf_ref[slot]


def _combine(dest, x1, ys):
    t = x1.shape[0]
    tm = COPY_TILE
    grid_spec = pltpu.PrefetchScalarGridSpec(
        num_scalar_prefetch=1,
        grid=(t // tm,),
        in_specs=[pl.BlockSpec((tm, D_MODEL), lambda i, dest: (i, 0)),
                  pl.BlockSpec(memory_space=pl.ANY)],
        out_specs=pl.BlockSpec((tm, D_MODEL), lambda i, dest: (i, 0)),
        scratch_shapes=[pltpu.VMEM((2, tm, D_MODEL), F32),
                        pltpu.SemaphoreType.DMA((2,))],
    )
    return pl.pallas_call(
        _combine_kernel,
        grid_spec=grid_spec,
        out_shape=jax.ShapeDtypeStruct((t, D_MODEL), F32),
        compiler_params=_cparams(("arbitrary",)),
        name="moe_combine",
    )(dest, x1, ys)


def _rope_tables(seq):
    inv_freq = 1.0 / (ROPE_THETA ** (jnp.arange(0, DIFF_HEAD_DIM, 2, dtype=F32) / DIFF_HEAD_DIM))
    ang = jnp.arange(seq, dtype=F32)[:, None] * jnp.tile(inv_freq, 4)[None, :]
    sign = jnp.where(jnp.arange(LANES) < LANES // 2, -1.0, 1.0).astype(F32)
    return jnp.cos(ang), jnp.sin(ang) * sign[None, :]


def _diff_head_layout(w):
    lead = w.shape[:-1]
    w = w.reshape(*lead, DIFF_HEADS, 2, 2, DIFF_HEAD_DIM // 2)
    w = jnp.swapaxes(w, -3, -2)
    return w.reshape(*lead, DIFF_QK_WIDTH)


def _diff_gain_layout(g, scale):
    g = (g.astype(F32) * scale).reshape(2, DIFF_HEAD_DIM // 2)
    g = jnp.broadcast_to(g[:, None, :], (2, 2, DIFF_HEAD_DIM // 2))
    return jnp.tile(g.reshape(1, LANES), (1, MXU_DIM // LANES))


def _pad_rope(w):
    half = MLA_ROPE // 2
    z = jnp.zeros(w.shape[:-1] + (half,), w.dtype)
    return jnp.concatenate([w[..., :half], z, w[..., half:], z], axis=-1)


def _mla_qk_layout(w):
    return jnp.concatenate([w[..., :MLA_NOPE], _pad_rope(w[..., MLA_NOPE:])], axis=-1)


def _segment_matrix():
    lane = jnp.arange(MXU_DIM)
    key = (lane // LANES) * 2 + (lane // 32) % 2
    return (key[:, None] == key[None, :]).astype(BF16)


def _moe(x1, hn, meta, cnt, w_gate, w_up, w_down, layer):
    t = x1.shape[0]
    tile = EXPERT_TILE
    n_slots = t + N_GROUPS * tile
    n_tiles = n_slots // tile
    groups = jnp.arange(N_GROUPS, dtype=jnp.int32)
    counts = cnt[0, N_EXPERTS:N_EXPERTS + N_GROUPS].astype(jnp.int32)
    padded = ((counts + tile - 1) // tile) * tile
    ends = jnp.cumsum(padded)
    offsets = ends - padded
    tile_start = jnp.arange(n_tiles, dtype=jnp.int32) * tile
    tile_valid = (tile_start < ends[-1]).astype(jnp.int32)
    tile_group = jnp.sum(tile_start[:, None] >= ends[None, :], axis=1).astype(jnp.int32)
    last_group = jnp.max(jnp.where(counts > 0, groups, 0))
    tile_group = jnp.where(tile_valid == 1, tile_group, last_group)
    later = (groups[None, :] > groups[:, None]) & (counts > 0)[None, :]
    next_of = jnp.min(jnp.where(later, groups[None, :], N_GROUPS), axis=1)
    next_of = jnp.where(next_of == N_GROUPS, groups, next_of)
    onehot_tile = tile_group[:, None] == groups[None, :]
    tile_next = jnp.sum(jnp.where(onehot_tile, next_of[None, :], 0), axis=1).astype(jnp.int32)
    partly_filled = jnp.any((tile_start[:, None] + tile == ends[None, :])
                            & (padded > counts)[None, :], axis=1)
    zero_tile = jnp.where(tile_valid == 1, partly_filled, True).astype(jnp.int32)
    grp = meta[:, 0].astype(jnp.int32)
    rank = meta[:, 1].astype(jnp.int32)
    dest = rank + jnp.sum(jnp.where(grp[:, None] == groups[None, :], offsets[None, :], 0), axis=-1)
    xs = _dispatch(dest, zero_tile, hn, n_slots)
    ys = _experts(tile_group, tile_next, tile_valid, xs, w_gate, w_up, w_down, layer)
    return _combine(dest, x1, ys)


def _router_weights(w_group, b_group, w_expert, b_expert):
    pad = LANES - N_EXPERTS - N_GROUPS
    w = jnp.concatenate([w_expert, w_group, jnp.zeros((D_MODEL, pad), F32)], axis=-1)
    b = jnp.concatenate([b_expert, b_group, jnp.zeros((pad,), F32)]).reshape(1, LANES)
    w_hi = w.astype(BF16)
    w_lo = (w - w_hi.astype(F32)).astype(BF16)
    return jnp.concatenate([w_hi, w_lo], axis=-1), w_hi, b


def kernel(x, attn_norm, ffn_norm, diff_w_in, diff_q_norm, diff_k_norm, diff_lambda_q1, diff_lambda_k1, diff_lambda_q2, diff_lambda_k2, diff_subln, diff_w_out, mla_w_a, mla_q_a_norm, mla_kv_a_norm, mla_w_qb, mla_w_kvb, mla_q_norm, mla_k_norm, mla_w_out, moe_w_group, moe_b_group, moe_w_expert, moe_b_expert, moe_w_gate, moe_w_up, moe_w_down):
    b, s, d = x.shape
    assert d == D_MODEL and s % Q_TILE == 0 and s % ROW_TILE == 0 and (b * s) % DISPATCH_TILE == 0
    t = b * s
    rope_cos, rope_sin = _rope_tables(s)
    row = lambda v: v.astype(F32).reshape(1, -1)
    x2d = x.reshape(t, d)

    lambda_init = 0.8 - 0.6 * math.exp(-0.3 * 0)
    w_in = diff_w_in[0]
    w_in = jnp.concatenate([_diff_head_layout(w_in[:, :DIFF_QK_WIDTH]),
                            _diff_head_layout(w_in[:, DIFF_QK_WIDTH:2 * DIFF_QK_WIDTH]),
                            w_in[:, 2 * DIFF_QK_WIDTH:]], axis=-1).astype(BF16)
    q, k, vt = _diff_qkv(x2d, row(attn_norm[0]), w_in,
                         _diff_gain_layout(diff_q_norm[0], DIFF_HEAD_DIM ** -0.5 * LOG2E),
                         _diff_gain_layout(diff_k_norm[0], 1.0),
                         rope_cos, rope_sin, _segment_matrix(), s)
    shp = (b, s, DIFF_QK_WIDTH)
    o = _flash(q.reshape(shp), k.reshape(shp), vt, DIFF_HEADS, LANES, Q_TILE,
               diff_params=(row(diff_lambda_q1[0]), row(diff_lambda_k1[0]),
                            row(diff_lambda_q2[0]), row(diff_lambda_k2[0]),
                            row(diff_subln[0])),
               lambda_init=lambda_init)
    wr_both, wr_hi, b_r = _router_weights(moe_w_group[0], moe_b_group[0],
                                          moe_w_expert[0], moe_b_expert[0])
    x1, hn, meta, cnt = _post_attn(o.reshape(t, d), x2d, diff_w_out[0].astype(BF16),
                                   row(ffn_norm[0]), wr_both, wr_hi, b_r)
    x2d = _moe(x1, hn, meta, cnt, moe_w_gate, moe_w_up, moe_w_down, 0)

    w_a = mla_w_a[0]
    split = MLA_Q_LORA + MLA_KV_LORA
    w_a = jnp.concatenate([w_a[:, :split], _pad_rope(w_a[:, split:])], axis=-1).astype(BF16)
    w_qb = _mla_qk_layout(mla_w_qb[0].reshape(MLA_Q_LORA, MLA_HEADS, MLA_QK_DIM))
    w_qb = w_qb.reshape(MLA_Q_LORA, MLA_HEADS * MLA_QK_PAD).astype(BF16)
    w_kvb = mla_w_kvb[0].reshape(MLA_KV_LORA, MLA_HEADS, 2 * MLA_NOPE)
    w_kvb = jnp.concatenate([w_kvb[..., :MLA_NOPE].reshape(MLA_KV_LORA, -1),
                             w_kvb[..., MLA_NOPE:].reshape(MLA_KV_LORA, -1)], axis=-1).astype(BF16)
    q_gain = _mla_qk_layout(mla_q_norm[0].astype(F32) * (MLA_QK_DIM ** -0.5 * LOG2E)).reshape(1, -1)
    k_gain = _mla_qk_layout(mla_k_norm[0].astype(F32)).reshape(1, -1)
    q, k, vt = _mla_proj(x2d, row(attn_norm[1]), w_a, row(mla_q_a_norm[0]), row(mla_kv_a_norm[0]),
                        w_qb, w_kvb, q_gain, k_gain, rope_cos, rope_sin, s)
    qk_shp = (b, s, MLA_HEADS * MLA_QK_PAD)
    o = _flash(q.reshape(qk_shp), k.reshape(qk_shp), vt, MLA_HEADS, MLA_QK_PAD, Q_TILE)
    wr_both, wr_hi, b_r = _router_weights(moe_w_group[1], moe_b_group[1],
                                          moe_w_expert[1], moe_b_expert[1])
    x1, hn, meta, cnt = _post_attn(o.reshape(t, d), x2d, mla_w_out[0].astype(BF16),
                                   row(ffn_norm[1]), wr_both, wr_hi, b_r)
    x2d = _moe(x1, hn, meta, cnt, moe_w_gate, moe_w_up, moe_w_down, 1)
    return x2d.reshape(b, s, d)
```
